```python
import jax, jax.numpy as jnp
from jax import lax
import numpy as np

D_MODEL = 1024
BATCH = 2
SEQ = 8192
DEPTH = 2
DEC_BATCH = 128
DEC_SEQ = 8
PAST_LEN = 8192
PAGE_SIZE = 128

D_BR = D_MODEL
N_BRANCH = 3
HEAD_A = 64
H_A = D_BR // HEAD_A
DECAY_LORA = 64
ICLR_LORA = 64
SHIFT_W = 3 * D_BR + DECAY_LORA + ICLR_LORA
GN_EPS = 64e-5
CONV_W = 3
H_C = 8
QK_NOPE = 64
QK_ROPE = 32
V_HEAD = D_BR // H_C
Q_LORA = D_MODEL // 4
KV_LORA = D_MODEL // 4
ROPE_THETA = 10000.0
Q_BLOCK = 128
NORM_EPS = 1e-6
N_IN = SHIFT_W + D_BR + 4 * D_BR + Q_LORA + KV_LORA + QK_ROPE + D_BR + N_BRANCH * D_MODEL

kernel_name = 'hybrid_rwkv7_shortconv_mla_decoder_step'


def rms_norm(x, g):
    xf = x.astype(jnp.float32)
    xf = xf * lax.rsqrt(jnp.mean(xf * xf, axis=-1, keepdims=True) + NORM_EPS)
    return (xf * g.astype(jnp.float32)).astype(x.dtype)


def rope_angles(pos):
    half = QK_ROPE // 2
    freqs = ROPE_THETA ** (-jnp.arange(half, dtype=jnp.float32) / half)
    ang = pos.astype(jnp.float32)[:, None] * freqs[None, :]
    return jnp.cos(ang), jnp.sin(ang)


def apply_rope(x, cos, sin):
    half = QK_ROPE // 2
    xf = x.astype(jnp.float32)
    x1, x2 = xf[..., :half], xf[..., half:]
    return jnp.concatenate([x1 * cos - x2 * sin, x2 * cos + x1 * sin], axis=-1).astype(x.dtype)


def wkv7_scan(r, decay, k, v, a_in, b_in, s0):
    xs = tuple(jnp.moveaxis(t.astype(jnp.float32), 1, 0) for t in (r, decay, k, v, a_in, b_in))

    def step(s, inp):
        r_t, w_t, k_t, v_t, a_t, b_t = inp
        sa = jnp.einsum('bhij,bhj->bhi', s, a_t)
        s = s * w_t[:, :, None, :] + sa[..., None] * b_t[:, :, None, :] + v_t[..., None] * k_t[:, :, None, :]
        return s, jnp.einsum('bhij,bhj->bhi', s, r_t)

    s_fin, ys = lax.scan(step, s0.astype(jnp.float32), xs)
    return jnp.moveaxis(ys, 0, 1), s_fin


def mla_prompt(q_nope, q_pe, ckv, kr, w_uk, w_uv):
    bsz, slen = q_nope.shape[0], q_nope.shape[1]
    k_nope = jnp.einsum('btc,chd->bthd', ckv, w_uk)
    v = jnp.einsum('btc,chv->bthv', ckv, w_uv)
    nb = slen // Q_BLOCK
    qn = jnp.moveaxis(q_nope.reshape(bsz, nb, Q_BLOCK, H_C, QK_NOPE), 1, 0)
    qr = jnp.moveaxis(q_pe.reshape(bsz, nb, Q_BLOCK, H_C, QK_ROPE), 1, 0)
    kpos = jnp.arange(slen)
    scale = (QK_NOPE + QK_ROPE) ** -0.5

    def block(args):
        i, qn_b, qr_b = args
        s = jnp.einsum('bqhd,bthd->bhqt', qn_b, k_nope) + jnp.einsum('bqhr,btr->bhqt', qr_b, kr)
        s = s.astype(jnp.float32) * scale
        qpos = i * Q_BLOCK + jnp.arange(Q_BLOCK)
        s = jnp.where(kpos[None, :] <= qpos[:, None], s, -1e30)
        p = jax.nn.softmax(s, axis=-1).astype(v.dtype)
        return jnp.einsum('bhqt,bthv->bqhv', p, v)

    o = lax.map(block, (jnp.arange(nb), qn, qr))
    return jnp.moveaxis(o, 0, 1).reshape(bsz, slen, H_C, V_HEAD)


def mla_sample(q_nope, q_pe, ckv_new, kr_new, past_ckv, past_kr, w_uk, w_uv):
    past_len, snew = past_ckv.shape[1], q_nope.shape[1]
    ckv_all = jnp.concatenate([past_ckv.astype(ckv_new.dtype), ckv_new], axis=1)
    kr_all = jnp.concatenate([past_kr.astype(kr_new.dtype), kr_new], axis=1)
    q_lat = jnp.einsum('bshd,chd->bshc', q_nope, w_uk)
    s = jnp.einsum('bshc,btc->bhst', q_lat, ckv_all) + jnp.einsum('bshr,btr->bhst', q_pe, kr_all)
    s = s.astype(jnp.float32) * ((QK_NOPE + QK_ROPE) ** -0.5)
    kpos = jnp.arange(past_len + snew)
    qpos = past_len + jnp.arange(snew)
    s = jnp.where(kpos[None, :] <= qpos[:, None], s, -1e30)
    p = jax.nn.softmax(s, axis=-1).astype(ckv_all.dtype)
    o_lat = jnp.einsum('bhst,btc->bshc', p, ckv_all)
    return jnp.einsum('bshc,chv->bshv', o_lat, w_uv)


def hybrid_layer(x, pos, shift_prev, conv_prev, wkv_prev, past_ckv, past_kr, lp):
    bsz, slen = x.shape[0], x.shape[1]
    h = rms_norm(x, lp['norm_pre'])
    proj = h @ lp['w_in']
    sizes = (SHIFT_W, D_BR, D_BR, D_BR, D_BR, D_BR, Q_LORA, KV_LORA, QK_ROPE, D_BR, N_BRANCH * D_MODEL)
    cuts = [int(c) for c in np.cumsum(sizes)[:-1]]
    p_rw, g_a, c_b, c_c, c_x, g_b, q_a, kv_a, k_pe, g_c, merge = jnp.split(proj, cuts, axis=-1)

    prev = jnp.concatenate([shift_prev[:, None].astype(p_rw.dtype), p_rw[:, :-1]], axis=1)
    p_sh = p_rw + (prev - p_rw) * lp['mu_shift']
    new_shift = p_rw[:, -1]
    r, k, v, pw, pa = jnp.split(p_sh, [D_BR, 2 * D_BR, 3 * D_BR, 3 * D_BR + DECAY_LORA], axis=-1)
    wf = (lp['w0'] + jnp.tanh(pw) @ lp['w2']).astype(jnp.float32)
    decay = jnp.exp(-jnp.exp(-jax.nn.softplus(-wf) - 0.5))
    a = jax.nn.sigmoid((lp['a0'] + pa @ lp['a2']).astype(jnp.float32))
    kk = (k * lp['k_k']).astype(jnp.float32).reshape(bsz, slen, H_A, HEAD_A)
    kk = kk / jnp.maximum(jnp.sqrt(jnp.sum(kk * kk, axis=-1, keepdims=True)), 1e-12)
    k = k.astype(jnp.float32) * (1.0 + (a - 1.0) * lp['k_a'].astype(jnp.float32))
    hs = (bsz, slen, H_A, HEAD_A)
    rh = r.astype(jnp.float32).reshape(hs)
    kh = k.reshape(hs)
    vh = v.astype(jnp.float32).reshape(hs)
    ah = a.reshape(hs)
    y, wkv_new = wkv7_scan(rh, decay.reshape(hs), kh, vh, -kk, kk * ah, wkv_prev)
    mu = jnp.mean(y, axis=-1, keepdims=True)
    var = jnp.mean((y - mu) ** 2, axis=-1, keepdims=True)
    y = (y - mu) * lax.rsqrt(var + GN_EPS) * lp['ln_x_w'].astype(jnp.float32).reshape(H_A, HEAD_A) \
        + lp['ln_x_b'].astype(jnp.float32).reshape(H_A, HEAD_A)
    y = y + jnp.sum(rh * kh * lp['r_k'].astype(jnp.float32), axis=-1, keepdims=True) * vh
    out_a = y.reshape(bsz, slen, D_BR).astype(x.dtype) * jax.nn.silu(g_a)

    z = c_c * c_x
    z_ext = jnp.concatenate([conv_prev.astype(z.dtype), z], axis=1)
    cw = lp['conv_w']
    yc = cw[0] * z_ext[:, :-2] + cw[1] * z_ext[:, 1:-1] + cw[2] * z_ext[:, 2:]
    out_b = c_b * yc * jax.nn.silu(g_b)
    new_conv = z_ext[:, -(CONV_W - 1):]

    cos, sin = rope_angles(pos)
    q = jnp.einsum('bsc,chd->bshd', rms_norm(q_a, lp['q_a_norm']), lp['w_uq'])
    q_nope = q[..., :QK_NOPE]
    q_pe = apply_rope(q[..., QK_NOPE:], cos[:, None, :], sin[:, None, :])
    ckv = rms_norm(kv_a, lp['kv_a_norm'])
    kr = apply_rope(k_pe, cos, sin)
    if past_ckv is None:
        o = mla_prompt(q_nope, q_pe, ckv, kr, lp['w_uk'], lp['w_uv'])
    else:
        o = mla_sample(q_nope, q_pe, ckv, kr, past_ckv, past_kr, lp['w_uk'], lp['w_uv'])
    out_c = o.reshape(bsz, slen, D_BR) * jax.nn.silu(g_c)

    branches = jnp.stack([out_a, out_b, out_c], axis=2)
    proj_br = jnp.einsum('bsnc,ncd->bsnd', branches, lp['w_branch'])
    gates = jax.nn.sigmoid(merge.reshape(bsz, slen, N_BRANCH, D_MODEL))
    mixed = jnp.sum(gates * proj_br, axis=2) @ lp['w_out']
    x = x + rms_norm(mixed, lp['norm_post'])
    return x, (ckv, kr, wkv_new.astype(wkv_prev.dtype), new_shift, new_conv)


def setup_inputs(seed: int = 0) -> dict:
    key = jax.random.key(seed)
    ks = jax.random.split(key, 32)
    n_pages = PAST_LEN // PAGE_SIZE
    n_pool = (DEC_BATCH * n_pages * 5) // 4
    f32 = jnp.float32

    def nrm(k, shape, scale):
        return jax.random.normal(k, shape, f32) * scale

    perm = jax.random.permutation(ks[7], n_pool)
    page_table = perm[:DEC_BATCH * n_pages].reshape(DEC_BATCH, n_pages).astype(jnp.int32)
    return {
        'x_prompt': nrm(ks[0], (BATCH, SEQ, D_MODEL), 1.0),
        'x_sample': nrm(ks[1], (DEC_BATCH, DEC_SEQ, D_MODEL), 1.0),
        'cache_ckv': nrm(ks[2], (DEPTH, n_pool, PAGE_SIZE, KV_LORA), 1.0),
        'cache_krope': nrm(ks[3], (DEPTH, n_pool, PAGE_SIZE, QK_ROPE), 1.0),
        'state_wkv': nrm(ks[4], (DEPTH, DEC_BATCH, H_A, HEAD_A, HEAD_A), 0.5),
        'state_shift': nrm(ks[5], (DEPTH, DEC_BATCH, SHIFT_W), 1.0),
        'state_conv': nrm(ks[6], (DEPTH, DEC_BATCH, CONV_W - 1, D_BR), 1.0),
        'page_table': page_table,
        'norm_pre': 1.0 + nrm(ks[8], (DEPTH, D_MODEL), 0.05),
        'norm_post': 1.0 + nrm(ks[9], (DEPTH, D_MODEL), 0.05),
        'w_in': nrm(ks[10], (DEPTH, D_MODEL, N_IN), D_MODEL ** -0.5),
        'mu_shift': jax.random.uniform(ks[11], (DEPTH, SHIFT_W), f32),
        'w0': jax.random.uniform(ks[12], (DEPTH, D_BR), f32, -5.0, -1.0),
        'w2': nrm(ks[13], (DEPTH, DECAY_LORA, D_BR), 0.5 * DECAY_LORA ** -0.5),
        'a0': nrm(ks[14], (DEPTH, D_BR), 0.1),
        'a2': nrm(ks[15], (DEPTH, ICLR_LORA, D_BR), ICLR_LORA ** -0.5),
        'k_k': 0.85 + nrm(ks[16], (DEPTH, D_BR), 0.05),
        'k_a': 1.0 + nrm(ks[17], (DEPTH, D_BR), 0.05),
        'r_k': nrm(ks[18], (DEPTH, H_A, HEAD_A), 0.1),
        'ln_x_w': 1.0 + nrm(ks[19], (DEPTH, D_BR), 0.05),
        'ln_x_b': nrm(ks[20], (DEPTH, D_BR), 0.02),
        'conv_w': nrm(ks[21], (DEPTH, CONV_W, D_BR), CONV_W ** -0.5),
        'q_a_norm': 1.0 + nrm(ks[22], (DEPTH, Q_LORA), 0.05),
        'w_uq': nrm(ks[23], (DEPTH, Q_LORA, H_C, QK_NOPE + QK_ROPE), Q_LORA ** -0.5),
        'kv_a_norm': 1.0 + nrm(ks[24], (DEPTH, KV_LORA), 0.05),
        'w_uk': nrm(ks[25], (DEPTH, KV_LORA, H_C, QK_NOPE), KV_LORA ** -0.5),
        'w_uv': nrm(ks[26], (DEPTH, KV_LORA, H_C, V_HEAD), KV_LORA ** -0.5),
        'w_branch': nrm(ks[27], (DEPTH, N_BRANCH, D_BR, D_MODEL), D_BR ** -0.5),
        'w_out': nrm(ks[28], (DEPTH, D_MODEL, D_MODEL), D_MODEL ** -0.5),
    }


def reference(x_prompt, x_sample, cache_ckv, cache_krope, state_wkv, state_shift, state_conv, page_table,
              norm_pre, norm_post, w_in, mu_shift, w0, w2, a0, a2, k_k, k_a, r_k, ln_x_w, ln_x_b,
              conv_w, q_a_norm, w_uq, kv_a_norm, w_uk, w_uv, w_branch, w_out):
    bp, sp = x_prompt.shape[0], x_prompt.shape[1]
    bd, sd = x_sample.shape[0], x_sample.shape[1]
    past_len = page_table.shape[1] * cache_ckv.shape[2]
    pos_p = jnp.arange(sp)
    pos_s = past_len + jnp.arange(sd)
    dt = x_prompt.dtype
    zero_shift = jnp.zeros((bp, SHIFT_W), dt)
    zero_conv = jnp.zeros((bp, CONV_W - 1, D_BR), dt)
    zero_wkv = jnp.zeros((bp, H_A, HEAD_A, HEAD_A), jnp.float32)
    xp, xs = x_prompt, x_sample
    acc_p = ([], [], [], [], [])
    acc_s = ([], [], [], [], [])
    for l in range(DEPTH):
        lp = {'norm_pre': norm_pre[l], 'norm_post': norm_post[l], 'w_in': w_in[l],
              'mu_shift': mu_shift[l], 'w0': w0[l], 'w2': w2[l], 'a0': a0[l], 'a2': a2[l],
              'k_k': k_k[l], 'k_a': k_a[l], 'r_k': r_k[l], 'ln_x_w': ln_x_w[l], 'ln_x_b': ln_x_b[l],
              'conv_w': conv_w[l], 'q_a_norm': q_a_norm[l], 'w_uq': w_uq[l], 'kv_a_norm': kv_a_norm[l],
              'w_uk': w_uk[l], 'w_uv': w_uv[l], 'w_branch': w_branch[l], 'w_out': w_out[l]}
        past_ckv = cache_ckv[l][page_table].reshape(bd, past_len, KV_LORA)
        past_kr = cache_krope[l][page_table].reshape(bd, past_len, QK_ROPE)
        xp, st_p = hybrid_layer(xp, pos_p, zero_shift, zero_conv, zero_wkv, None, None, lp)
        xs, st_s = hybrid_layer(xs, pos_s, state_shift[l], state_conv[l], state_wkv[l], past_ckv, past_kr, lp)
        for acc, val in zip(acc_p, st_p):
            acc.append(val)
        for acc, val in zip(acc_s, st_s):
            acc.append(val)
    ckv_p, kr_p, wkv_p, shift_p, conv_p = [jnp.stack(a, axis=0) for a in acc_p]
    ckv_s, kr_s, wkv_s, shift_s, conv_s = [jnp.stack(a, axis=0) for a in acc_s]
    return (xp, xs, ckv_p, kr_p, wkv_p, shift_p, conv_p, ckv_s, kr_s, wkv_s, shift_s, conv_s)
```

```python
import functools
import math

import numpy as np
import jax
import jax.numpy as jnp
from jax import lax
from jax.experimental import pallas as pl
from jax.experimental.pallas import tpu as pltpu

F32 = jnp.float32
BF16 = jnp.bfloat16

D_MODEL = 1024
D_BR = D_MODEL
HEAD_A = 64
H_A = D_BR // HEAD_A
DECAY_LORA = 64
ICLR_LORA = 64
GN_EPS = 64e-5
H_C = 8
QK_NOPE = 64
QK_ROPE = 32
V_HEAD = D_BR // H_C
Q_LORA = D_MODEL // 4
KV_LORA = D_MODEL // 4
ROPE_THETA = 10000.0
NORM_EPS = 1e-6
LANES = 128
SUBLANES = 8
N_LANE_GROUPS = D_BR // LANES
ROPE_LANE0 = 64
VMEM_LIMIT = 56 * 1024 * 1024

COL_R, COL_K, COL_V, COL_GA, COL_CB, COL_CC, COL_CX, COL_GB, COL_GC, COL_MA, COL_MB, COL_MC = (
    i * D_BR for i in range(12))
COL_QA = 12 * D_BR
COL_KVA = COL_QA + Q_LORA
COL_LORA = COL_KVA + KV_LORA
COL_KPE = COL_LORA + LANES
N_PROJ = 13 * D_BR


def _cparams(sem):
    return pltpu.CompilerParams(dimension_semantics=sem, vmem_limit_bytes=VMEM_LIMIT)


def _head_ones():
    r = lax.broadcasted_iota(jnp.int32, (LANES, LANES), 0) // HEAD_A
    c = lax.broadcasted_iota(jnp.int32, (LANES, LANES), 1) // HEAD_A
    return (r == c).astype(BF16)


def _split_dot(x, w):
    hi = x.astype(BF16)
    lo = (x - hi.astype(F32)).astype(BF16)
    return (jnp.dot(hi, w, preferred_element_type=F32) + jnp.dot(lo, w, preferred_element_type=F32))


def _head_sum(x, hm):
    parts = [_split_dot(x[:, g * LANES:(g + 1) * LANES], hm) for g in range(x.shape[1] // LANES)]
    return jnp.concatenate(parts, axis=1)


def _shift_rows(p, halo, k, nb, tq):
    c = p.shape[1]
    prev = pltpu.roll(p, k, axis=0)
    pos = lax.broadcasted_iota(jnp.int32, p.shape, 0) % tq
    out = prev
    for j in range(k):
        hb = jnp.broadcast_to(halo[:, j:j + 1, :], (nb, tq, c)).reshape(nb * tq, c)
        out = jnp.where(pos == j, hb, out)
    return out


def _sigmoid(x):
    return 1.0 / (1.0 + jnp.exp(-x))


def _silu(x):
    return x * _sigmoid(x)


def _rms(x, g):
    return x * lax.rsqrt(jnp.mean(x * x, axis=-1, keepdims=True) + NORM_EPS) * g


def _inproj_kernel(x_ref, g_ref, w_ref, o_ref, h_ref):
    @pl.when(pl.program_id(1) == 0)
    def _():
        h_ref[...] = _rms(x_ref[...], g_ref[...]).astype(BF16)

    o_ref[...] = jnp.dot(h_ref[...], w_ref[...], preferred_element_type=F32)


def _inproj(x, g, w):
    t = x.shape[0]
    tm = min(t, 1024)
    tn = 1024
    return pl.pallas_call(
        _inproj_kernel,
        grid=(t // tm, N_PROJ // tn),
        in_specs=[pl.BlockSpec((tm, D_MODEL), lambda i, j: (i, 0)),
                  pl.BlockSpec((1, D_MODEL), lambda i, j: (0, 0)),
                  pl.BlockSpec((D_MODEL, tn), lambda i, j: (0, j))],
        out_specs=pl.BlockSpec((tm, tn), lambda i, j: (i, j)),
        out_shape=jax.ShapeDtypeStruct((t, N_PROJ), F32),
        scratch_shapes=[pltpu.VMEM((tm, D_MODEL), BF16)],
        compiler_params=_cparams(("parallel", "arbitrary")),
    )(x, g, w)


def _rwkv_prep_kernel(nb, tq, r_ref, k_ref, v_ref, lo_ref, hr_ref, hk_ref, hv_ref, hl_ref,
                      mur_ref, muk_ref, muv_ref, mul_ref, w0_ref, w2_ref, a0_ref, a2_ref,
                      kk_ref, ka_ref, rk_ref,
                      wr_o, a_o, w_o, b_o, k_o, v_o, br_o, kr_o, bonus_o):
    hm = _head_ones()

    def mix(ref, halo_ref, mu_ref):
        p = ref[...]
        prev = _shift_rows(p, halo_ref[...], 1, nb, tq)
        return p + (prev - p) * mu_ref[...]

    r = mix(r_ref, hr_ref, mur_ref)
    k = mix(k_ref, hk_ref, muk_ref)
    v = mix(v_ref, hv_ref, muv_ref)
    lo = mix(lo_ref, hl_ref, mul_ref)
    wf = w0_ref[...] + jnp.dot(jnp.tanh(lo).astype(BF16), w2_ref[...], preferred_element_type=F32)
    decay = jnp.exp(-math.exp(-0.5) * _sigmoid(wf))
    a = _sigmoid(a0_ref[...] + jnp.dot(lo.astype(BF16), a2_ref[...], preferred_element_type=F32))
    kk = k * kk_ref[...]
    kk = kk / jnp.maximum(jnp.sqrt(_head_sum(kk * kk, hm)), 1e-12)
    kmod = k * (1.0 + (a - 1.0) * ka_ref[...])
    b = kk * a
    wr_o[...] = decay * r
    a_o[...] = -kk
    w_o[...] = decay
    b_o[...] = b
    k_o[...] = kmod
    v_o[...] = v
    br_o[...] = _head_sum(b * r, hm)
    kr_o[...] = _head_sum(kmod * r, hm)
    bonus_o[...] = _head_sum(r * kmod * rk_ref[...], hm) * v


def _rwkv_prep(proj, halos, lw, nb, tq):
    t = proj.shape[0]
    tm = nb * tq
    row = lambda cb: pl.BlockSpec((tm, D_BR), lambda i, cb=cb: (i, cb))
    halo = lambda c: pl.BlockSpec((nb, 1, c), lambda i: (i, 0, 0))
    vec = lambda c: pl.BlockSpec((1, c), lambda i: (0, 0))
    mat = pl.BlockSpec((LANES, D_BR), lambda i: (0, 0))
    outs = [jax.ShapeDtypeStruct((t, D_BR), F32)] * 9
    return pl.pallas_call(
        functools.partial(_rwkv_prep_kernel, nb, tq),
        grid=(t // tm,),
        in_specs=[row(0), row(1), row(2),
                  pl.BlockSpec((tm, LANES), lambda i: (i, COL_LORA // LANES)),
                  halo(D_BR), halo(D_BR), halo(D_BR), halo(LANES),
                  vec(D_BR), vec(D_BR), vec(D_BR), vec(LANES),
                  vec(D_BR), mat, vec(D_BR), mat, vec(D_BR), vec(D_BR), vec(D_BR)],
        out_specs=[pl.BlockSpec((tm, D_BR), lambda i: (i, 0))] * 9,
        out_shape=outs,
        compiler_params=_cparams(("parallel",)),
    )(proj, proj, proj, proj, halos["r"], halos["k"], halos["v"], halos["lora"],
      lw["mu_r"], lw["mu_k"], lw["mu_v"], lw["mu_lora"], lw["w0"], lw["w2p"], lw["a0"], lw["a2p"],
      lw["k_k"], lw["k_a"], lw["r_k"])


def _wkv_kernel(ns, tb, wr_ref, a_ref, w_ref, b_ref, k_ref, v_ref, br_ref, kr_ref, s0_ref,
                y_ref, st_ref):
    @pl.when(pl.program_id(1) == 0)
    def _():
        st_ref[...] = s0_ref[...]

    hm = _head_ones()
    ri = lax.broadcasted_iota(jnp.int32, (HEAD_A, LANES), 0)
    ci = lax.broadcasted_iota(jnp.int32, (HEAD_A, LANES), 1)
    diag = ((ci % HEAD_A) == ri).astype(F32)

    sub = lax.broadcasted_iota(jnp.int32, (SUBLANES, LANES), 0)

    def step(u, ybuf, base):
        t = base + u
        ybuf = list(ybuf)
        for s in range(ns):
            row = lambda ref: ref[s, pl.ds(t, 1), :]
            wr, a, w, b, k, v, br, kr = (row(x) for x in (wr_ref, a_ref, w_ref, b_ref, k_ref, v_ref,
                                                          br_ref, kr_ref))
            for g in range(N_LANE_GROUPS):
                sl = slice(g * LANES, (g + 1) * LANES)
                st = st_ref[s, :, sl]
                m1 = st * a[:, sl]
                m1_hi = m1.astype(BF16)
                m1_lo = (m1 - m1_hi.astype(F32)).astype(BF16)
                m2 = (st * wr[:, sl]).astype(BF16)
                m3 = (diag * v[:, sl]).astype(BF16)
                res = jnp.dot(jnp.concatenate([m1_hi, m1_lo, m2, m3], axis=0), hm,
                              preferred_element_type=F32)
                sa = res[0:HEAD_A] + res[HEAD_A:2 * HEAD_A]
                y2 = res[2 * HEAD_A:3 * HEAD_A]
                vc = res[3 * HEAD_A:4 * HEAD_A]
                st_ref[s, :, sl] = st * w[:, sl] + sa * b[:, sl] + vc * k[:, sl]
                ybc = y2 + sa * br[:, sl] + vc * kr[:, sl]
                yrow = jnp.sum(ybc * diag, axis=0, keepdims=True)
                i = s * N_LANE_GROUPS + g
                ybuf[i] = jnp.where(sub == u, yrow, ybuf[i])
        return tuple(ybuf)

    def block8(t8, carry):
        base = pl.multiple_of(t8 * SUBLANES, SUBLANES)
        zero = tuple(jnp.zeros((SUBLANES, LANES), F32) for _ in range(ns * N_LANE_GROUPS))
        ybuf = lax.fori_loop(0, SUBLANES, functools.partial(step, base=base), zero)
        for s in range(ns):
            y_ref[s, pl.ds(base, SUBLANES), :] = jnp.concatenate(
                ybuf[s * N_LANE_GROUPS:(s + 1) * N_LANE_GROUPS], axis=1)
        return carry

    lax.fori_loop(0, tb // SUBLANES, block8, 0)


def _wkv_scan(ops, s0, nseq, tseq):
    ns = 2
    tb = min(tseq, 128)
    ops3 = [o.reshape(nseq, tseq, D_BR) for o in ops]
    blk = pl.BlockSpec((ns, tb, D_BR), lambda i, j: (i, j, 0))
    sblk = pl.BlockSpec((ns, HEAD_A, D_BR), lambda i, j: (i, 0, 0))
    y, st = pl.pallas_call(
        functools.partial(_wkv_kernel, ns, tb),
        grid=(nseq // ns, tseq // tb),
        in_specs=[blk] * 8 + [sblk],
        out_specs=[blk, sblk],
        out_shape=[jax.ShapeDtypeStruct((nseq, tseq, D_BR), F32),
                   jax.ShapeDtypeStruct((nseq, HEAD_A, D_BR), F32)],
        compiler_params=_cparams(("parallel", "arbitrary")),
    )(*ops3, s0)
    return y.reshape(nseq * tseq, D_BR), st


def _rope(x, c_ref, s1_ref, s2_ref):
    parts = []
    for h in range(x.shape[1] // LANES):
        xh = x[:, h * LANES:(h + 1) * LANES]
        half = QK_ROPE // 2
        parts.append(xh * c_ref[...] + pltpu.roll(xh, half, axis=1) * s1_ref[...]
                     + pltpu.roll(xh, LANES - half, axis=1) * s2_ref[...])
    return parts[0] if len(parts) == 1 else jnp.concatenate(parts, axis=1)


def _mla_common(qa_ref, kva_ref, kpe_ref, qn_ref, kvn_ref, wq_ref, cq_ref, ck_ref, s1_ref, s2_ref):
    scale = (QK_NOPE + QK_ROPE) ** -0.5
    qn = _rms(qa_ref[...], qn_ref[...]).astype(BF16)
    q = jnp.dot(qn, wq_ref[...], preferred_element_type=F32)
    q = _rope(q, cq_ref, s1_ref, s2_ref) * scale
    ckv = _rms(kva_ref[...], kvn_ref[...])
    kr = _rope(kpe_ref[...], ck_ref, s1_ref, s2_ref)
    return q, ckv, kr


def _mla_prep_prompt_kernel(qa_ref, kva_ref, kpe_ref, qn_ref, kvn_ref, wq_ref, wk_ref, wv_ref,
                            cq_ref, ck_ref, s1_ref, s2_ref, q_o, k_o, v_o, ckv_o, kr_o):
    q, ckv, kr = _mla_common(qa_ref, kva_ref, kpe_ref, qn_ref, kvn_ref, wq_ref, cq_ref, ck_ref,
                             s1_ref, s2_ref)
    q_o[...] = q.astype(BF16)
    ckv_o[...] = ckv
    kr_o[...] = kr
    cb = ckv.astype(BF16)
    kn = jnp.dot(cb, wk_ref[...], preferred_element_type=F32)
    k_o[...] = (kn + jnp.concatenate([kr] * H_C, axis=1)).astype(BF16)
    v_o[...] = jnp.dot(cb, wv_ref[...], preferred_element_type=F32).astype(BF16)


def _small_specs(tm, npos_blocks):
    return [pl.BlockSpec((tm, Q_LORA), lambda i: (i, COL_QA // Q_LORA)),
            pl.BlockSpec((tm, KV_LORA), lambda i: (i, COL_KVA // KV_LORA)),
            pl.BlockSpec((tm, LANES), lambda i: (i, COL_KPE // LANES)),
            pl.BlockSpec((1, Q_LORA), lambda i: (0, 0)),
            pl.BlockSpec((1, KV_LORA), lambda i: (0, 0)),
            pl.BlockSpec((Q_LORA, H_C * LANES), lambda i: (0, 0))], \
           [pl.BlockSpec((tm, LANES), lambda i: (i % npos_blocks, 0))] * 4


def _mla_prep_prompt(proj, lw, tabs, seq):
    t = proj.shape[0]
    tm = min(seq, 512)
    head, tab = _small_specs(tm, seq // tm)
    wfull = pl.BlockSpec((KV_LORA, H_C * LANES), lambda i: (0, 0))
    row = lambda c: pl.BlockSpec((tm, c), lambda i: (i, 0))
    return pl.pallas_call(
        _mla_prep_prompt_kernel,
        grid=(t // tm,),
        in_specs=head + [wfull, wfull] + tab,
        out_specs=[row(D_BR), row(D_BR), row(D_BR), row(KV_LORA), row(LANES)],
        out_shape=[jax.ShapeDtypeStruct((t, D_BR), BF16)] * 3
                  + [jax.ShapeDtypeStruct((t, KV_LORA), F32), jax.ShapeDtypeStruct((t, LANES), F32)],
        compiler_params=_cparams(("parallel",)),
    )(proj, proj, proj, lw["q_a_norm"], lw["kv_a_norm"], lw["wq"], lw["wk"], lw["wv"], *tabs)


def _mla_prep_sample_kernel(qa_ref, kva_ref, kpe_ref, qn_ref, kvn_ref, wq_ref, wuk_ref, sel_ref,
                            cq_ref, ck_ref, s1_ref, s2_ref, ql_o, qp_o, ckv_o, kr_o):
    q, ckv, kr = _mla_common(qa_ref, kva_ref, kpe_ref, qn_ref, kvn_ref, wq_ref, cq_ref, ck_ref,
                             s1_ref, s2_ref)
    ckv_o[...] = ckv
    kr_o[...] = kr
    qb = q.astype(BF16)
    for h in range(H_C):
        qh = qb[:, h * LANES:(h + 1) * LANES]
        ql_o[h] = jnp.dot(qh, wuk_ref[h], preferred_element_type=F32)
        qp_o[h] = jnp.dot(qh, sel_ref[...], preferred_element_type=F32)


def _mla_prep_sample(proj, lw, tabs):
    t = proj.shape[0]
    tm = min(t, 256)
    head, tab = _small_specs(tm, 1)
    row = lambda c: pl.BlockSpec((tm, c), lambda i: (i, 0))
    return pl.pallas_call(
        _mla_prep_sample_kernel,
        grid=(t // tm,),
        in_specs=head + [pl.BlockSpec((H_C, LANES, KV_LORA), lambda i: (0, 0, 0)),
                         pl.BlockSpec((LANES, QK_ROPE), lambda i: (0, 0))] + tab,
        out_specs=[pl.BlockSpec((H_C, tm, KV_LORA), lambda i: (0, i, 0)),
                   pl.BlockSpec((H_C, tm, QK_ROPE), lambda i: (0, i, 0)),
                   row(KV_LORA), row(LANES)],
        out_shape=[jax.ShapeDtypeStruct((H_C, t, KV_LORA), F32),
                   jax.ShapeDtypeStruct((H_C, t, QK_ROPE), F32),
                   jax.ShapeDtypeStruct((t, KV_LORA), F32), jax.ShapeDtypeStruct((t, LANES), F32)],
        compiler_params=_cparams(("parallel",)),
    )(proj, proj, proj, lw["q_a_norm"], lw["kv_a_norm"], lw["wq"], lw["wuk_abs"], lw["sel_rope"], *tabs)


NEG = -1e30


def _online_softmax_update(s, v_dot, m_ref, l_ref, acc_ref):
    m_old = m_ref[...]
    m_new = jnp.maximum(m_old, jnp.max(s, axis=-1, keepdims=True))
    alpha = jnp.exp(m_old - m_new)
    p = jnp.exp(s - m_new)
    l_ref[...] = alpha * l_ref[...] + jnp.sum(p, axis=-1, keepdims=True)
    acc_ref[...] = alpha * acc_ref[...] + v_dot(p.astype(BF16))
    m_ref[...] = m_new


def _flash_kernel(tq, tk, q_ref, k_ref, v_ref, o_ref, m_ref, l_ref, acc_ref):
    qi = pl.program_id(2)
    ki = pl.program_id(3)

    @pl.when(ki == 0)
    def _():
        m_ref[...] = jnp.full(m_ref.shape, NEG, F32)
        l_ref[...] = jnp.zeros(l_ref.shape, F32)
        acc_ref[...] = jnp.zeros(acc_ref.shape, F32)

    @pl.when(ki <= qi)
    def _():
        s = lax.dot_general(q_ref[...], k_ref[...], (((1,), (1,)), ((), ())),
                            preferred_element_type=F32)
        qpos = qi * tq + lax.broadcasted_iota(jnp.int32, s.shape, 0)
        kpos = ki * tk + lax.broadcasted_iota(jnp.int32, s.shape, 1)
        s = jnp.where(kpos <= qpos, s, NEG)
        _online_softmax_update(
            s, lambda p: jnp.dot(p, v_ref[...], preferred_element_type=F32), m_ref, l_ref, acc_ref)

    @pl.when(ki == qi)
    def _():
        o_ref[...] = acc_ref[...] / l_ref[...]


def _flash_prompt(q, k, v, nbatch, seq):
    tq = tk = min(seq, 1024)
    nq = seq // tq
    return pl.pallas_call(
        functools.partial(_flash_kernel, tq, tk),
        grid=(nbatch, H_C, nq, nq),
        in_specs=[pl.BlockSpec((tq, LANES), lambda b, h, qi, ki: (b * nq + qi, h)),
                  pl.BlockSpec((tk, LANES), lambda b, h, qi, ki: (b * nq + jnp.minimum(ki, qi), h)),
                  pl.BlockSpec((tk, LANES), lambda b, h, qi, ki: (b * nq + jnp.minimum(ki, qi), h))],
        out_specs=pl.BlockSpec((tq, LANES), lambda b, h, qi, ki: (b * nq + qi, h)),
        out_shape=jax.ShapeDtypeStruct((nbatch * seq, D_BR), F32),
        scratch_shapes=[pltpu.VMEM((tq, 1), F32), pltpu.VMEM((tq, 1), F32), pltpu.VMEM((tq, LANES), F32)],
        compiler_params=_cparams(("parallel", "parallel", "parallel", "arbitrary")),
    )(q, k, v)


PAGES_PER_STEP = 8


def _paged_kernel(snew, pt_ref, ql_ref, qp_ref, *rest):
    npg = PAGES_PER_STEP
    ckv_refs = rest[:npg]
    kr_refs = rest[npg:2 * npg]
    cn_ref, kn_ref, wv_ref, o_ref, m_ref, l_ref, acc_ref = rest[2 * npg:]
    pg = pl.program_id(1)
    rows = H_C * snew
    ql = ql_ref[...].reshape(rows, KV_LORA).astype(BF16)
    qp = qp_ref[...].reshape(rows, QK_ROPE).astype(BF16)
    page = ckv_refs[0].shape[1]

    @pl.when(pg == 0)
    def _():
        m_ref[...] = jnp.full(m_ref.shape, NEG, F32)
        l_ref[...] = jnp.zeros(l_ref.shape, F32)
        acc_ref[...] = jnp.zeros(acc_ref.shape, F32)

    def scores(c, r):
        return (lax.dot_general(ql, c, (((1,), (1,)), ((), ())), preferred_element_type=F32)
                + lax.dot_general(qp, r, (((1,), (1,)), ((), ())), preferred_element_type=F32))

    for j in range(npg):
        c = ckv_refs[j][0].astype(BF16)
        r = kr_refs[j][0].astype(BF16)
        _online_softmax_update(
            scores(c, r), lambda p, c=c: jnp.dot(p, c, preferred_element_type=F32), m_ref, l_ref, acc_ref)

    @pl.when(pg == pl.num_programs(1) - 1)
    def _():
        pad = lambda x: jnp.concatenate(
            [x, jnp.zeros((page - snew, x.shape[1]), F32)], axis=0).astype(BF16)
        c = pad(cn_ref[...])
        s = scores(c, pad(kn_ref[...]))
        tok = lax.broadcasted_iota(jnp.int32, s.shape, 0) % snew
        key = lax.broadcasted_iota(jnp.int32, s.shape, 1)
        s = jnp.where(key <= tok, s, NEG)
        _online_softmax_update(
            s, lambda p: jnp.dot(p, c, preferred_element_type=F32), m_ref, l_ref, acc_ref)
        o_lat = acc_ref[...] / l_ref[...]
        for h in range(H_C):
            o_ref[:, h * V_HEAD:(h + 1) * V_HEAD] = jnp.dot(
                o_lat[h * snew:(h + 1) * snew].astype(BF16), wv_ref[h], preferred_element_type=F32)


def _paged_attention(ql, qp, cache_ckv, cache_kr, ckv_new, kr_new, wv, page_table, snew):
    nb, n_pages = page_table.shape
    page = cache_ckv.shape[1]
    npg = PAGES_PER_STEP
    ckv_specs = [pl.BlockSpec((1, page, KV_LORA), lambda b, g, pt, j=j: (pt[b, g * npg + j], 0, 0))
                 for j in range(npg)]
    kr_specs = [pl.BlockSpec((1, page, QK_ROPE), lambda b, g, pt, j=j: (pt[b, g * npg + j], 0, 0))
                for j in range(npg)]
    rows = H_C * snew
    grid_spec = pltpu.PrefetchScalarGridSpec(
        num_scalar_prefetch=1,
        grid=(nb, n_pages // npg),
        in_specs=[pl.BlockSpec((H_C, snew, KV_LORA), lambda b, g, pt: (0, b, 0)),
                  pl.BlockSpec((H_C, snew, QK_ROPE), lambda b, g, pt: (0, b, 0))]
                 + ckv_specs + kr_specs
                 + [pl.BlockSpec((snew, KV_LORA), lambda b, g, pt: (b, 0)),
                    pl.BlockSpec((snew, QK_ROPE), lambda b, g, pt: (b, 0)),
                    pl.BlockSpec((H_C, KV_LORA, V_HEAD), lambda b, g, pt: (0, 0, 0))],
        out_specs=pl.BlockSpec((snew, D_BR), lambda b, g, pt: (b, 0)),
        scratch_shapes=[pltpu.VMEM((rows, 1), F32), pltpu.VMEM((rows, 1), F32),
                        pltpu.VMEM((rows, KV_LORA), F32)],
    )
    return pl.pallas_call(
        functools.partial(_paged_kernel, snew),
        grid_spec=grid_spec,
        out_shape=jax.ShapeDtypeStruct((nb * snew, D_BR), F32),
        compiler_params=_cparams(("parallel", "arbitrary")),
    )(page_table, ql, qp, *([cache_ckv] * npg), *([cache_kr] * npg), ckv_new, kr_new, wv)


def _merge_kernel(nb, tq, y_ref, bonus_ref, x_ref, o_ref, ga_ref, cb_ref, cc_ref, cx_ref, gb_ref,
                  gc_ref, ma_ref, mb_ref, mc_ref, hc_ref, hx_ref, lnw_ref, lnb_ref, cw_ref, np_ref,
                  wbr_ref, wout_ref, xo_ref, zt_ref):
    hm = _head_ones()
    tm = nb * tq
    y = y_ref[...]
    mu = _head_sum(y, hm) * (1.0 / HEAD_A)
    yc = y - mu
    var = _head_sum(yc * yc, hm) * (1.0 / HEAD_A)
    out_a = (yc * lax.rsqrt(var + GN_EPS) * lnw_ref[...] + lnb_ref[...] + bonus_ref[...]) * _silu(ga_ref[...])
    z = cc_ref[...] * cx_ref[...]
    zh = hc_ref[...] * hx_ref[...]
    z1 = _shift_rows(z, zh[:, 1:2, :], 1, nb, tq)
    z2 = _shift_rows(z, zh, 2, nb, tq)
    cw = cw_ref[...]
    conv = cw[0:1] * z2 + cw[1:2] * z1 + cw[2:3] * z
    out_b = cb_ref[...] * conv * _silu(gb_ref[...])
    if nb == 1:
        zt_ref[0] = z[tm - SUBLANES:, :]
    else:
        zt_ref[...] = z.reshape(nb, tq, D_BR)
    out_c = o_ref[...] * _silu(gc_ref[...])
    mixed = (_sigmoid(ma_ref[...]) * jnp.dot(out_a.astype(BF16), wbr_ref[0], preferred_element_type=F32)
             + _sigmoid(mb_ref[...]) * jnp.dot(out_b.astype(BF16), wbr_ref[1], preferred_element_type=F32)
             + _sigmoid(mc_ref[...]) * jnp.dot(out_c.astype(BF16), wbr_ref[2], preferred_element_type=F32))
    res = jnp.dot(mixed.astype(BF16), wout_ref[...], preferred_element_type=F32)
    xo_ref[...] = x_ref[...] + _rms(res, np_ref[...])


def _merge(y, bonus, x, o, proj, halo_c, halo_x, lw, nb, tq):
    t = x.shape[0]
    tm = nb * tq
    nseg = t // tq
    row = pl.BlockSpec((tm, D_BR), lambda i: (i, 0))
    prow = lambda cb: pl.BlockSpec((tm, D_BR), lambda i, cb=cb: (i, cb))
    halo = pl.BlockSpec((nb, 2, D_BR), lambda i: (i, 0, 0))
    vec = pl.BlockSpec((1, D_BR), lambda i: (0, 0))
    zt_rows = min(tq, SUBLANES)
    return pl.pallas_call(
        functools.partial(_merge_kernel, nb, tq),
        grid=(t // tm,),
        in_specs=[row, row, row, row] + [prow(c) for c in range(3, 12)] + [halo, halo, vec, vec,
                  pl.BlockSpec((3, D_BR), lambda i: (0, 0)), vec,
                  pl.BlockSpec((3, D_BR, D_MODEL), lambda i: (0, 0, 0)),
                  pl.BlockSpec((D_MODEL, D_MODEL), lambda i: (0, 0))],
        out_specs=[row, pl.BlockSpec((nb, zt_rows, D_BR), lambda i: (i, 0, 0))],
        out_shape=[jax.ShapeDtypeStruct((t, D_MODEL), F32),
                   jax.ShapeDtypeStruct((nseg, zt_rows, D_BR), F32)],
        compiler_params=_cparams(("parallel",)),
    )(y, bonus, x, o, *([proj] * 9), halo_c, halo_x, lw["ln_x_w"], lw["ln_x_b"], lw["conv_w"],
      lw["norm_post"], lw["w_branch"], lw["w_out"])


def _layer_weights(l, p):
    w_in = p["w_in"][l]
    z = lambda n: jnp.zeros((D_MODEL, n), F32)
    o_ga = 3 * D_BR + DECAY_LORA + ICLR_LORA
    o_qa = o_ga + 5 * D_BR
    o_kpe = o_qa + Q_LORA + KV_LORA
    o_gc = o_kpe + QK_ROPE
    w_re = jnp.concatenate([
        w_in[:, :3 * D_BR], w_in[:, o_ga:o_qa], w_in[:, o_gc:o_gc + 4 * D_BR],
        w_in[:, o_qa:o_kpe], w_in[:, 3 * D_BR:o_ga],
        z(ROPE_LANE0), w_in[:, o_kpe:o_gc], z(LANES - ROPE_LANE0 - QK_ROPE),
        z(N_PROJ - COL_KPE - LANES)], axis=1).astype(BF16)
    mu = p["mu_shift"][l]
    r2 = lambda a: a.reshape(1, -1)
    w_uq = p["w_uq"][l]
    wq = jnp.concatenate([w_uq, jnp.zeros((Q_LORA, H_C, LANES - QK_NOPE - QK_ROPE), F32)], axis=2)
    w_uk = p["w_uk"][l]
    wk = jnp.concatenate([w_uk, jnp.zeros((KV_LORA, H_C, LANES - QK_NOPE), F32)], axis=2)
    wuk_abs = jnp.concatenate([jnp.transpose(w_uk, (1, 2, 0)),
                               jnp.zeros((H_C, LANES - QK_NOPE, KV_LORA), F32)], axis=1)
    sel = np.zeros((LANES, QK_ROPE), np.float32)
    sel[ROPE_LANE0 + np.arange(QK_ROPE), np.arange(QK_ROPE)] = 1.0
    zl = jnp.zeros((DECAY_LORA, D_BR), F32)
    return {
        "w_in": w_re, "norm_pre": r2(p["norm_pre"][l]), "norm_post": r2(p["norm_post"][l]),
        "mu_r": r2(mu[:D_BR]), "mu_k": r2(mu[D_BR:2 * D_BR]), "mu_v": r2(mu[2 * D_BR:3 * D_BR]),
        "mu_lora": r2(mu[3 * D_BR:]),
        "w0": r2(p["w0"][l]), "a0": r2(p["a0"][l]),
        "w2p": jnp.concatenate([p["w2"][l], zl], axis=0).astype(BF16),
        "a2p": jnp.concatenate([zl, p["a2"][l]], axis=0).astype(BF16),
        "k_k": r2(p["k_k"][l]), "k_a": r2(p["k_a"][l]), "r_k": r2(p["r_k"][l]),
        "ln_x_w": r2(p["ln_x_w"][l]), "ln_x_b": r2(p["ln_x_b"][l]), "conv_w": p["conv_w"][l],
        "q_a_norm": r2(p["q_a_norm"][l]), "kv_a_norm": r2(p["kv_a_norm"][l]),
        "wq": wq.reshape(Q_LORA, H_C * LANES).astype(BF16),
        "wk": wk.reshape(KV_LORA, H_C * LANES).astype(BF16),
        "wv": p["w_uv"][l].reshape(KV_LORA, H_C * V_HEAD).astype(BF16),
        "wuk_abs": wuk_abs.astype(BF16),
        "wv_heads": jnp.transpose(p["w_uv"][l], (1, 0, 2)).astype(BF16),
        "sel_rope": jnp.asarray(sel, BF16),
        "w_branch": p["w_branch"][l].astype(BF16), "w_out": p["w_out"][l].astype(BF16),
    }


def _rope_tables(pos):
    half = QK_ROPE // 2
    freqs = ROPE_THETA ** (-jnp.arange(half, dtype=F32) / half)
    ang = pos.astype(F32)[:, None] * freqs[None, :]
    cos, sin = jnp.cos(ang), jnp.sin(ang)
    n = pos.shape[0]
    lo = jnp.zeros((n, ROPE_LANE0), F32)
    hi = jnp.zeros((n, LANES - ROPE_LANE0 - QK_ROPE), F32)
    zh = jnp.zeros((n, half), F32)
    ck = jnp.concatenate([lo, cos, cos, hi], axis=1)
    cq = jnp.concatenate([lo + 1.0, cos, cos, hi], axis=1)
    s1 = jnp.concatenate([lo, zh, sin, hi], axis=1)
    s2 = jnp.concatenate([lo, -sin, zh, hi], axis=1)
    return cq, ck, s1, s2


def _prev_row_halos(proj, first_rows, nseq, tseq, tq):
    nseg_per = tseq // tq
    out = {}
    for name, (c0, c1) in {"r": (COL_R, COL_R + D_BR), "k": (COL_K, COL_K + D_BR),
                           "v": (COL_V, COL_V + D_BR), "lora": (COL_LORA, COL_LORA + LANES)}.items():
        first = first_rows[name].reshape(nseq, 1, c1 - c0)
        if nseg_per > 1:
            p3 = proj.reshape(nseq, tseq, N_PROJ)
            inner = p3[:, tq - 1:tseq - 1:tq, c0:c1]
            first = jnp.concatenate([first, inner], axis=1)
        out[name] = first.reshape(nseq * nseg_per, 1, c1 - c0)
    return out


def _conv_halos(proj, conv_prev, nseq, tseq, tq):
    nseg_per = tseq // tq
    hc = conv_prev
    hx = jnp.ones_like(conv_prev)
    if nseg_per > 1:
        p3 = proj.reshape(nseq, nseg_per, tq, N_PROJ)[:, :-1, tq - 2:, :]
        hc = jnp.concatenate([hc[:, None], p3[..., COL_CC:COL_CC + D_BR]], axis=1)
        hx = jnp.concatenate([hx[:, None], p3[..., COL_CX:COL_CX + D_BR]], axis=1)
    return hc.reshape(nseq * nseg_per, 2, D_BR), hx.reshape(nseq * nseg_per, 2, D_BR)


def _layer(x, lw, tabs, nseq, tseq, shift_prev, conv_prev, wkv_prev, attn):
    t = nseq * tseq
    proj = _inproj(x, lw["norm_pre"], lw["w_in"])
    if tseq >= 128:
        nb, tq = 1, 128
    else:
        nb, tq = min(nseq, 128 // tseq), tseq
    first = {"r": shift_prev[:, :D_BR], "k": shift_prev[:, D_BR:2 * D_BR],
             "v": shift_prev[:, 2 * D_BR:3 * D_BR], "lora": shift_prev[:, 3 * D_BR:]}
    ops = _rwkv_prep(proj, _prev_row_halos(proj, first, nseq, tseq, tq), lw, nb, tq)
    bonus = ops[8]
    s0 = jnp.transpose(wkv_prev, (0, 2, 1, 3)).reshape(nseq, HEAD_A, D_BR)
    y, st = _wkv_scan(ops[:8], s0, nseq, tseq)
    wkv_new = jnp.transpose(st.reshape(nseq, HEAD_A, H_A, HEAD_A), (0, 2, 1, 3))
    o, ckv, kr128 = attn(proj, lw, tabs)
    hc, hx = _conv_halos(proj, conv_prev, nseq, tseq, tq)
    x_new, zt = _merge(y, bonus, x, o, proj, hc, hx, lw, nb, tq)
    p3 = proj.reshape(nseq, tseq, N_PROJ)[:, -1, :]
    new_shift = jnp.concatenate([p3[:, :3 * D_BR], p3[:, COL_LORA:COL_LORA + LANES]], axis=1)
    new_conv = zt.reshape(nseq, tseq // tq, -1, D_BR)[:, -1, -2:, :]
    kr = kr128[:, ROPE_LANE0:ROPE_LANE0 + QK_ROPE]
    return x_new, (ckv.reshape(nseq, tseq, KV_LORA), kr.reshape(nseq, tseq, QK_ROPE), wkv_new,
                   new_shift, new_conv)


def kernel(x_prompt, x_sample, cache_ckv, cache_krope, state_wkv, state_shift, state_conv, page_table,
           norm_pre, norm_post, w_in, mu_shift, w0, w2, a0, a2, k_k, k_a, r_k, ln_x_w, ln_x_b,
           conv_w, q_a_norm, w_uq, kv_a_norm, w_uk, w_uv, w_branch, w_out):
    params = dict(norm_pre=norm_pre, norm_post=norm_post, w_in=w_in, mu_shift=mu_shift, w0=w0, w2=w2,
                  a0=a0, a2=a2, k_k=k_k, k_a=k_a, r_k=r_k.reshape(r_k.shape[0], -1), ln_x_w=ln_x_w,
                  ln_x_b=ln_x_b, conv_w=conv_w, q_a_norm=q_a_norm, w_uq=w_uq, kv_a_norm=kv_a_norm,
                  w_uk=w_uk, w_uv=w_uv, w_branch=w_branch, w_out=w_out)
    bp, sp, _ = x_prompt.shape
    bd, sd, _ = x_sample.shape
    depth = w_in.shape[0]
    past_len = page_table.shape[1] * cache_ckv.shape[2]
    tabs_p = _rope_tables(jnp.arange(sp))
    tm_s = min(bd * sd, 256)
    tabs_s = _rope_tables(past_len + (jnp.arange(tm_s) % sd))

    xp = x_prompt.reshape(bp * sp, D_MODEL)
    xs = x_sample.reshape(bd * sd, D_MODEL)
    acc_p, acc_s = [], []
    for l in range(depth):
        lw = _layer_weights(l, params)

        def attn_prompt(proj, lw, tabs):
            q, k, v, ckv, kr128 = _mla_prep_prompt(proj, lw, tabs, sp)
            return _flash_prompt(q, k, v, bp, sp), ckv, kr128

        def attn_sample(proj, lw, tabs, l=l):
            ql, qp, ckv, kr128 = _mla_prep_sample(proj, lw, tabs)
            kr_new = kr128[:, ROPE_LANE0:ROPE_LANE0 + QK_ROPE]
            o = _paged_attention(ql, qp, cache_ckv[l], cache_krope[l], ckv, kr_new, lw["wv_heads"],
                                 page_table, sd)
            return o, ckv, kr128

        xp, st_p = _layer(xp, lw, tabs_p, bp, sp, jnp.zeros((bp, state_shift.shape[2]), F32),
                          jnp.zeros((bp, 2, D_BR), F32), jnp.zeros((bp, H_A, HEAD_A, HEAD_A), F32),
                          attn_prompt)
        xs, st_s = _layer(xs, lw, tabs_s, bd, sd, state_shift[l], state_conv[l], state_wkv[l],
                          attn_sample)
        acc_p.append(st_p)
        acc_s.append(st_s)
    outs_p = [jnp.stack([a[i] for a in acc_p], axis=0) for i in range(5)]
    outs_s = [jnp.stack([a[i] for a in acc_s], axis=0) for i in range(5)]
    return (xp.reshape(bp, sp, D_MODEL), xs.reshape(bd, sd, D_MODEL), *outs_p, *outs_s)
```

```python
import functools
import math

import numpy as np
import jax
import jax.numpy as jnp
from jax import lax
from jax.experimental import pallas as pl
from jax.experimental.pallas import tpu as pltpu

F32 = jnp.float32
BF16 = jnp.bfloat16

D_MODEL = 1024
D_BR = D_MODEL
HEAD_A = 64
H_A = D_BR // HEAD_A
DECAY_LORA = 64
ICLR_LORA = 64
GN_EPS = 64e-5
H_C = 8
QK_NOPE = 64
QK_ROPE = 32
V_HEAD = D_BR // H_C
Q_LORA = D_MODEL // 4
KV_LORA = D_MODEL // 4
ROPE_THETA = 10000.0
NORM_EPS = 1e-6
LANES = 128
SUBLANES = 8
N_LANE_GROUPS = D_BR // LANES
ROPE_LANE0 = 64
VMEM_LIMIT = 56 * 1024 * 1024

COL_R, COL_K, COL_V, COL_GA, COL_CB, COL_CC, COL_CX, COL_GB, COL_GC, COL_MA, COL_MB, COL_MC = (
    i * D_BR for i in range(12))
COL_QA = 12 * D_BR
COL_KVA = COL_QA + Q_LORA
COL_LORA = COL_KVA + KV_LORA
COL_KPE = COL_LORA + LANES
N_PROJ = 13 * D_BR


def _cparams(sem):
    return pltpu.CompilerParams(dimension_semantics=sem, vmem_limit_bytes=VMEM_LIMIT)


def _head_ones():
    r = lax.broadcasted_iota(jnp.int32, (LANES, LANES), 0) // HEAD_A
    c = lax.broadcasted_iota(jnp.int32, (LANES, LANES), 1) // HEAD_A
    return (r == c).astype(BF16)


def _split_dot(x, w):
    hi = x.astype(BF16)
    lo = (x - hi.astype(F32)).astype(BF16)
    return (jnp.dot(hi, w, preferred_element_type=F32) + jnp.dot(lo, w, preferred_element_type=F32))


def _split_dot_rhs(w, x):
    hi = x.astype(BF16)
    lo = (x - hi.astype(F32)).astype(BF16)
    return (jnp.dot(w, hi, preferred_element_type=F32) + jnp.dot(w, lo, preferred_element_type=F32))


def _head_sum(x, hm):
    parts = [_split_dot(x[:, g * LANES:(g + 1) * LANES], hm) for g in range(x.shape[1] // LANES)]
    return jnp.concatenate(parts, axis=1)


def _shift_rows(p, halo, k, nb, tq):
    c = p.shape[1]
    prev = pltpu.roll(p, k, axis=0)
    pos = lax.broadcasted_iota(jnp.int32, p.shape, 0) % tq
    out = prev
    for j in range(k):
        hb = jnp.broadcast_to(halo[:, j:j + 1, :], (nb, tq, c)).reshape(nb * tq, c)
        out = jnp.where(pos == j, hb, out)
    return out


def _sigmoid(x):
    return 1.0 / (1.0 + jnp.exp(-x))


def _silu(x):
    return x * _sigmoid(x)


def _rms(x, g):
    return x * lax.rsqrt(jnp.mean(x * x, axis=-1, keepdims=True) + NORM_EPS) * g


def _inproj_kernel(x_ref, g_ref, w_ref, o_ref, h_ref):
    @pl.when(pl.program_id(1) == 0)
    def _():
        h_ref[...] = _rms(x_ref[...], g_ref[...]).astype(BF16)

    o_ref[...] = jnp.dot(h_ref[...], w_ref[...], preferred_element_type=F32)


def _inproj(x, g, w):
    t = x.shape[0]
    tm = min(t, 1024)
    tn = 1024
    return pl.pallas_call(
        _inproj_kernel,
        grid=(t // tm, N_PROJ // tn),
        in_specs=[pl.BlockSpec((tm, D_MODEL), lambda i, j: (i, 0)),
                  pl.BlockSpec((1, D_MODEL), lambda i, j: (0, 0)),
                  pl.BlockSpec((D_MODEL, tn), lambda i, j: (0, j))],
        out_specs=pl.BlockSpec((tm, tn), lambda i, j: (i, j)),
        out_shape=jax.ShapeDtypeStruct((t, N_PROJ), F32),
        scratch_shapes=[pltpu.VMEM((tm, D_MODEL), BF16)],
        compiler_params=_cparams(("parallel", "arbitrary")),
    )(x, g, w)


def _rwkv_prep_kernel(nb, tq, r_ref, k_ref, v_ref, lo_ref, hr_ref, hk_ref, hv_ref, hl_ref,
                      mur_ref, muk_ref, muv_ref, mul_ref, w0_ref, w2_ref, a0_ref, a2_ref,
                      kk_ref, ka_ref, rk_ref,
                      wr_o, a_o, w_o, b_o, k_o, v_o, br_o, vkr_o, bonus_o):
    hm = _head_ones()

    def mix(ref, halo_ref, mu_ref):
        p = ref[...]
        prev = _shift_rows(p, halo_ref[...], 1, nb, tq)
        return p + (prev - p) * mu_ref[...]

    r = mix(r_ref, hr_ref, mur_ref)
    k = mix(k_ref, hk_ref, muk_ref)
    v = mix(v_ref, hv_ref, muv_ref)
    lo = mix(lo_ref, hl_ref, mul_ref)
    wf = w0_ref[...] + jnp.dot(jnp.tanh(lo).astype(BF16), w2_ref[...], preferred_element_type=F32)
    decay = jnp.exp(-math.exp(-0.5) * _sigmoid(wf))
    a = _sigmoid(a0_ref[...] + jnp.dot(lo.astype(BF16), a2_ref[...], preferred_element_type=F32))
    kk = k * kk_ref[...]
    kk = kk / jnp.maximum(jnp.sqrt(_head_sum(kk * kk, hm)), 1e-12)
    kmod = k * (1.0 + (a - 1.0) * ka_ref[...])
    b = kk * a
    wr_o[...] = decay * r
    a_o[...] = -kk
    w_o[...] = decay
    b_o[...] = b
    k_o[...] = kmod
    v_o[...] = v
    br_o[...] = _head_sum(b * r, hm)
    vkr_o[...] = _head_sum(kmod * r, hm) * v
    bonus_o[...] = _head_sum(r * kmod * rk_ref[...], hm) * v


def _rwkv_prep(proj, halos, lw, nb, tq):
    t = proj.shape[0]
    tm = nb * tq
    row = lambda cb: pl.BlockSpec((tm, D_BR), lambda i, cb=cb: (i, cb))
    halo = lambda c: pl.BlockSpec((nb, 1, c), lambda i: (i, 0, 0))
    vec = lambda c: pl.BlockSpec((1, c), lambda i: (0, 0))
    mat = pl.BlockSpec((LANES, D_BR), lambda i: (0, 0))
    outs = [jax.ShapeDtypeStruct((t, D_BR), F32)] * 9
    return pl.pallas_call(
        functools.partial(_rwkv_prep_kernel, nb, tq),
        grid=(t // tm,),
        in_specs=[row(0), row(1), row(2),
                  pl.BlockSpec((tm, LANES), lambda i: (i, COL_LORA // LANES)),
                  halo(D_BR), halo(D_BR), halo(D_BR), halo(LANES),
                  vec(D_BR), vec(D_BR), vec(D_BR), vec(LANES),
                  vec(D_BR), mat, vec(D_BR), mat, vec(D_BR), vec(D_BR), vec(D_BR)],
        out_specs=[pl.BlockSpec((tm, D_BR), lambda i: (i, 0))] * 9,
        out_shape=outs,
        compiler_params=_cparams(("parallel",)),
    )(proj, proj, proj, proj, halos["r"], halos["k"], halos["v"], halos["lora"],
      lw["mu_r"], lw["mu_k"], lw["mu_v"], lw["mu_lora"], lw["w0"], lw["w2p"], lw["a0"], lw["a2p"],
      lw["k_k"], lw["k_a"], lw["r_k"])


def _wkv_kernel(ns, tb, wr_ref, a_ref, w_ref, b_ref, k_ref, v_ref, br_ref, s0_ref,
                y_ref, st_ref, p_ref):
    @pl.when(pl.program_id(1) == 0)
    def _():
        st_ref[...] = s0_ref[...]

    hm = _head_ones()
    ri = lax.broadcasted_iota(jnp.int32, (HEAD_A, LANES), 0)
    ci = lax.broadcasted_iota(jnp.int32, (HEAD_A, LANES), 1)
    diag = ((ci % HEAD_A) == ri).astype(F32)

    er = lax.broadcasted_iota(jnp.int32, (SUBLANES, HEAD_A), 0)
    ec = lax.broadcasted_iota(jnp.int32, (SUBLANES, HEAD_A), 1)
    tok_sum = ((ec // SUBLANES) == er).astype(BF16)

    def step(u, carry, base):
        t = base + u
        for s in range(ns):
            row = lambda ref: ref[s, pl.ds(t, 1), :]
            wr, a, w, b, k, v, br = (row(x) for x in (wr_ref, a_ref, w_ref, b_ref, k_ref, v_ref, br_ref))
            for g in range(N_LANE_GROUPS):
                sl = slice(g * LANES, (g + 1) * LANES)
                st = st_ref[s, :, sl]
                m1 = st * a[:, sl]
                m1_hi = m1.astype(BF16)
                m1_lo = (m1 - m1_hi.astype(F32)).astype(BF16)
                m2 = (st * wr[:, sl]).astype(BF16)
                m3 = (diag * v[:, sl]).astype(BF16)
                res = jnp.dot(jnp.concatenate([m1_hi, m1_lo, m2, m3], axis=0), hm,
                              preferred_element_type=F32)
                sa = res[0:HEAD_A] + res[HEAD_A:2 * HEAD_A]
                y2 = res[2 * HEAD_A:3 * HEAD_A]
                vc = res[3 * HEAD_A:4 * HEAD_A]
                st_ref[s, :, sl] = st * w[:, sl] + sa * b[:, sl] + vc * k[:, sl]
                yd = (y2 + sa * br[:, sl]) * diag
                p_ref[s, u, :, sl] = jnp.sum(yd.reshape(HEAD_A // SUBLANES, SUBLANES, LANES), axis=0)
        return carry

    def block8(t8, carry):
        base = pl.multiple_of(t8 * SUBLANES, SUBLANES)
        lax.fori_loop(0, SUBLANES, functools.partial(step, base=base), 0)
        for s in range(ns):
            y_ref[s, pl.ds(base, SUBLANES), :] = _split_dot_rhs(
                tok_sum, p_ref[s].reshape(SUBLANES * SUBLANES, D_BR))
        return carry

    lax.fori_loop(0, tb // SUBLANES, block8, 0)


def _wkv_scan(ops, s0, nseq, tseq):
    ns = 2
    tb = min(tseq, 128)
    ops3 = [o.reshape(nseq, tseq, D_BR) for o in ops]
    blk = pl.BlockSpec((ns, tb, D_BR), lambda i, j: (i, j, 0))
    sblk = pl.BlockSpec((ns, HEAD_A, D_BR), lambda i, j: (i, 0, 0))
    y, st = pl.pallas_call(
        functools.partial(_wkv_kernel, ns, tb),
        grid=(nseq // ns, tseq // tb),
        in_specs=[blk] * 7 + [sblk],
        out_specs=[blk, sblk],
        out_shape=[jax.ShapeDtypeStruct((nseq, tseq, D_BR), F32),
                   jax.ShapeDtypeStruct((nseq, HEAD_A, D_BR), F32)],
        scratch_shapes=[pltpu.VMEM((ns, SUBLANES, SUBLANES, D_BR), F32)],
        compiler_params=_cparams(("parallel", "arbitrary")),
    )(*ops3, s0)
    return y.reshape(nseq * tseq, D_BR), st


def _rope(x, c_ref, s1_ref, s2_ref):
    parts = []
    for h in range(x.shape[1] // LANES):
        xh = x[:, h * LANES:(h + 1) * LANES]
        half = QK_ROPE // 2
        parts.append(xh * c_ref[...] + pltpu.roll(xh, half, axis=1) * s1_ref[...]
                     + pltpu.roll(xh, LANES - half, axis=1) * s2_ref[...])
    return parts[0] if len(parts) == 1 else jnp.concatenate(parts, axis=1)


SOFTMAX_SCALE = (QK_NOPE + QK_ROPE) ** -0.5
LOG2E = math.log2(math.e)


def _mla_common(qa_ref, kva_ref, kpe_ref, qn_ref, kvn_ref, wq_ref, cq_ref, ck_ref, s1_ref, s2_ref,
                scale=SOFTMAX_SCALE):
    qn = _rms(qa_ref[...], qn_ref[...]).astype(BF16)
    q = jnp.dot(qn, wq_ref[...], preferred_element_type=F32)
    q = _rope(q, cq_ref, s1_ref, s2_ref) * scale
    ckv = _rms(kva_ref[...], kvn_ref[...])
    kr = _rope(kpe_ref[...], ck_ref, s1_ref, s2_ref)
    return q, ckv, kr


def _mla_prep_prompt_kernel(qa_ref, kva_ref, kpe_ref, qn_ref, kvn_ref, wq_ref, wk_ref, wv_ref,
                            cq_ref, ck_ref, s1_ref, s2_ref, q_o, k_o, v_o, ckv_o, kr_o):
    q, ckv, kr = _mla_common(qa_ref, kva_ref, kpe_ref, qn_ref, kvn_ref, wq_ref, cq_ref, ck_ref,
                             s1_ref, s2_ref, scale=SOFTMAX_SCALE * LOG2E)
    q_o[...] = q.astype(BF16)
    ckv_o[...] = ckv
    kr_o[...] = kr
    cb = ckv.astype(BF16)
    kn = jnp.dot(cb, wk_ref[...], preferred_element_type=F32)
    k_o[...] = (kn + jnp.concatenate([kr] * H_C, axis=1)).astype(BF16)
    v_o[...] = jnp.dot(cb, wv_ref[...], preferred_element_type=F32).astype(BF16)


def _small_specs(tm, npos_blocks):
    return [pl.BlockSpec((tm, Q_LORA), lambda i: (i, COL_QA // Q_LORA)),
            pl.BlockSpec((tm, KV_LORA), lambda i: (i, COL_KVA // KV_LORA)),
            pl.BlockSpec((tm, LANES), lambda i: (i, COL_KPE // LANES)),
            pl.BlockSpec((1, Q_LORA), lambda i: (0, 0)),
            pl.BlockSpec((1, KV_LORA), lambda i: (0, 0)),
            pl.BlockSpec((Q_LORA, H_C * LANES), lambda i: (0, 0))], \
           [pl.BlockSpec((tm, LANES), lambda i: (i % npos_blocks, 0))] * 4


def _mla_prep_prompt(proj, lw, tabs, seq):
    t = proj.shape[0]
    tm = min(seq, 512)
    head, tab = _small_specs(tm, seq // tm)
    wfull = pl.BlockSpec((KV_LORA, H_C * LANES), lambda i: (0, 0))
    row = lambda c: pl.BlockSpec((tm, c), lambda i: (i, 0))
    return pl.pallas_call(
        _mla_prep_prompt_kernel,
        grid=(t // tm,),
        in_specs=head + [wfull, wfull] + tab,
        out_specs=[row(D_BR), row(D_BR), row(D_BR), row(KV_LORA), row(LANES)],
        out_shape=[jax.ShapeDtypeStruct((t, D_BR), BF16)] * 3
                  + [jax.ShapeDtypeStruct((t, KV_LORA), F32), jax.ShapeDtypeStruct((t, LANES), F32)],
        compiler_params=_cparams(("parallel",)),
    )(proj, proj, proj, lw["q_a_norm"], lw["kv_a_norm"], lw["wq"], lw["wk"], lw["wv"], *tabs)


def _mla_prep_sample_kernel(qa_ref, kva_ref, kpe_ref, qn_ref, kvn_ref, wq_ref, wuk_ref, sel_ref,
                            cq_ref, ck_ref, s1_ref, s2_ref, ql_o, qp_o, ckv_o, kr_o):
    q, ckv, kr = _mla_common(qa_ref, kva_ref, kpe_ref, qn_ref, kvn_ref, wq_ref, cq_ref, ck_ref,
                             s1_ref, s2_ref)
    ckv_o[...] = ckv
    kr_o[...] = kr
    qb = q.astype(BF16)
    for h in range(H_C):
        qh = qb[:, h * LANES:(h + 1) * LANES]
        ql_o[h] = jnp.dot(qh, wuk_ref[h], preferred_element_type=F32)
        qp_o[h] = jnp.dot(qh, sel_ref[...], preferred_element_type=F32)


def _mla_prep_sample(proj, lw, tabs):
    t = proj.shape[0]
    tm = min(t, 256)
    head, tab = _small_specs(tm, 1)
    row = lambda c: pl.BlockSpec((tm, c), lambda i: (i, 0))
    return pl.pallas_call(
        _mla_prep_sample_kernel,
        grid=(t // tm,),
        in_specs=head + [pl.BlockSpec((H_C, LANES, KV_LORA), lambda i: (0, 0, 0)),
                         pl.BlockSpec((LANES, QK_ROPE), lambda i: (0, 0))] + tab,
        out_specs=[pl.BlockSpec((H_C, tm, KV_LORA), lambda i: (0, i, 0)),
                   pl.BlockSpec((H_C, tm, QK_ROPE), lambda i: (0, i, 0)),
                   row(KV_LORA), row(LANES)],
        out_shape=[jax.ShapeDtypeStruct((H_C, t, KV_LORA), F32),
                   jax.ShapeDtypeStruct((H_C, t, QK_ROPE), F32),
                   jax.ShapeDtypeStruct((t, KV_LORA), F32), jax.ShapeDtypeStruct((t, LANES), F32)],
        compiler_params=_cparams(("parallel",)),
    )(proj, proj, proj, lw["q_a_norm"], lw["kv_a_norm"], lw["wq"], lw["wuk_abs"], lw["sel_rope"], *tabs)


NEG = -1e30


def _flash_kernel(tq, sub, q_ref, k_ref, v_ref, o_ref, m_ref, l_ref, acc_ref):
    qi = pl.program_id(2)
    ki = pl.program_id(3)

    @pl.when(ki == 0)
    def _():
        m_ref[...] = jnp.full(m_ref.shape, NEG, F32)
        l_ref[...] = jnp.zeros(l_ref.shape, F32)
        acc_ref[...] = jnp.zeros(acc_ref.shape, F32)

    def block(diagonal):
        for r in range(tq // sub):
            rows = pl.ds(r * sub, sub)
            nk = (r + 1) * sub if diagonal else tq
            s = lax.dot_general(q_ref[rows, :], k_ref[0:nk, :], (((1,), (1,)), ((), ())),
                                preferred_element_type=F32)
            if diagonal:
                qpos = r * sub + lax.broadcasted_iota(jnp.int32, s.shape, 0)
                kpos = lax.broadcasted_iota(jnp.int32, s.shape, 1)
                s = jnp.where(kpos <= qpos, s, NEG)
            m_old = m_ref[rows, :]
            m_new = jnp.maximum(m_old, jnp.max(s, axis=-1, keepdims=True))
            alpha = jnp.exp2(m_old - m_new)
            p = jnp.exp2(s - m_new)
            l_ref[rows, :] = alpha * l_ref[rows, :] + jnp.sum(p, axis=-1, keepdims=True)
            acc_ref[rows, :] = alpha * acc_ref[rows, :] + jnp.dot(
                p.astype(BF16), v_ref[0:nk, :], preferred_element_type=F32)
            m_ref[rows, :] = m_new

    @pl.when(ki < qi)
    def _():
        block(False)

    @pl.when(ki == qi)
    def _():
        block(True)
        o_ref[...] = acc_ref[...] / l_ref[...]


def _flash_prompt(q, k, v, nbatch, seq):
    tq = tk = min(seq, 1024)
    nq = seq // tq
    return pl.pallas_call(
        functools.partial(_flash_kernel, tq, min(tq, 256)),
        grid=(nbatch, H_C, nq, nq),
        in_specs=[pl.BlockSpec((tq, LANES), lambda b, h, qi, ki: (b * nq + qi, h)),
                  pl.BlockSpec((tk, LANES), lambda b, h, qi, ki: (b * nq + jnp.minimum(ki, qi), h)),
                  pl.BlockSpec((tk, LANES), lambda b, h, qi, ki: (b * nq + jnp.minimum(ki, qi), h))],
        out_specs=pl.BlockSpec((tq, LANES), lambda b, h, qi, ki: (b * nq + qi, h)),
        out_shape=jax.ShapeDtypeStruct((nbatch * seq, D_BR), F32),
        scratch_shapes=[pltpu.VMEM((tq, 1), F32), pltpu.VMEM((tq, 1), F32), pltpu.VMEM((tq, LANES), F32)],
        compiler_params=_cparams(("parallel", "parallel", "parallel", "arbitrary")),
    )(q, k, v)


PAGES_PER_STEP = 16


def _joint_softmax_update(tiles, values, m_ref, l_ref, acc_ref):
    m_old = m_ref[...]
    m_new = m_old
    for s in tiles:
        m_new = jnp.maximum(m_new, jnp.max(s, axis=-1, keepdims=True))
    alpha = jnp.exp(m_old - m_new)
    l_new = alpha * l_ref[...]
    acc = alpha * acc_ref[...]
    for s, c in zip(tiles, values):
        p = jnp.exp(s - m_new)
        l_new = l_new + jnp.sum(p, axis=-1, keepdims=True)
        acc = acc + jnp.dot(p.astype(BF16), c, preferred_element_type=F32)
    l_ref[...] = l_new
    acc_ref[...] = acc
    m_ref[...] = m_new


def _paged_kernel(snew, npg, pt_ref, ql_ref, qp_ref, *rest):
    ckv_refs = rest[:npg]
    kr_refs = rest[npg:2 * npg]
    cn_ref, kn_ref, wv_ref, o_ref, m_ref, l_ref, acc_ref = rest[2 * npg:]
    pg = pl.program_id(1)
    rows = H_C * snew
    ql = ql_ref[...].reshape(rows, KV_LORA).astype(BF16)
    qp = qp_ref[...].reshape(rows, QK_ROPE).astype(BF16)
    page = ckv_refs[0].shape[2]

    @pl.when(pg == 0)
    def _():
        m_ref[...] = jnp.full(m_ref.shape, NEG, F32)
        l_ref[...] = jnp.zeros(l_ref.shape, F32)
        acc_ref[...] = jnp.zeros(acc_ref.shape, F32)

    def scores(c, r):
        return (lax.dot_general(ql, c, (((1,), (1,)), ((), ())), preferred_element_type=F32)
                + lax.dot_general(qp, r, (((1,), (1,)), ((), ())), preferred_element_type=F32))

    cs = [ckv_refs[j][0, 0].astype(BF16) for j in range(npg)]
    tiles = [scores(cs[j], kr_refs[j][0, 0].astype(BF16)) for j in range(npg)]
    _joint_softmax_update(tiles, cs, m_ref, l_ref, acc_ref)

    @pl.when(pg == pl.num_programs(1) - 1)
    def _():
        pad = lambda x: jnp.concatenate(
            [x, jnp.zeros((page - snew, x.shape[1]), F32)], axis=0).astype(BF16)
        c = pad(cn_ref[...])
        s = scores(c, pad(kn_ref[...]))
        tok = lax.broadcasted_iota(jnp.int32, s.shape, 0) % snew
        key = lax.broadcasted_iota(jnp.int32, s.shape, 1)
        s = jnp.where(key <= tok, s, NEG)
        _joint_softmax_update([s], [c], m_ref, l_ref, acc_ref)
        o_lat = acc_ref[...] / l_ref[...]
        for h in range(H_C):
            o_ref[:, h * V_HEAD:(h + 1) * V_HEAD] = jnp.dot(
                o_lat[h * snew:(h + 1) * snew].astype(BF16), wv_ref[h], preferred_element_type=F32)


def _paged_attention(ql, qp, cache_ckv, cache_kr, layer, ckv_new, kr_new, wv, page_table, snew):
    nb, n_pages = page_table.shape
    page = cache_ckv.shape[2]
    npg = min(PAGES_PER_STEP, n_pages)
    ckv_specs = [pl.BlockSpec((1, 1, page, KV_LORA),
                              lambda b, g, pt, j=j: (layer, pt[b, g * npg + j], 0, 0)) for j in range(npg)]
    kr_specs = [pl.BlockSpec((1, 1, page, QK_ROPE),
                             lambda b, g, pt, j=j: (layer, pt[b, g * npg + j], 0, 0)) for j in range(npg)]
    rows = H_C * snew
    grid_spec = pltpu.PrefetchScalarGridSpec(
        num_scalar_prefetch=1,
        grid=(nb, n_pages // npg),
        in_specs=[pl.BlockSpec((H_C, snew, KV_LORA), lambda b, g, pt: (0, b, 0)),
                  pl.BlockSpec((H_C, snew, QK_ROPE), lambda b, g, pt: (0, b, 0))]
                 + ckv_specs + kr_specs
                 + [pl.BlockSpec((snew, KV_LORA), lambda b, g, pt: (b, 0)),
                    pl.BlockSpec((snew, QK_ROPE), lambda b, g, pt: (b, 0)),
                    pl.BlockSpec((H_C, KV_LORA, V_HEAD), lambda b, g, pt: (0, 0, 0))],
        out_specs=pl.BlockSpec((snew, D_BR), lambda b, g, pt: (b, 0)),
        scratch_shapes=[pltpu.VMEM((rows, 1), F32), pltpu.VMEM((rows, 1), F32),
                        pltpu.VMEM((rows, KV_LORA), F32)],
    )
    return pl.pallas_call(
        functools.partial(_paged_kernel, snew, npg),
        grid_spec=grid_spec,
        out_shape=jax.ShapeDtypeStruct((nb * snew, D_BR), F32),
        compiler_params=_cparams(("parallel", "arbitrary")),
    )(page_table, ql, qp, *([cache_ckv] * npg), *([cache_kr] * npg), ckv_new, kr_new, wv)


def _merge_kernel(nb, tq, y_ref, vkr_ref, bonus_ref, x_ref, o_ref, ga_ref, cb_ref, cc_ref, cx_ref, gb_ref,
                  gc_ref, ma_ref, mb_ref, mc_ref, hc_ref, hx_ref, lnw_ref, lnb_ref, cw_ref, np_ref,
                  wbr_ref, wout_ref, xo_ref, zt_ref):
    hm = _head_ones()
    tm = nb * tq
    y = y_ref[...] + vkr_ref[...]
    mu = _head_sum(y, hm) * (1.0 / HEAD_A)
    yc = y - mu
    var = _head_sum(yc * yc, hm) * (1.0 / HEAD_A)
    out_a = (yc * lax.rsqrt(var + GN_EPS) * lnw_ref[...] + lnb_ref[...] + bonus_ref[...]) * _silu(ga_ref[...])
    z = cc_ref[...] * cx_ref[...]
    zh = hc_ref[...] * hx_ref[...]
    z1 = _shift_rows(z, zh[:, 1:2, :], 1, nb, tq)
    z2 = _shift_rows(z, zh, 2, nb, tq)
    cw = cw_ref[...]
    conv = cw[0:1] * z2 + cw[1:2] * z1 + cw[2:3] * z
    out_b = cb_ref[...] * conv * _silu(gb_ref[...])
    if nb == 1:
        zt_ref[0] = z[tm - SUBLANES:, :]
    else:
        zt_ref[...] = z.reshape(nb, tq, D_BR)
    out_c = o_ref[...] * _silu(gc_ref[...])
    mixed = (_sigmoid(ma_ref[...]) * jnp.dot(out_a.astype(BF16), wbr_ref[0], preferred_element_type=F32)
             + _sigmoid(mb_ref[...]) * jnp.dot(out_b.astype(BF16), wbr_ref[1], preferred_element_type=F32)
             + _sigmoid(mc_ref[...]) * jnp.dot(out_c.astype(BF16), wbr_ref[2], preferred_element_type=F32))
    res = jnp.dot(mixed.astype(BF16), wout_ref[...], preferred_element_type=F32)
    xo_ref[...] = x_ref[...] + _rms(res, np_ref[...])


def _merge(y, vkr, bonus, x, o, proj, halo_c, halo_x, lw, nb, tq):
    t = x.shape[0]
    tm = nb * tq
    nseg = t // tq
    row = pl.BlockSpec((tm, D_BR), lambda i: (i, 0))
    prow = lambda cb: pl.BlockSpec((tm, D_BR), lambda i, cb=cb: (i, cb))
    halo = pl.BlockSpec((nb, 2, D_BR), lambda i: (i, 0, 0))
    vec = pl.BlockSpec((1, D_BR), lambda i: (0, 0))
    zt_rows = min(tq, SUBLANES)
    return pl.pallas_call(
        functools.partial(_merge_kernel, nb, tq),
        grid=(t // tm,),
        in_specs=[row, row, row, row, row] + [prow(c) for c in range(3, 12)] + [halo, halo, vec, vec,
                  pl.BlockSpec((3, D_BR), lambda i: (0, 0)), vec,
                  pl.BlockSpec((3, D_BR, D_MODEL), lambda i: (0, 0, 0)),
                  pl.BlockSpec((D_MODEL, D_MODEL), lambda i: (0, 0))],
        out_specs=[row, pl.BlockSpec((nb, zt_rows, D_BR), lambda i: (i, 0, 0))],
        out_shape=[jax.ShapeDtypeStruct((t, D_MODEL), F32),
                   jax.ShapeDtypeStruct((nseg, zt_rows, D_BR), F32)],
        compiler_params=_cparams(("parallel",)),
    )(y, vkr, bonus, x, o, *([proj] * 9), halo_c, halo_x, lw["ln_x_w"], lw["ln_x_b"], lw["conv_w"],
      lw["norm_post"], lw["w_branch"], lw["w_out"])


def _layer_weights(l, p):
    w_in = p["w_in"][l]
    z = lambda n: jnp.zeros((D_MODEL, n), F32)
    o_ga = 3 * D_BR + DECAY_LORA + ICLR_LORA
    o_qa = o_ga + 5 * D_BR
    o_kpe = o_qa + Q_LORA + KV_LORA
    o_gc = o_kpe + QK_ROPE
    w_re = jnp.concatenate([
        w_in[:, :3 * D_BR], w_in[:, o_ga:o_qa], w_in[:, o_gc:o_gc + 4 * D_BR],
        w_in[:, o_qa:o_kpe], w_in[:, 3 * D_BR:o_ga],
        z(ROPE_LANE0), w_in[:, o_kpe:o_gc], z(LANES - ROPE_LANE0 - QK_ROPE),
        z(N_PROJ - COL_KPE - LANES)], axis=1).astype(BF16)
    mu = p["mu_shift"][l]
    r2 = lambda a: a.reshape(1, -1)
    w_uq = p["w_uq"][l]
    wq = jnp.concatenate([w_uq, jnp.zeros((Q_LORA, H_C, LANES - QK_NOPE - QK_ROPE), F32)], axis=2)
    w_uk = p["w_uk"][l]
    wk = jnp.concatenate([w_uk, jnp.zeros((KV_LORA, H_C, LANES - QK_NOPE), F32)], axis=2)
    wuk_abs = jnp.concatenate([jnp.transpose(w_uk, (1, 2, 0)),
                               jnp.zeros((H_C, LANES - QK_NOPE, KV_LORA), F32)], axis=1)
    sel = np.zeros((LANES, QK_ROPE), np.float32)
    sel[ROPE_LANE0 + np.arange(QK_ROPE), np.arange(QK_ROPE)] = 1.0
    zl = jnp.zeros((DECAY_LORA, D_BR), F32)
    return {
        "w_in": w_re, "norm_pre": r2(p["norm_pre"][l]), "norm_post": r2(p["norm_post"][l]),
        "mu_r": r2(mu[:D_BR]), "mu_k": r2(mu[D_BR:2 * D_BR]), "mu_v": r2(mu[2 * D_BR:3 * D_BR]),
        "mu_lora": r2(mu[3 * D_BR:]),
        "w0": r2(p["w0"][l]), "a0": r2(p["a0"][l]),
        "w2p": jnp.concatenate([p["w2"][l], zl], axis=0).astype(BF16),
        "a2p": jnp.concatenate([zl, p["a2"][l]], axis=0).astype(BF16),
        "k_k": r2(p["k_k"][l]), "k_a": r2(p["k_a"][l]), "r_k": r2(p["r_k"][l]),
        "ln_x_w": r2(p["ln_x_w"][l]), "ln_x_b": r2(p["ln_x_b"][l]), "conv_w": p["conv_w"][l],
        "q_a_norm": r2(p["q_a_norm"][l]), "kv_a_norm": r2(p["kv_a_norm"][l]),
        "wq": wq.reshape(Q_LORA, H_C * LANES).astype(BF16),
        "wk": wk.reshape(KV_LORA, H_C * LANES).astype(BF16),
        "wv": p["w_uv"][l].reshape(KV_LORA, H_C * V_HEAD).astype(BF16),
        "wuk_abs": wuk_abs.astype(BF16),
        "wv_heads": jnp.transpose(p["w_uv"][l], (1, 0, 2)).astype(BF16),
        "sel_rope": jnp.asarray(sel, BF16),
        "w_branch": p["w_branch"][l].astype(BF16), "w_out": p["w_out"][l].astype(BF16),
    }


def _rope_tables(pos):
    half = QK_ROPE // 2
    freqs = ROPE_THETA ** (-jnp.arange(half, dtype=F32) / half)
    ang = pos.astype(F32)[:, None] * freqs[None, :]
    cos, sin = jnp.cos(ang), jnp.sin(ang)
    n = pos.shape[0]
    lo = jnp.zeros((n, ROPE_LANE0), F32)
    hi = jnp.zeros((n, LANES - ROPE_LANE0 - QK_ROPE), F32)
    zh = jnp.zeros((n, half), F32)
    ck = jnp.concatenate([lo, cos, cos, hi], axis=1)
    cq = jnp.concatenate([lo + 1.0, cos, cos, hi], axis=1)
    s1 = jnp.concatenate([lo, zh, sin, hi], axis=1)
    s2 = jnp.concatenate([lo, -sin, zh, hi], axis=1)
    return cq, ck, s1, s2


def _prev_row_halos(proj, first_rows, nseq, tseq, tq):
    nseg_per = tseq // tq
    out = {}
    for name, (c0, c1) in {"r": (COL_R, COL_R + D_BR), "k": (COL_K, COL_K + D_BR),
                           "v": (COL_V, COL_V + D_BR), "lora": (COL_LORA, COL_LORA + LANES)}.items():
        first = first_rows[name].reshape(nseq, 1, c1 - c0)
        if nseg_per > 1:
            p3 = proj.reshape(nseq, tseq, N_PROJ)
            inner = p3[:, tq - 1:tseq - 1:tq, c0:c1]
            first = jnp.concatenate([first, inner], axis=1)
        out[name] = first.reshape(nseq * nseg_per, 1, c1 - c0)
    return out


def _conv_halos(proj, conv_prev, nseq, tseq, tq):
    nseg_per = tseq // tq
    hc = conv_prev
    hx = jnp.ones_like(conv_prev)
    if nseg_per > 1:
        p3 = proj.reshape(nseq, nseg_per, tq, N_PROJ)[:, :-1, tq - 2:, :]
        hc = jnp.concatenate([hc[:, None], p3[..., COL_CC:COL_CC + D_BR]], axis=1)
        hx = jnp.concatenate([hx[:, None], p3[..., COL_CX:COL_CX + D_BR]], axis=1)
    return hc.reshape(nseq * nseg_per, 2, D_BR), hx.reshape(nseq * nseg_per, 2, D_BR)


def _layer(x, lw, tabs, nseq, tseq, shift_prev, conv_prev, wkv_prev, attn):
    t = nseq * tseq
    proj = _inproj(x, lw["norm_pre"], lw["w_in"])
    if tseq >= 128:
        nb, tq = 1, 128
    else:
        nb, tq = min(nseq, 128 // tseq), tseq
    first = {"r": shift_prev[:, :D_BR], "k": shift_prev[:, D_BR:2 * D_BR],
             "v": shift_prev[:, 2 * D_BR:3 * D_BR], "lora": shift_prev[:, 3 * D_BR:]}
    ops = _rwkv_prep(proj, _prev_row_halos(proj, first, nseq, tseq, tq), lw, nb, tq)
    vkr, bonus = ops[7], ops[8]
    s0 = jnp.transpose(wkv_prev, (0, 2, 1, 3)).reshape(nseq, HEAD_A, D_BR)
    y, st = _wkv_scan(ops[:7], s0, nseq, tseq)
    wkv_new = jnp.transpose(st.reshape(nseq, HEAD_A, H_A, HEAD_A), (0, 2, 1, 3))
    o, ckv, kr128 = attn(proj, lw, tabs)
    hc, hx = _conv_halos(proj, conv_prev, nseq, tseq, tq)
    x_new, zt = _merge(y, vkr, bonus, x, o, proj, hc, hx, lw, nb, tq)
    p3 = proj.reshape(nseq, tseq, N_PROJ)[:, -1, :]
    new_shift = jnp.concatenate([p3[:, :3 * D_BR], p3[:, COL_LORA:COL_LORA + LANES]], axis=1)
    new_conv = zt.reshape(nseq, tseq // tq, -1, D_BR)[:, -1, -2:, :]
    kr = kr128[:, ROPE_LANE0:ROPE_LANE0 + QK_ROPE]
    return x_new, (ckv.reshape(nseq, tseq, KV_LORA), kr.reshape(nseq, tseq, QK_ROPE), wkv_new,
                   new_shift, new_conv)


def kernel(x_prompt, x_sample, cache_ckv, cache_krope, state_wkv, state_shift, state_conv, page_table,
           norm_pre, norm_post, w_in, mu_shift, w0, w2, a0, a2, k_k, k_a, r_k, ln_x_w, ln_x_b,
           conv_w, q_a_norm, w_uq, kv_a_norm, w_uk, w_uv, w_branch, w_out):
    params = dict(norm_pre=norm_pre, norm_post=norm_post, w_in=w_in, mu_shift=mu_shift, w0=w0, w2=w2,
                  a0=a0, a2=a2, k_k=k_k, k_a=k_a, r_k=r_k.reshape(r_k.shape[0], -1), ln_x_w=ln_x_w,
                  ln_x_b=ln_x_b, conv_w=conv_w, q_a_norm=q_a_norm, w_uq=w_uq, kv_a_norm=kv_a_norm,
                  w_uk=w_uk, w_uv=w_uv, w_branch=w_branch, w_out=w_out)
    bp, sp, _ = x_prompt.shape
    bd, sd, _ = x_sample.shape
    depth = w_in.shape[0]
    past_len = page_table.shape[1] * cache_ckv.shape[2]
    tabs_p = _rope_tables(jnp.arange(sp))
    tm_s = min(bd * sd, 256)
    tabs_s = _rope_tables(past_len + (jnp.arange(tm_s) % sd))

    xp = x_prompt.reshape(bp * sp, D_MODEL)
    xs = x_sample.reshape(bd * sd, D_MODEL)
    acc_p, acc_s = [], []
    for l in range(depth):
        lw = _layer_weights(l, params)

        def attn_prompt(proj, lw, tabs):
            q, k, v, ckv, kr128 = _mla_prep_prompt(proj, lw, tabs, sp)
            return _flash_prompt(q, k, v, bp, sp), ckv, kr128

        def attn_sample(proj, lw, tabs, l=l):
            ql, qp, ckv, kr128 = _mla_prep_sample(proj, lw, tabs)
            kr_new = kr128[:, ROPE_LANE0:ROPE_LANE0 + QK_ROPE]
            o = _paged_attention(ql, qp, cache_ckv, cache_krope, l, ckv, kr_new, lw["wv_heads"],
                                 page_table, sd)
            return o, ckv, kr128

        xp, st_p = _layer(xp, lw, tabs_p, bp, sp, jnp.zeros((bp, state_shift.shape[2]), F32),
                          jnp.zeros((bp, 2, D_BR), F32), jnp.zeros((bp, H_A, HEAD_A, HEAD_A), F32),
                          attn_prompt)
        xs, st_s = _layer(xs, lw, tabs_s, bd, sd, state_shift[l], state_conv[l], state_wkv[l],
                          attn_sample)
        acc_p.append(st_p)
        acc_s.append(st_s)
    outs_p = [jnp.stack([a[i] for a in acc_p], axis=0) for i in range(5)]
    outs_s = [jnp.stack([a[i] for a in acc_s], axis=0) for i in range(5)]
    return (xp.reshape(bp, sp, D_MODEL), xs.reshape(bd, sd, D_MODEL), *outs_p, *outs_s)
```

```python
import functools
import math

import numpy as np
import jax
import jax.numpy as jnp
from jax import lax
from jax.experimental import pallas as pl
from jax.experimental.pallas import tpu as pltpu

F32 = jnp.float32
BF16 = jnp.bfloat16

D_MODEL = 1024
D_BR = D_MODEL
HEAD_A = 64
H_A = D_BR // HEAD_A
DECAY_LORA = 64
ICLR_LORA = 64
GN_EPS = 64e-5
H_C = 8
QK_NOPE = 64
QK_ROPE = 32
V_HEAD = D_BR // H_C
Q_LORA = D_MODEL // 4
KV_LORA = D_MODEL // 4
ROPE_THETA = 10000.0
NORM_EPS = 1e-6
LANES = 128
SUBLANES = 8
N_LANE_GROUPS = D_BR // LANES
ROPE_LANE0 = 64
VMEM_LIMIT = 56 * 1024 * 1024

COL_R, COL_K, COL_V, COL_GA, COL_CB, COL_CC, COL_CX, COL_GB, COL_GC, COL_MA, COL_MB, COL_MC = (
    i * D_BR for i in range(12))
COL_QA = 12 * D_BR
COL_KVA = COL_QA + Q_LORA
COL_LORA = COL_KVA + KV_LORA
COL_KPE = COL_LORA + LANES
N_PROJ = 13 * D_BR


def _cparams(sem):
    return pltpu.CompilerParams(dimension_semantics=sem, vmem_limit_bytes=VMEM_LIMIT)


def _head_ones():
    r = lax.broadcasted_iota(jnp.int32, (LANES, LANES), 0) // HEAD_A
    c = lax.broadcasted_iota(jnp.int32, (LANES, LANES), 1) // HEAD_A
    return (r == c).astype(BF16)


def _split_dot(x, w):
    hi = x.astype(BF16)
    lo = (x - hi.astype(F32)).astype(BF16)
    return (jnp.dot(hi, w, preferred_element_type=F32) + jnp.dot(lo, w, preferred_element_type=F32))


def _split_dot_rhs(w, x):
    hi = x.astype(BF16)
    lo = (x - hi.astype(F32)).astype(BF16)
    return (jnp.dot(w, hi, preferred_element_type=F32) + jnp.dot(w, lo, preferred_element_type=F32))


def _head_sum(x, hm):
    parts = [_split_dot(x[:, g * LANES:(g + 1) * LANES], hm) for g in range(x.shape[1] // LANES)]
    return jnp.concatenate(parts, axis=1)


def _shift_rows(p, halo, k, nb, tq):
    c = p.shape[1]
    prev = pltpu.roll(p, k, axis=0)
    pos = lax.broadcasted_iota(jnp.int32, p.shape, 0) % tq
    out = prev
    for j in range(k):
        hb = jnp.broadcast_to(halo[:, j:j + 1, :], (nb, tq, c)).reshape(nb * tq, c)
        out = jnp.where(pos == j, hb, out)
    return out


def _sigmoid(x):
    return 1.0 / (1.0 + jnp.exp(-x))


def _silu(x):
    return x * _sigmoid(x)


def _rms(x, g):
    return x * lax.rsqrt(jnp.mean(x * x, axis=-1, keepdims=True) + NORM_EPS) * g


def _inproj_kernel(x_ref, g_ref, w_ref, o_ref, h_ref):
    @pl.when(pl.program_id(1) == 0)
    def _():
        h_ref[...] = _rms(x_ref[...], g_ref[...]).astype(BF16)

    o_ref[...] = jnp.dot(h_ref[...], w_ref[...], preferred_element_type=F32)


def _inproj(x, g, w):
    t = x.shape[0]
    tm = min(t, 1024)
    tn = 1024
    return pl.pallas_call(
        _inproj_kernel,
        grid=(t // tm, N_PROJ // tn),
        in_specs=[pl.BlockSpec((tm, D_MODEL), lambda i, j: (i, 0)),
                  pl.BlockSpec((1, D_MODEL), lambda i, j: (0, 0)),
                  pl.BlockSpec((D_MODEL, tn), lambda i, j: (0, j))],
        out_specs=pl.BlockSpec((tm, tn), lambda i, j: (i, j)),
        out_shape=jax.ShapeDtypeStruct((t, N_PROJ), F32),
        scratch_shapes=[pltpu.VMEM((tm, D_MODEL), BF16)],
        compiler_params=_cparams(("parallel", "arbitrary")),
    )(x, g, w)


def _rwkv_prep_kernel(nb, tq, chunked, r_ref, k_ref, v_ref, lo_ref, hr_ref, hk_ref, hv_ref, hl_ref,
                      mur_ref, muk_ref, muv_ref, mul_ref, w0_ref, w2_ref, a0_ref, a2_ref,
                      kk_ref, ka_ref, rk_ref, *outs):
    hm = _head_ones()

    def mix(ref, halo_ref, mu_ref):
        p = ref[...]
        prev = _shift_rows(p, halo_ref[...], 1, nb, tq)
        return p + (prev - p) * mu_ref[...]

    r = mix(r_ref, hr_ref, mur_ref)
    k = mix(k_ref, hk_ref, muk_ref)
    v = mix(v_ref, hv_ref, muv_ref)
    lo = mix(lo_ref, hl_ref, mul_ref)
    wf = w0_ref[...] + jnp.dot(jnp.tanh(lo).astype(BF16), w2_ref[...], preferred_element_type=F32)
    log_decay = -math.exp(-0.5) * _sigmoid(wf)
    a = _sigmoid(a0_ref[...] + jnp.dot(lo.astype(BF16), a2_ref[...], preferred_element_type=F32))
    kk = k * kk_ref[...]
    kk = kk / jnp.maximum(jnp.sqrt(_head_sum(kk * kk, hm)), 1e-12)
    kmod = k * (1.0 + (a - 1.0) * ka_ref[...])
    b = kk * a
    bonus = _head_sum(r * kmod * rk_ref[...], hm) * v
    if chunked:
        *group_outs, bonus_o = outs
        bonus_o[...] = bonus
        for o_ref, val in zip(group_outs, (r, log_decay, kmod, v, -kk, b)):
            for g in range(N_LANE_GROUPS):
                o_ref[g] = val[:, g * LANES:(g + 1) * LANES]
        return
    wr_o, a_o, w_o, b_o, k_o, v_o, br_o, vkr_o, bonus_o = outs
    decay = jnp.exp(log_decay)
    wr_o[...] = decay * r
    a_o[...] = -kk
    w_o[...] = decay
    b_o[...] = b
    k_o[...] = kmod
    v_o[...] = v
    br_o[...] = _head_sum(b * r, hm)
    vkr_o[...] = _head_sum(kmod * r, hm) * v
    bonus_o[...] = bonus


def _rwkv_prep(proj, halos, lw, nb, tq, chunked):
    t = proj.shape[0]
    tm = nb * tq
    row_out = pl.BlockSpec((tm, D_BR), lambda i: (i, 0))
    if chunked:
        out_specs = [pl.BlockSpec((N_LANE_GROUPS, tm, LANES), lambda i: (0, i, 0))] * 6 + [row_out]
        outs = [jax.ShapeDtypeStruct((N_LANE_GROUPS, t, LANES), F32)] * 6 + [
            jax.ShapeDtypeStruct((t, D_BR), F32)]
    else:
        out_specs = [row_out] * 9
        outs = [jax.ShapeDtypeStruct((t, D_BR), F32)] * 9
    row = lambda cb: pl.BlockSpec((tm, D_BR), lambda i, cb=cb: (i, cb))
    halo = lambda c: pl.BlockSpec((nb, 1, c), lambda i: (i, 0, 0))
    vec = lambda c: pl.BlockSpec((1, c), lambda i: (0, 0))
    mat = pl.BlockSpec((LANES, D_BR), lambda i: (0, 0))
    return pl.pallas_call(
        functools.partial(_rwkv_prep_kernel, nb, tq, chunked),
        grid=(t // tm,),
        in_specs=[row(0), row(1), row(2),
                  pl.BlockSpec((tm, LANES), lambda i: (i, COL_LORA // LANES)),
                  halo(D_BR), halo(D_BR), halo(D_BR), halo(LANES),
                  vec(D_BR), vec(D_BR), vec(D_BR), vec(LANES),
                  vec(D_BR), mat, vec(D_BR), mat, vec(D_BR), vec(D_BR), vec(D_BR)],
        out_specs=out_specs,
        out_shape=outs,
        compiler_params=_cparams(("parallel",)),
    )(proj, proj, proj, proj, halos["r"], halos["k"], halos["v"], halos["lora"],
      lw["mu_r"], lw["mu_k"], lw["mu_v"], lw["mu_lora"], lw["w0"], lw["w2p"], lw["a0"], lw["a2p"],
      lw["k_k"], lw["k_a"], lw["r_k"])


def _wkv_kernel(ns, tb, wr_ref, a_ref, w_ref, b_ref, k_ref, v_ref, br_ref, s0_ref,
                y_ref, st_ref, p_ref):
    @pl.when(pl.program_id(1) == 0)
    def _():
        st_ref[...] = s0_ref[...]

    hm = _head_ones()
    ri = lax.broadcasted_iota(jnp.int32, (HEAD_A, LANES), 0)
    ci = lax.broadcasted_iota(jnp.int32, (HEAD_A, LANES), 1)
    diag = ((ci % HEAD_A) == ri).astype(F32)

    er = lax.broadcasted_iota(jnp.int32, (SUBLANES, HEAD_A), 0)
    ec = lax.broadcasted_iota(jnp.int32, (SUBLANES, HEAD_A), 1)
    tok_sum = ((ec // SUBLANES) == er).astype(BF16)

    def step(u, carry, base):
        t = base + u
        for s in range(ns):
            row = lambda ref: ref[s, pl.ds(t, 1), :]
            wr, a, w, b, k, v, br = (row(x) for x in (wr_ref, a_ref, w_ref, b_ref, k_ref, v_ref, br_ref))
            for g in range(N_LANE_GROUPS):
                sl = slice(g * LANES, (g + 1) * LANES)
                st = st_ref[s, :, sl]
                m1 = st * a[:, sl]
                m1_hi = m1.astype(BF16)
                m1_lo = (m1 - m1_hi.astype(F32)).astype(BF16)
                m2 = (st * wr[:, sl]).astype(BF16)
                m3 = (diag * v[:, sl]).astype(BF16)
                res = jnp.dot(jnp.concatenate([m1_hi, m1_lo, m2, m3], axis=0), hm,
                              preferred_element_type=F32)
                sa = res[0:HEAD_A] + res[HEAD_A:2 * HEAD_A]
                y2 = res[2 * HEAD_A:3 * HEAD_A]
                vc = res[3 * HEAD_A:4 * HEAD_A]
                st_ref[s, :, sl] = st * w[:, sl] + sa * b[:, sl] + vc * k[:, sl]
                yd = (y2 + sa * br[:, sl]) * diag
                p_ref[s, u, :, sl] = jnp.sum(yd.reshape(HEAD_A // SUBLANES, SUBLANES, LANES), axis=0)
        return carry

    def block8(t8, carry):
        base = pl.multiple_of(t8 * SUBLANES, SUBLANES)
        lax.fori_loop(0, SUBLANES, functools.partial(step, base=base), 0)
        for s in range(ns):
            y_ref[s, pl.ds(base, SUBLANES), :] = _split_dot_rhs(
                tok_sum, p_ref[s].reshape(SUBLANES * SUBLANES, D_BR))
        return carry

    lax.fori_loop(0, tb // SUBLANES, block8, 0)


def _wkv_scan(ops, s0, nseq, tseq):
    ns = 2
    tb = min(tseq, 128)
    ops3 = [o.reshape(nseq, tseq, D_BR) for o in ops]
    blk = pl.BlockSpec((ns, tb, D_BR), lambda i, j: (i, j, 0))
    sblk = pl.BlockSpec((ns, HEAD_A, D_BR), lambda i, j: (i, 0, 0))
    y, st = pl.pallas_call(
        functools.partial(_wkv_kernel, ns, tb),
        grid=(nseq // ns, tseq // tb),
        in_specs=[blk] * 7 + [sblk],
        out_specs=[blk, sblk],
        out_shape=[jax.ShapeDtypeStruct((nseq, tseq, D_BR), F32),
                   jax.ShapeDtypeStruct((nseq, HEAD_A, D_BR), F32)],
        scratch_shapes=[pltpu.VMEM((ns, SUBLANES, SUBLANES, D_BR), F32)],
        compiler_params=_cparams(("parallel", "arbitrary")),
    )(*ops3, s0)
    return y.reshape(nseq * tseq, D_BR), st


WKV_CHUNK = 64
WKV_PASSES_LOCAL = 1
WKV_PASSES_STATE = 3


def _dot_nn(a, b):
    return jnp.dot(a, b, preferred_element_type=F32)


def _dot_nt(a, b):
    return lax.dot_general(a, b, (((1,), (1,)), ((), ())), preferred_element_type=F32)


def _dot_tn(a, b):
    return lax.dot_general(a, b, (((0,), (0,)), ((), ())), preferred_element_type=F32)


def _hilo(x):
    hi = x.astype(BF16)
    return hi, (x - hi.astype(F32)).astype(BF16)


def _mm(dot, a, b, passes):
    ah, al = _hilo(a)
    bh, bl = _hilo(b)
    out = dot(ah, bh)
    if passes >= 3:
        out = out + dot(al, bh) + dot(ah, bl)
    return out


def _wkv_chunk_kernel(ns, r_ref, lw_ref, k_ref, v_ref, a_ref, b_ref, s0_ref, y_ref, st_ref):
    L = r_ref.shape[2]

    @pl.when(pl.program_id(1) == 0)
    def _():
        st_ref[...] = s0_ref[...]

    ti = lax.broadcasted_iota(jnp.int32, (L, L), 0)
    si = lax.broadcasted_iota(jnp.int32, (L, L), 1)
    incl = si <= ti
    strict = si < ti
    eye_l = (si == ti).astype(F32)
    tril_ones = incl.astype(BF16)
    levels = []
    m = 1
    while m < L:
        levels.append(((ti // (2 * m)) == (si // (2 * m))) & ((ti % (2 * m)) >= m) & ((si % (2 * m)) < m))
        m *= 2
    lane = lax.broadcasted_iota(jnp.int32, (1, LANES), 1)
    head_masks = [(lane < HEAD_A).astype(F32), (lane >= HEAD_A).astype(F32)]
    gr = lax.broadcasted_iota(jnp.int32, (LANES, LANES), 0)
    gc = lax.broadcasted_iota(jnp.int32, (LANES, LANES), 1)
    same_head = (gr // HEAD_A) == (gc // HEAD_A)
    eye_g = gr == gc

    def each(fn, *lists):
        return [fn(*xs) for xs in zip(*lists)]

    def mm(dot, xs, ys, passes=WKV_PASSES_LOCAL):
        return each(lambda x, y: _mm(dot, x, y, passes), xs, ys)

    chains = [(s, g) for s in range(ns) for g in range(N_LANE_GROUPS)]
    r, lw, k, v, a, b = ([ref[g, s] for s, g in chains]
                         for ref in (r_ref, lw_ref, k_ref, v_ref, a_ref, b_ref))
    st = [st_ref[s, g] for s, g in chains]

    def cumsum(x):
        hi, lo = _hilo(x)
        return _dot_nn(tril_ones, hi) + _dot_nn(tril_ones, lo)

    c = each(cumsum, lw)
    c_last = [x[L - 1:L, :] for x in c]
    e_neg = [jnp.exp(-x) for x in c]
    e_end = each(lambda cl, x: jnp.exp(cl - x), c_last, c)
    at = each(lambda a_, x, w_: a_ * jnp.exp(x - w_), a, c, lw)
    rt = each(lambda r_, x: r_ * jnp.exp(x), r, c)
    bt = each(jnp.multiply, b, e_neg)
    kt = each(jnp.multiply, k, e_neg)
    bend = each(jnp.multiply, b, e_end)
    kend = each(jnp.multiply, k, e_end)

    def per_head(xs):
        return [x * hm_ for x in xs for hm_ in head_masks]

    def per_group(xs):
        return [x for x in xs for _ in head_masks]

    at_h, rt_h, v_h = per_head(at), per_head(rt), per_head(v)
    q2 = each(lambda x, y: jnp.concatenate([x, y], axis=0), at_h, rt_h)
    gb = mm(_dot_nt, q2, per_group(bt))
    gk = mm(_dot_nt, q2, per_group(kt))
    n = [jnp.where(strict, x[:L], 0.0) for x in gb]
    pb = [jnp.where(incl, x[L:], 0.0) for x in gb]
    m_ = [jnp.where(strict, x[:L], 0.0) for x in gk]
    pk = [jnp.where(incl, x[L:], 0.0) for x in gk]
    t_inv = [eye_l + jnp.where(levels[0], x, 0.0) for x in n]
    for lev in levels[1:]:
        cm = [jnp.where(lev, x, 0.0) for x in n]
        t_inv = each(jnp.add, t_inv, mm(_dot_nn, mm(_dot_nn, t_inv, cm), t_inv))
    abar_h = mm(_dot_nn, t_inv, at_h)
    uv_h = mm(_dot_nn, mm(_dot_nn, t_inv, m_), v_h)
    rbar_h = mm(_dot_nn, pb, abar_h)
    yv_h = each(jnp.add, mm(_dot_nn, pb, uv_h), mm(_dot_nn, pk, v_h))

    def head_total(xs):
        return [xs[2 * i] + xs[2 * i + 1] for i in range(len(chains))]

    abar, uv, yv = head_total(abar_h), head_total(uv_h), head_total(yv_h)
    rbar = each(jnp.add, rt, head_total(rbar_h))
    phi = each(lambda x, cl: jnp.where(same_head, x, 0.0) + jnp.where(eye_g, jnp.exp(cl), 0.0),
               mm(_dot_tn, abar, bend), c_last)
    psi = each(lambda x, y: jnp.where(same_head, x + y, 0.0), mm(_dot_tn, uv, bend), mm(_dot_tn, v, kend))
    y = each(jnp.add, mm(_dot_nt, rbar, st, WKV_PASSES_STATE), yv)
    st_new = each(jnp.add, mm(_dot_nn, st, phi, WKV_PASSES_STATE), psi)
    for (s, g), y_, st_ in zip(chains, y, st_new):
        y_ref[g, s] = y_
        st_ref[s, g] = st_


def _wkv_chunked(ops, s0, nseq, tseq):
    ns = 2
    L = WKV_CHUNK
    ops4 = [o.reshape(N_LANE_GROUPS, nseq, tseq, LANES) for o in ops]
    blk = pl.BlockSpec((N_LANE_GROUPS, ns, L, LANES), lambda i, j: (0, i, j, 0))
    sblk = pl.BlockSpec((ns, N_LANE_GROUPS, LANES, LANES), lambda i, j: (i, 0, 0, 0))
    y, st = pl.pallas_call(
        functools.partial(_wkv_chunk_kernel, ns),
        grid=(nseq // ns, tseq // L),
        in_specs=[blk] * 6 + [sblk],
        out_specs=[blk, sblk],
        out_shape=[jax.ShapeDtypeStruct((N_LANE_GROUPS, nseq, tseq, LANES), F32),
                   jax.ShapeDtypeStruct((nseq, N_LANE_GROUPS, LANES, LANES), F32)],
        compiler_params=_cparams(("parallel", "arbitrary")),
    )(*ops4, s0)
    return y.reshape(N_LANE_GROUPS, nseq * tseq, LANES), st


def _rope(x, c_ref, s1_ref, s2_ref):
    parts = []
    for h in range(x.shape[1] // LANES):
        xh = x[:, h * LANES:(h + 1) * LANES]
        half = QK_ROPE // 2
        parts.append(xh * c_ref[...] + pltpu.roll(xh, half, axis=1) * s1_ref[...]
                     + pltpu.roll(xh, LANES - half, axis=1) * s2_ref[...])
    return parts[0] if len(parts) == 1 else jnp.concatenate(parts, axis=1)


SOFTMAX_SCALE = (QK_NOPE + QK_ROPE) ** -0.5
LOG2E = math.log2(math.e)


def _mla_common(qa_ref, kva_ref, kpe_ref, qn_ref, kvn_ref, wq_ref, cq_ref, ck_ref, s1_ref, s2_ref,
                scale=SOFTMAX_SCALE):
    qn = _rms(qa_ref[...], qn_ref[...]).astype(BF16)
    q = jnp.dot(qn, wq_ref[...], preferred_element_type=F32)
    q = _rope(q, cq_ref, s1_ref, s2_ref) * scale
    ckv = _rms(kva_ref[...], kvn_ref[...])
    kr = _rope(kpe_ref[...], ck_ref, s1_ref, s2_ref)
    return q, ckv, kr


def _mla_prep_prompt_kernel(qa_ref, kva_ref, kpe_ref, qn_ref, kvn_ref, wq_ref, wk_ref, wv_ref,
                            cq_ref, ck_ref, s1_ref, s2_ref, q_o, k_o, v_o, ckv_o, kr_o):
    q, ckv, kr = _mla_common(qa_ref, kva_ref, kpe_ref, qn_ref, kvn_ref, wq_ref, cq_ref, ck_ref,
                             s1_ref, s2_ref, scale=SOFTMAX_SCALE * LOG2E)
    ckv_o[...] = ckv
    kr_o[...] = kr
    cb = ckv.astype(BF16)
    kn = jnp.dot(cb, wk_ref[...], preferred_element_type=F32)
    vv = jnp.dot(cb, wv_ref[...], preferred_element_type=F32)
    for h in range(H_C):
        sl = slice(h * LANES, (h + 1) * LANES)
        q_o[h] = q[:, sl].astype(BF16)
        k_o[h] = (kn[:, sl] + kr).astype(BF16)
        v_o[h] = vv[:, sl].astype(BF16)


def _small_specs(tm, npos_blocks):
    return [pl.BlockSpec((tm, Q_LORA), lambda i: (i, COL_QA // Q_LORA)),
            pl.BlockSpec((tm, KV_LORA), lambda i: (i, COL_KVA // KV_LORA)),
            pl.BlockSpec((tm, LANES), lambda i: (i, COL_KPE // LANES)),
            pl.BlockSpec((1, Q_LORA), lambda i: (0, 0)),
            pl.BlockSpec((1, KV_LORA), lambda i: (0, 0)),
            pl.BlockSpec((Q_LORA, H_C * LANES), lambda i: (0, 0))], \
           [pl.BlockSpec((tm, LANES), lambda i: (i % npos_blocks, 0))] * 4


def _mla_prep_prompt(proj, lw, tabs, seq):
    t = proj.shape[0]
    tm = min(seq, 512)
    head, tab = _small_specs(tm, seq // tm)
    wfull = pl.BlockSpec((KV_LORA, H_C * LANES), lambda i: (0, 0))
    row = lambda c: pl.BlockSpec((tm, c), lambda i: (i, 0))
    return pl.pallas_call(
        _mla_prep_prompt_kernel,
        grid=(t // tm,),
        in_specs=head + [wfull, wfull] + tab,
        out_specs=[pl.BlockSpec((H_C, tm, LANES), lambda i: (0, i, 0))] * 3 + [row(KV_LORA), row(LANES)],
        out_shape=[jax.ShapeDtypeStruct((H_C, t, LANES), BF16)] * 3
                  + [jax.ShapeDtypeStruct((t, KV_LORA), F32), jax.ShapeDtypeStruct((t, LANES), F32)],
        compiler_params=_cparams(("parallel",)),
    )(proj, proj, proj, lw["q_a_norm"], lw["kv_a_norm"], lw["wq"], lw["wk"], lw["wv"], *tabs)


def _mla_prep_sample_kernel(qa_ref, kva_ref, kpe_ref, qn_ref, kvn_ref, wq_ref, wuk_ref, sel_ref,
                            cq_ref, ck_ref, s1_ref, s2_ref, ql_o, qp_o, ckv_o, kr_o):
    q, ckv, kr = _mla_common(qa_ref, kva_ref, kpe_ref, qn_ref, kvn_ref, wq_ref, cq_ref, ck_ref,
                             s1_ref, s2_ref)
    ckv_o[...] = ckv
    kr_o[...] = kr
    qb = q.astype(BF16)
    for h in range(H_C):
        qh = qb[:, h * LANES:(h + 1) * LANES]
        ql_o[h] = jnp.dot(qh, wuk_ref[h], preferred_element_type=F32)
        qp_o[h] = jnp.dot(qh, sel_ref[...], preferred_element_type=F32)


def _mla_prep_sample(proj, lw, tabs):
    t = proj.shape[0]
    tm = min(t, 256)
    head, tab = _small_specs(tm, 1)
    row = lambda c: pl.BlockSpec((tm, c), lambda i: (i, 0))
    return pl.pallas_call(
        _mla_prep_sample_kernel,
        grid=(t // tm,),
        in_specs=head + [pl.BlockSpec((H_C, LANES, KV_LORA), lambda i: (0, 0, 0)),
                         pl.BlockSpec((LANES, QK_ROPE), lambda i: (0, 0))] + tab,
        out_specs=[pl.BlockSpec((H_C, tm, KV_LORA), lambda i: (0, i, 0)),
                   pl.BlockSpec((H_C, tm, QK_ROPE), lambda i: (0, i, 0)),
                   row(KV_LORA), row(LANES)],
        out_shape=[jax.ShapeDtypeStruct((H_C, t, KV_LORA), F32),
                   jax.ShapeDtypeStruct((H_C, t, QK_ROPE), F32),
                   jax.ShapeDtypeStruct((t, KV_LORA), F32), jax.ShapeDtypeStruct((t, LANES), F32)],
        compiler_params=_cparams(("parallel",)),
    )(proj, proj, proj, lw["q_a_norm"], lw["kv_a_norm"], lw["wq"], lw["wuk_abs"], lw["sel_rope"], *tabs)


NEG = -1e30


def _flash_kernel(tq, sub, q_ref, k_ref, v_ref, o_ref, m_ref, l_ref, acc_ref):
    qi = pl.program_id(2)
    ki = pl.program_id(3)

    @pl.when(ki == 0)
    def _():
        m_ref[...] = jnp.full(m_ref.shape, NEG, F32)
        l_ref[...] = jnp.zeros(l_ref.shape, F32)
        acc_ref[...] = jnp.zeros(acc_ref.shape, F32)

    def block(diagonal):
        for r in range(tq // sub):
            rows = pl.ds(r * sub, sub)
            nk = (r + 1) * sub if diagonal else tq
            s = lax.dot_general(q_ref[0, rows, :], k_ref[0, 0:nk, :], (((1,), (1,)), ((), ())),
                                preferred_element_type=F32)
            if diagonal:
                qpos = r * sub + lax.broadcasted_iota(jnp.int32, s.shape, 0)
                kpos = lax.broadcasted_iota(jnp.int32, s.shape, 1)
                s = jnp.where(kpos <= qpos, s, NEG)
            m_old = m_ref[rows, :]
            m_new = jnp.maximum(m_old, jnp.max(s, axis=-1, keepdims=True))
            alpha = jnp.exp2(m_old - m_new)
            p = jnp.exp2(s - m_new)
            l_ref[rows, :] = alpha * l_ref[rows, :] + jnp.sum(p, axis=-1, keepdims=True)
            acc_ref[rows, :] = alpha * acc_ref[rows, :] + jnp.dot(
                p.astype(BF16), v_ref[0, 0:nk, :], preferred_element_type=F32)
            m_ref[rows, :] = m_new

    @pl.when(ki < qi)
    def _():
        block(False)

    @pl.when(ki == qi)
    def _():
        block(True)
        o_ref[0] = acc_ref[...] / l_ref[...]


def _flash_prompt(q, k, v, nbatch, seq):
    tq = tk = min(seq, 1024)
    nq = seq // tq
    return pl.pallas_call(
        functools.partial(_flash_kernel, tq, min(tq, 256)),
        grid=(nbatch, H_C, nq, nq),
        in_specs=[pl.BlockSpec((1, tq, LANES), lambda b, h, qi, ki: (h, b * nq + qi, 0)),
                  pl.BlockSpec((1, tk, LANES), lambda b, h, qi, ki: (h, b * nq + jnp.minimum(ki, qi), 0)),
                  pl.BlockSpec((1, tk, LANES), lambda b, h, qi, ki: (h, b * nq + jnp.minimum(ki, qi), 0))],
        out_specs=pl.BlockSpec((1, tq, LANES), lambda b, h, qi, ki: (h, b * nq + qi, 0)),
        out_shape=jax.ShapeDtypeStruct((H_C, nbatch * seq, LANES), F32),
        scratch_shapes=[pltpu.VMEM((tq, 1), F32), pltpu.VMEM((tq, 1), F32), pltpu.VMEM((tq, LANES), F32)],
        compiler_params=_cparams(("parallel", "parallel", "parallel", "arbitrary")),
    )(q, k, v)


PAGES_PER_STEP = 16


def _joint_softmax_update(tiles, values, m_ref, l_ref, acc_ref):
    m_old = m_ref[...]
    m_new = m_old
    for s in tiles:
        m_new = jnp.maximum(m_new, jnp.max(s, axis=-1, keepdims=True))
    alpha = jnp.exp(m_old - m_new)
    l_new = alpha * l_ref[...]
    acc = alpha * acc_ref[...]
    for s, c in zip(tiles, values):
        p = jnp.exp(s - m_new)
        l_new = l_new + jnp.sum(p, axis=-1, keepdims=True)
        acc = acc + jnp.dot(p.astype(BF16), c, preferred_element_type=F32)
    l_ref[...] = l_new
    acc_ref[...] = acc
    m_ref[...] = m_new


def _paged_kernel(snew, npg, pt_ref, ql_ref, qp_ref, *rest):
    ckv_refs = rest[:npg]
    kr_refs = rest[npg:2 * npg]
    cn_ref, kn_ref, wv_ref, o_ref, m_ref, l_ref, acc_ref = rest[2 * npg:]
    pg = pl.program_id(1)
    rows = H_C * snew
    ql = ql_ref[...].reshape(rows, KV_LORA).astype(BF16)
    qp = qp_ref[...].reshape(rows, QK_ROPE).astype(BF16)
    page = ckv_refs[0].shape[2]

    @pl.when(pg == 0)
    def _():
        m_ref[...] = jnp.full(m_ref.shape, NEG, F32)
        l_ref[...] = jnp.zeros(l_ref.shape, F32)
        acc_ref[...] = jnp.zeros(acc_ref.shape, F32)

    def scores(c, r):
        return (lax.dot_general(ql, c, (((1,), (1,)), ((), ())), preferred_element_type=F32)
                + lax.dot_general(qp, r, (((1,), (1,)), ((), ())), preferred_element_type=F32))

    cs = [ckv_refs[j][0, 0].astype(BF16) for j in range(npg)]
    tiles = [scores(cs[j], kr_refs[j][0, 0].astype(BF16)) for j in range(npg)]
    _joint_softmax_update(tiles, cs, m_ref, l_ref, acc_ref)

    @pl.when(pg == pl.num_programs(1) - 1)
    def _():
        pad = lambda x: jnp.concatenate(
            [x, jnp.zeros((page - snew, x.shape[1]), F32)], axis=0).astype(BF16)
        c = pad(cn_ref[...])
        s = scores(c, pad(kn_ref[...]))
        tok = lax.broadcasted_iota(jnp.int32, s.shape, 0) % snew
        key = lax.broadcasted_iota(jnp.int32, s.shape, 1)
        s = jnp.where(key <= tok, s, NEG)
        _joint_softmax_update([s], [c], m_ref, l_ref, acc_ref)
        o_lat = acc_ref[...] / l_ref[...]
        for h in range(H_C):
            o_ref[:, h * V_HEAD:(h + 1) * V_HEAD] = jnp.dot(
                o_lat[h * snew:(h + 1) * snew].astype(BF16), wv_ref[h], preferred_element_type=F32)


def _paged_attention(ql, qp, cache_ckv, cache_kr, layer, ckv_new, kr_new, wv, page_table, snew):
    nb, n_pages = page_table.shape
    page = cache_ckv.shape[2]
    npg = min(PAGES_PER_STEP, n_pages)
    ckv_specs = [pl.BlockSpec((1, 1, page, KV_LORA),
                              lambda b, g, pt, j=j: (layer, pt[b, g * npg + j], 0, 0)) for j in range(npg)]
    kr_specs = [pl.BlockSpec((1, 1, page, QK_ROPE),
                             lambda b, g, pt, j=j: (layer, pt[b, g * npg + j], 0, 0)) for j in range(npg)]
    rows = H_C * snew
    grid_spec = pltpu.PrefetchScalarGridSpec(
        num_scalar_prefetch=1,
        grid=(nb, n_pages // npg),
        in_specs=[pl.BlockSpec((H_C, snew, KV_LORA), lambda b, g, pt: (0, b, 0)),
                  pl.BlockSpec((H_C, snew, QK_ROPE), lambda b, g, pt: (0, b, 0))]
                 + ckv_specs + kr_specs
                 + [pl.BlockSpec((snew, KV_LORA), lambda b, g, pt: (b, 0)),
                    pl.BlockSpec((snew, QK_ROPE), lambda b, g, pt: (b, 0)),
                    pl.BlockSpec((H_C, KV_LORA, V_HEAD), lambda b, g, pt: (0, 0, 0))],
        out_specs=pl.BlockSpec((snew, D_BR), lambda b, g, pt: (b, 0)),
        scratch_shapes=[pltpu.VMEM((rows, 1), F32), pltpu.VMEM((rows, 1), F32),
                        pltpu.VMEM((rows, KV_LORA), F32)],
    )
    return pl.pallas_call(
        functools.partial(_paged_kernel, snew, npg),
        grid_spec=grid_spec,
        out_shape=jax.ShapeDtypeStruct((nb * snew, D_BR), F32),
        compiler_params=_cparams(("parallel", "arbitrary")),
    )(page_table, ql, qp, *([cache_ckv] * npg), *([cache_kr] * npg), ckv_new, kr_new, wv)


def _merge_kernel(nb, tq, y_grouped, o_grouped, y_ref, vkr_ref, bonus_ref, x_ref, o_ref, ga_ref, cb_ref, cc_ref, cx_ref, gb_ref,
                  gc_ref, ma_ref, mb_ref, mc_ref, hc_ref, hx_ref, lnw_ref, lnb_ref, cw_ref, np_ref,
                  wbr_ref, wout_ref, xo_ref, zt_ref):
    hm = _head_ones()
    tm = nb * tq
    if y_grouped:
        y = jnp.concatenate([y_ref[g] for g in range(N_LANE_GROUPS)], axis=1)
    else:
        y = y_ref[...] + vkr_ref[...]
    mu = _head_sum(y, hm) * (1.0 / HEAD_A)
    yc = y - mu
    var = _head_sum(yc * yc, hm) * (1.0 / HEAD_A)
    out_a = (yc * lax.rsqrt(var + GN_EPS) * lnw_ref[...] + lnb_ref[...] + bonus_ref[...]) * _silu(ga_ref[...])
    z = cc_ref[...] * cx_ref[...]
    zh = hc_ref[...] * hx_ref[...]
    z1 = _shift_rows(z, zh[:, 1:2, :], 1, nb, tq)
    z2 = _shift_rows(z, zh, 2, nb, tq)
    cw = cw_ref[...]
    conv = cw[0:1] * z2 + cw[1:2] * z1 + cw[2:3] * z
    out_b = cb_ref[...] * conv * _silu(gb_ref[...])
    if nb == 1:
        zt_ref[0] = z[tm - SUBLANES:, :]
    else:
        zt_ref[...] = z.reshape(nb, tq, D_BR)
    if o_grouped:
        o = jnp.concatenate([o_ref[h] for h in range(H_C)], axis=1)
    else:
        o = o_ref[...]
    out_c = o * _silu(gc_ref[...])
    mixed = (_sigmoid(ma_ref[...]) * jnp.dot(out_a.astype(BF16), wbr_ref[0], preferred_element_type=F32)
             + _sigmoid(mb_ref[...]) * jnp.dot(out_b.astype(BF16), wbr_ref[1], preferred_element_type=F32)
             + _sigmoid(mc_ref[...]) * jnp.dot(out_c.astype(BF16), wbr_ref[2], preferred_element_type=F32))
    res = jnp.dot(mixed.astype(BF16), wout_ref[...], preferred_element_type=F32)
    xo_ref[...] = x_ref[...] + _rms(res, np_ref[...])


def _merge(y, vkr, bonus, x, o, proj, halo_c, halo_x, lw, nb, tq):
    t = x.shape[0]
    y_grouped, o_grouped = y.ndim == 3, o.ndim == 3
    tm = nb * tq
    nseg = t // tq
    row = pl.BlockSpec((tm, D_BR), lambda i: (i, 0))
    prow = lambda cb: pl.BlockSpec((tm, D_BR), lambda i, cb=cb: (i, cb))
    halo = pl.BlockSpec((nb, 2, D_BR), lambda i: (i, 0, 0))
    vec = pl.BlockSpec((1, D_BR), lambda i: (0, 0))
    grouped = pl.BlockSpec((N_LANE_GROUPS, tm, LANES), lambda i: (0, i, 0))
    zt_rows = min(tq, SUBLANES)
    return pl.pallas_call(
        functools.partial(_merge_kernel, nb, tq, y_grouped, o_grouped),
        grid=(t // tm,),
        in_specs=[grouped if y_grouped else row, row, row, row, grouped if o_grouped else row] + [prow(c) for c in range(3, 12)] + [halo, halo, vec, vec,
                  pl.BlockSpec((3, D_BR), lambda i: (0, 0)), vec,
                  pl.BlockSpec((3, D_BR, D_MODEL), lambda i: (0, 0, 0)),
                  pl.BlockSpec((D_MODEL, D_MODEL), lambda i: (0, 0))],
        out_specs=[row, pl.BlockSpec((nb, zt_rows, D_BR), lambda i: (i, 0, 0))],
        out_shape=[jax.ShapeDtypeStruct((t, D_MODEL), F32),
                   jax.ShapeDtypeStruct((nseg, zt_rows, D_BR), F32)],
        compiler_params=_cparams(("parallel",)),
    )(y, vkr, bonus, x, o, *([proj] * 9), halo_c, halo_x, lw["ln_x_w"], lw["ln_x_b"], lw["conv_w"],
      lw["norm_post"], lw["w_branch"], lw["w_out"])


def _layer_weights(l, p):
    w_in = p["w_in"][l]
    z = lambda n: jnp.zeros((D_MODEL, n), F32)
    o_ga = 3 * D_BR + DECAY_LORA + ICLR_LORA
    o_qa = o_ga + 5 * D_BR
    o_kpe = o_qa + Q_LORA + KV_LORA
    o_gc = o_kpe + QK_ROPE
    w_re = jnp.concatenate([
        w_in[:, :3 * D_BR], w_in[:, o_ga:o_qa], w_in[:, o_gc:o_gc + 4 * D_BR],
        w_in[:, o_qa:o_kpe], w_in[:, 3 * D_BR:o_ga],
        z(ROPE_LANE0), w_in[:, o_kpe:o_gc], z(LANES - ROPE_LANE0 - QK_ROPE),
        z(N_PROJ - COL_KPE - LANES)], axis=1).astype(BF16)
    mu = p["mu_shift"][l]
    r2 = lambda a: a.reshape(1, -1)
    w_uq = p["w_uq"][l]
    wq = jnp.concatenate([w_uq, jnp.zeros((Q_LORA, H_C, LANES - QK_NOPE - QK_ROPE), F32)], axis=2)
    w_uk = p["w_uk"][l]
    wk = jnp.concatenate([w_uk, jnp.zeros((KV_LORA, H_C, LANES - QK_NOPE), F32)], axis=2)
    wuk_abs = jnp.concatenate([jnp.transpose(w_uk, (1, 2, 0)),
                               jnp.zeros((H_C, LANES - QK_NOPE, KV_LORA), F32)], axis=1)
    sel = np.zeros((LANES, QK_ROPE), np.float32)
    sel[ROPE_LANE0 + np.arange(QK_ROPE), np.arange(QK_ROPE)] = 1.0
    zl = jnp.zeros((DECAY_LORA, D_BR), F32)
    return {
        "w_in": w_re, "norm_pre": r2(p["norm_pre"][l]), "norm_post": r2(p["norm_post"][l]),
        "mu_r": r2(mu[:D_BR]), "mu_k": r2(mu[D_BR:2 * D_BR]), "mu_v": r2(mu[2 * D_BR:3 * D_BR]),
        "mu_lora": r2(mu[3 * D_BR:]),
        "w0": r2(p["w0"][l]), "a0": r2(p["a0"][l]),
        "w2p": jnp.concatenate([p["w2"][l], zl], axis=0).astype(BF16),
        "a2p": jnp.concatenate([zl, p["a2"][l]], axis=0).astype(BF16),
        "k_k": r2(p["k_k"][l]), "k_a": r2(p["k_a"][l]), "r_k": r2(p["r_k"][l]),
        "ln_x_w": r2(p["ln_x_w"][l]), "ln_x_b": r2(p["ln_x_b"][l]), "conv_w": p["conv_w"][l],
        "q_a_norm": r2(p["q_a_norm"][l]), "kv_a_norm": r2(p["kv_a_norm"][l]),
        "wq": wq.reshape(Q_LORA, H_C * LANES).astype(BF16),
        "wk": wk.reshape(KV_LORA, H_C * LANES).astype(BF16),
        "wv": p["w_uv"][l].reshape(KV_LORA, H_C * V_HEAD).astype(BF16),
        "wuk_abs": wuk_abs.astype(BF16),
        "wv_heads": jnp.transpose(p["w_uv"][l], (1, 0, 2)).astype(BF16),
        "sel_rope": jnp.asarray(sel, BF16),
        "w_branch": p["w_branch"][l].astype(BF16), "w_out": p["w_out"][l].astype(BF16),
    }


def _rope_tables(pos):
    half = QK_ROPE // 2
    freqs = ROPE_THETA ** (-jnp.arange(half, dtype=F32) / half)
    ang = pos.astype(F32)[:, None] * freqs[None, :]
    cos, sin = jnp.cos(ang), jnp.sin(ang)
    n = pos.shape[0]
    lo = jnp.zeros((n, ROPE_LANE0), F32)
    hi = jnp.zeros((n, LANES - ROPE_LANE0 - QK_ROPE), F32)
    zh = jnp.zeros((n, half), F32)
    ck = jnp.concatenate([lo, cos, cos, hi], axis=1)
    cq = jnp.concatenate([lo + 1.0, cos, cos, hi], axis=1)
    s1 = jnp.concatenate([lo, zh, sin, hi], axis=1)
    s2 = jnp.concatenate([lo, -sin, zh, hi], axis=1)
    return cq, ck, s1, s2


def _prev_row_halos(proj, first_rows, nseq, tseq, tq):
    nseg_per = tseq // tq
    out = {}
    for name, (c0, c1) in {"r": (COL_R, COL_R + D_BR), "k": (COL_K, COL_K + D_BR),
                           "v": (COL_V, COL_V + D_BR), "lora": (COL_LORA, COL_LORA + LANES)}.items():
        first = first_rows[name].reshape(nseq, 1, c1 - c0)
        if nseg_per > 1:
            p3 = proj.reshape(nseq, tseq, N_PROJ)
            inner = p3[:, tq - 1:tseq - 1:tq, c0:c1]
            first = jnp.concatenate([first, inner], axis=1)
        out[name] = first.reshape(nseq * nseg_per, 1, c1 - c0)
    return out


def _conv_halos(proj, conv_prev, nseq, tseq, tq):
    nseg_per = tseq // tq
    hc = conv_prev
    hx = jnp.ones_like(conv_prev)
    if nseg_per > 1:
        p3 = proj.reshape(nseq, nseg_per, tq, N_PROJ)[:, :-1, tq - 2:, :]
        hc = jnp.concatenate([hc[:, None], p3[..., COL_CC:COL_CC + D_BR]], axis=1)
        hx = jnp.concatenate([hx[:, None], p3[..., COL_CX:COL_CX + D_BR]], axis=1)
    return hc.reshape(nseq * nseg_per, 2, D_BR), hx.reshape(nseq * nseg_per, 2, D_BR)


def _layer(x, lw, tabs, nseq, tseq, shift_prev, conv_prev, wkv_prev, attn):
    t = nseq * tseq
    proj = _inproj(x, lw["norm_pre"], lw["w_in"])
    if tseq >= 128:
        nb, tq = 1, 128
    else:
        nb, tq = min(nseq, 128 // tseq), tseq
    first = {"r": shift_prev[:, :D_BR], "k": shift_prev[:, D_BR:2 * D_BR],
             "v": shift_prev[:, 2 * D_BR:3 * D_BR], "lora": shift_prev[:, 3 * D_BR:]}
    chunked = tseq % WKV_CHUNK == 0
    ops = _rwkv_prep(proj, _prev_row_halos(proj, first, nseq, tseq, tq), lw, nb, tq, chunked)
    if chunked:
        bonus = ops[6]
        vkr = bonus
        w6 = wkv_prev.reshape(nseq, N_LANE_GROUPS, 2, HEAD_A, HEAD_A)
        s0 = jnp.zeros((nseq, N_LANE_GROUPS, 2, HEAD_A, 2, HEAD_A), F32)
        s0 = s0.at[:, :, 0, :, 0, :].set(w6[:, :, 0]).at[:, :, 1, :, 1, :].set(w6[:, :, 1])
        y, st = _wkv_chunked(ops[:6], s0.reshape(nseq, N_LANE_GROUPS, LANES, LANES), nseq, tseq)
        st6 = st.reshape(nseq, N_LANE_GROUPS, 2, HEAD_A, 2, HEAD_A)
        wkv_new = jnp.stack([st6[:, :, 0, :, 0, :], st6[:, :, 1, :, 1, :]], axis=2).reshape(
            nseq, H_A, HEAD_A, HEAD_A)
    else:
        vkr, bonus = ops[7], ops[8]
        s0 = jnp.transpose(wkv_prev, (0, 2, 1, 3)).reshape(nseq, HEAD_A, D_BR)
        y, st = _wkv_scan(ops[:7], s0, nseq, tseq)
        wkv_new = jnp.transpose(st.reshape(nseq, HEAD_A, H_A, HEAD_A), (0, 2, 1, 3))
    o, ckv, kr128 = attn(proj, lw, tabs)
    hc, hx = _conv_halos(proj, conv_prev, nseq, tseq, tq)
    x_new, zt = _merge(y, vkr, bonus, x, o, proj, hc, hx, lw, nb, tq)
    p3 = proj.reshape(nseq, tseq, N_PROJ)[:, -1, :]
    new_shift = jnp.concatenate([p3[:, :3 * D_BR], p3[:, COL_LORA:COL_LORA + LANES]], axis=1)
    new_conv = zt.reshape(nseq, tseq // tq, -1, D_BR)[:, -1, -2:, :]
    kr = kr128[:, ROPE_LANE0:ROPE_LANE0 + QK_ROPE]
    return x_new, (ckv.reshape(nseq, tseq, KV_LORA), kr.reshape(nseq, tseq, QK_ROPE), wkv_new,
                   new_shift, new_conv)


def kernel(x_prompt, x_sample, cache_ckv, cache_krope, state_wkv, state_shift, state_conv, page_table,
           norm_pre, norm_post, w_in, mu_shift, w0, w2, a0, a2, k_k, k_a, r_k, ln_x_w, ln_x_b,
           conv_w, q_a_norm, w_uq, kv_a_norm, w_uk, w_uv, w_branch, w_out):
    params = dict(norm_pre=norm_pre, norm_post=norm_post, w_in=w_in, mu_shift=mu_shift, w0=w0, w2=w2,
                  a0=a0, a2=a2, k_k=k_k, k_a=k_a, r_k=r_k.reshape(r_k.shape[0], -1), ln_x_w=ln_x_w,
                  ln_x_b=ln_x_b, conv_w=conv_w, q_a_norm=q_a_norm, w_uq=w_uq, kv_a_norm=kv_a_norm,
                  w_uk=w_uk, w_uv=w_uv, w_branch=w_branch, w_out=w_out)
    bp, sp, _ = x_prompt.shape
    bd, sd, _ = x_sample.shape
    depth = w_in.shape[0]
    past_len = page_table.shape[1] * cache_ckv.shape[2]
    tabs_p = _rope_tables(jnp.arange(sp))
    tm_s = min(bd * sd, 256)
    tabs_s = _rope_tables(past_len + (jnp.arange(tm_s) % sd))

    xp = x_prompt.reshape(bp * sp, D_MODEL)
    xs = x_sample.reshape(bd * sd, D_MODEL)
    acc_p, acc_s = [], []
    for l in range(depth):
        lw = _layer_weights(l, params)

        def attn_prompt(proj, lw, tabs):
            q, k, v, ckv, kr128 = _mla_prep_prompt(proj, lw, tabs, sp)
            return _flash_prompt(q, k, v, bp, sp), ckv, kr128

        def attn_sample(proj, lw, tabs, l=l):
            ql, qp, ckv, kr128 = _mla_prep_sample(proj, lw, tabs)
            kr_new = kr128[:, ROPE_LANE0:ROPE_LANE0 + QK_ROPE]
            o = _paged_attention(ql, qp, cache_ckv, cache_krope, l, ckv, kr_new, lw["wv_heads"],
                                 page_table, sd)
            return o, ckv, kr128

        xp, st_p = _layer(xp, lw, tabs_p, bp, sp, jnp.zeros((bp, state_shift.shape[2]), F32),
                          jnp.zeros((bp, 2, D_BR), F32), jnp.zeros((bp, H_A, HEAD_A, HEAD_A), F32),
                          attn_prompt)
        xs, st_s = _layer(xs, lw, tabs_s, bd, sd, state_shift[l], state_conv[l], state_wkv[l],
                          attn_sample)
        acc_p.append(st_p)
        acc_s.append(st_s)
    outs_p = [jnp.stack([a[i] for a in acc_p], axis=0) for i in range(5)]
    outs_s = [jnp.stack([a[i] for a in acc_s], axis=0) for i in range(5)]
    return (xp.reshape(bp, sp, D_MODEL), xs.reshape(bd, sd, D_MODEL), *outs_p, *outs_s)
```

```python
import functools
import math

import numpy as np
import jax
import jax.numpy as jnp
from jax import lax
from jax.experimental import pallas as pl
from jax.experimental.pallas import tpu as pltpu

F32 = jnp.float32
BF16 = jnp.bfloat16

D_MODEL = 1024
D_BR = D_MODEL
HEAD_A = 64
H_A = D_BR // HEAD_A
DECAY_LORA = 64
ICLR_LORA = 64
GN_EPS = 64e-5
H_C = 8
QK_NOPE = 64
QK_ROPE = 32
V_HEAD = D_BR // H_C
Q_LORA = D_MODEL // 4
KV_LORA = D_MODEL // 4
ROPE_THETA = 10000.0
NORM_EPS = 1e-6
LANES = 128
SUBLANES = 8
N_LANE_GROUPS = D_BR // LANES
ROPE_LANE0 = 64
VMEM_LIMIT = 56 * 1024 * 1024

COL_R, COL_K, COL_V, COL_GA, COL_CB, COL_CC, COL_CX, COL_GB, COL_GC, COL_MA, COL_MB, COL_MC = (
    i * D_BR for i in range(12))
COL_QA = 12 * D_BR
COL_KVA = COL_QA + Q_LORA
COL_LORA = COL_KVA + KV_LORA
COL_KPE = COL_LORA + LANES
N_PROJ = 13 * D_BR


def _cparams(sem):
    return pltpu.CompilerParams(dimension_semantics=sem, vmem_limit_bytes=VMEM_LIMIT)


def _head_ones():
    r = lax.broadcasted_iota(jnp.int32, (LANES, LANES), 0) // HEAD_A
    c = lax.broadcasted_iota(jnp.int32, (LANES, LANES), 1) // HEAD_A
    return (r == c).astype(BF16)


def _split_dot(x, w):
    hi = x.astype(BF16)
    lo = (x - hi.astype(F32)).astype(BF16)
    return (jnp.dot(hi, w, preferred_element_type=F32) + jnp.dot(lo, w, preferred_element_type=F32))


def _split_dot_rhs(w, x):
    hi = x.astype(BF16)
    lo = (x - hi.astype(F32)).astype(BF16)
    return (jnp.dot(w, hi, preferred_element_type=F32) + jnp.dot(w, lo, preferred_element_type=F32))


def _head_sum(x, hm):
    parts = [_split_dot(x[:, g * LANES:(g + 1) * LANES], hm) for g in range(x.shape[1] // LANES)]
    return jnp.concatenate(parts, axis=1)


def _shift_rows(p, halo, k, nb, tq):
    c = p.shape[1]
    prev = pltpu.roll(p, k, axis=0)
    pos = lax.broadcasted_iota(jnp.int32, p.shape, 0) % tq
    out = prev
    for j in range(k):
        hb = jnp.broadcast_to(halo[:, j:j + 1, :], (nb, tq, c)).reshape(nb * tq, c)
        out = jnp.where(pos == j, hb, out)
    return out


def _sigmoid(x):
    return 1.0 / (1.0 + jnp.exp(-x))


def _silu(x):
    return x * _sigmoid(x)


def _rms(x, g):
    return x * lax.rsqrt(jnp.mean(x * x, axis=-1, keepdims=True) + NORM_EPS) * g


def _inproj_kernel(x_ref, g_ref, w_ref, o_ref, h_ref):
    @pl.when(pl.program_id(1) == 0)
    def _():
        h_ref[...] = _rms(x_ref[...], g_ref[...]).astype(BF16)

    o_ref[...] = jnp.dot(h_ref[...], w_ref[...], preferred_element_type=F32)


def _inproj(x, g, w):
    t = x.shape[0]
    tm = min(t, 1024)
    tn = 1024
    return pl.pallas_call(
        _inproj_kernel,
        grid=(t // tm, N_PROJ // tn),
        in_specs=[pl.BlockSpec((tm, D_MODEL), lambda i, j: (i, 0)),
                  pl.BlockSpec((1, D_MODEL), lambda i, j: (0, 0)),
                  pl.BlockSpec((D_MODEL, tn), lambda i, j: (0, j))],
        out_specs=pl.BlockSpec((tm, tn), lambda i, j: (i, j)),
        out_shape=jax.ShapeDtypeStruct((t, N_PROJ), F32),
        scratch_shapes=[pltpu.VMEM((tm, D_MODEL), BF16)],
        compiler_params=_cparams(("parallel", "arbitrary")),
    )(x, g, w)


def _rwkv_prep_kernel(nb, tq, chunked, r_ref, k_ref, v_ref, lo_ref, hr_ref, hk_ref, hv_ref, hl_ref,
                      mur_ref, muk_ref, muv_ref, mul_ref, w0_ref, w2_ref, a0_ref, a2_ref,
                      kk_ref, ka_ref, rk_ref, *outs):
    hm = _head_ones()

    def mix(ref, halo_ref, mu_ref):
        p = ref[...]
        prev = _shift_rows(p, halo_ref[...], 1, nb, tq)
        return p + (prev - p) * mu_ref[...]

    r = mix(r_ref, hr_ref, mur_ref)
    k = mix(k_ref, hk_ref, muk_ref)
    v = mix(v_ref, hv_ref, muv_ref)
    lo = mix(lo_ref, hl_ref, mul_ref)
    wf = w0_ref[...] + jnp.dot(jnp.tanh(lo).astype(BF16), w2_ref[...], preferred_element_type=F32)
    log_decay = -math.exp(-0.5) * _sigmoid(wf)
    a = _sigmoid(a0_ref[...] + jnp.dot(lo.astype(BF16), a2_ref[...], preferred_element_type=F32))
    kk = k * kk_ref[...]
    kk = kk / jnp.maximum(jnp.sqrt(_head_sum(kk * kk, hm)), 1e-12)
    kmod = k * (1.0 + (a - 1.0) * ka_ref[...])
    b = kk * a
    bonus = _head_sum(r * kmod * rk_ref[...], hm) * v
    if chunked:
        *group_outs, bonus_o = outs
        bonus_o[...] = bonus
        for o_ref, val in zip(group_outs, (r, log_decay, kmod, v, -kk, b)):
            for g in range(N_LANE_GROUPS):
                o_ref[g] = val[:, g * LANES:(g + 1) * LANES]
        return
    wr_o, a_o, w_o, b_o, k_o, v_o, br_o, vkr_o, bonus_o = outs
    decay = jnp.exp(log_decay)
    wr_o[...] = decay * r
    a_o[...] = -kk
    w_o[...] = decay
    b_o[...] = b
    k_o[...] = kmod
    v_o[...] = v
    br_o[...] = _head_sum(b * r, hm)
    vkr_o[...] = _head_sum(kmod * r, hm) * v
    bonus_o[...] = bonus


def _rwkv_prep(proj, halos, lw, nb, tq, chunked):
    t = proj.shape[0]
    tm = nb * tq
    row_out = pl.BlockSpec((tm, D_BR), lambda i: (i, 0))
    if chunked:
        out_specs = [pl.BlockSpec((N_LANE_GROUPS, tm, LANES), lambda i: (0, i, 0))] * 6 + [row_out]
        outs = [jax.ShapeDtypeStruct((N_LANE_GROUPS, t, LANES), F32)] * 6 + [
            jax.ShapeDtypeStruct((t, D_BR), F32)]
    else:
        out_specs = [row_out] * 9
        outs = [jax.ShapeDtypeStruct((t, D_BR), F32)] * 9
    row = lambda cb: pl.BlockSpec((tm, D_BR), lambda i, cb=cb: (i, cb))
    halo = lambda c: pl.BlockSpec((nb, 1, c), lambda i: (i, 0, 0))
    vec = lambda c: pl.BlockSpec((1, c), lambda i: (0, 0))
    mat = pl.BlockSpec((LANES, D_BR), lambda i: (0, 0))
    return pl.pallas_call(
        functools.partial(_rwkv_prep_kernel, nb, tq, chunked),
        grid=(t // tm,),
        in_specs=[row(0), row(1), row(2),
                  pl.BlockSpec((tm, LANES), lambda i: (i, COL_LORA // LANES)),
                  halo(D_BR), halo(D_BR), halo(D_BR), halo(LANES),
                  vec(D_BR), vec(D_BR), vec(D_BR), vec(LANES),
                  vec(D_BR), mat, vec(D_BR), mat, vec(D_BR), vec(D_BR), vec(D_BR)],
        out_specs=out_specs,
        out_shape=outs,
        compiler_params=_cparams(("parallel",)),
    )(proj, proj, proj, proj, halos["r"], halos["k"], halos["v"], halos["lora"],
      lw["mu_r"], lw["mu_k"], lw["mu_v"], lw["mu_lora"], lw["w0"], lw["w2p"], lw["a0"], lw["a2p"],
      lw["k_k"], lw["k_a"], lw["r_k"])


def _wkv_kernel(ns, tb, wr_ref, a_ref, w_ref, b_ref, k_ref, v_ref, br_ref, s0_ref,
                y_ref, st_ref, p_ref):
    @pl.when(pl.program_id(1) == 0)
    def _():
        st_ref[...] = s0_ref[...]

    hm = _head_ones()
    ri = lax.broadcasted_iota(jnp.int32, (HEAD_A, LANES), 0)
    ci = lax.broadcasted_iota(jnp.int32, (HEAD_A, LANES), 1)
    diag = ((ci % HEAD_A) == ri).astype(F32)

    er = lax.broadcasted_iota(jnp.int32, (SUBLANES, HEAD_A), 0)
    ec = lax.broadcasted_iota(jnp.int32, (SUBLANES, HEAD_A), 1)
    tok_sum = ((ec // SUBLANES) == er).astype(BF16)

    def step(u, carry, base):
        t = base + u
        for s in range(ns):
            row = lambda ref: ref[s, pl.ds(t, 1), :]
            wr, a, w, b, k, v, br = (row(x) for x in (wr_ref, a_ref, w_ref, b_ref, k_ref, v_ref, br_ref))
            for g in range(N_LANE_GROUPS):
                sl = slice(g * LANES, (g + 1) * LANES)
                st = st_ref[s, :, sl]
                m1 = st * a[:, sl]
                m1_hi = m1.astype(BF16)
                m1_lo = (m1 - m1_hi.astype(F32)).astype(BF16)
                m2 = (st * wr[:, sl]).astype(BF16)
                m3 = (diag * v[:, sl]).astype(BF16)
                res = jnp.dot(jnp.concatenate([m1_hi, m1_lo, m2, m3], axis=0), hm,
                              preferred_element_type=F32)
                sa = res[0:HEAD_A] + res[HEAD_A:2 * HEAD_A]
                y2 = res[2 * HEAD_A:3 * HEAD_A]
                vc = res[3 * HEAD_A:4 * HEAD_A]
                st_ref[s, :, sl] = st * w[:, sl] + sa * b[:, sl] + vc * k[:, sl]
                yd = (y2 + sa * br[:, sl]) * diag
                p_ref[s, u, :, sl] = jnp.sum(yd.reshape(HEAD_A // SUBLANES, SUBLANES, LANES), axis=0)
        return carry

    def block8(t8, carry):
        base = pl.multiple_of(t8 * SUBLANES, SUBLANES)
        lax.fori_loop(0, SUBLANES, functools.partial(step, base=base), 0)
        for s in range(ns):
            y_ref[s, pl.ds(base, SUBLANES), :] = _split_dot_rhs(
                tok_sum, p_ref[s].reshape(SUBLANES * SUBLANES, D_BR))
        return carry

    lax.fori_loop(0, tb // SUBLANES, block8, 0)


def _wkv_scan(ops, s0, nseq, tseq):
    ns = 2
    tb = min(tseq, 128)
    ops3 = [o.reshape(nseq, tseq, D_BR) for o in ops]
    blk = pl.BlockSpec((ns, tb, D_BR), lambda i, j: (i, j, 0))
    sblk = pl.BlockSpec((ns, HEAD_A, D_BR), lambda i, j: (i, 0, 0))
    y, st = pl.pallas_call(
        functools.partial(_wkv_kernel, ns, tb),
        grid=(nseq // ns, tseq // tb),
        in_specs=[blk] * 7 + [sblk],
        out_specs=[blk, sblk],
        out_shape=[jax.ShapeDtypeStruct((nseq, tseq, D_BR), F32),
                   jax.ShapeDtypeStruct((nseq, HEAD_A, D_BR), F32)],
        scratch_shapes=[pltpu.VMEM((ns, SUBLANES, SUBLANES, D_BR), F32)],
        compiler_params=_cparams(("parallel", "arbitrary")),
    )(*ops3, s0)
    return y.reshape(nseq * tseq, D_BR), st


WKV_CHUNK = 64
WKV_PASSES_LOCAL = 1
WKV_PASSES_STATE = 3


def _dot_nn(a, b):
    return jnp.dot(a, b, preferred_element_type=F32)


def _dot_nt(a, b):
    return lax.dot_general(a, b, (((1,), (1,)), ((), ())), preferred_element_type=F32)


def _dot_tn(a, b):
    return lax.dot_general(a, b, (((0,), (0,)), ((), ())), preferred_element_type=F32)


def _hilo(x):
    hi = x.astype(BF16)
    return hi, (x - hi.astype(F32)).astype(BF16)


def _mm(dot, a, b, passes):
    ah, al = _hilo(a)
    bh, bl = _hilo(b)
    out = dot(ah, bh)
    if passes >= 3:
        out = out + dot(al, bh) + dot(ah, bl)
    return out


def _wkv_chunk_kernel(ns, r_ref, lw_ref, k_ref, v_ref, a_ref, b_ref, s0_ref, y_ref, st_ref):
    L = r_ref.shape[2]

    @pl.when(pl.program_id(1) == 0)
    def _():
        st_ref[...] = s0_ref[...]

    ti = lax.broadcasted_iota(jnp.int32, (L, L), 0)
    si = lax.broadcasted_iota(jnp.int32, (L, L), 1)
    incl = si <= ti
    strict = si < ti
    eye_l = (si == ti).astype(F32)
    tril_ones = incl.astype(BF16)
    levels = []
    m = 1
    while m < L:
        levels.append(((ti // (2 * m)) == (si // (2 * m))) & ((ti % (2 * m)) >= m) & ((si % (2 * m)) < m))
        m *= 2
    lane = lax.broadcasted_iota(jnp.int32, (1, LANES), 1)
    head_masks = [(lane < HEAD_A).astype(F32), (lane >= HEAD_A).astype(F32)]
    gr = lax.broadcasted_iota(jnp.int32, (LANES, LANES), 0)
    gc = lax.broadcasted_iota(jnp.int32, (LANES, LANES), 1)
    same_head = (gr // HEAD_A) == (gc // HEAD_A)
    eye_g = gr == gc

    def each(fn, *lists):
        return [fn(*xs) for xs in zip(*lists)]

    def mm(dot, xs, ys, passes=WKV_PASSES_LOCAL):
        return each(lambda x, y: _mm(dot, x, y, passes), xs, ys)

    chains = [(s, g) for s in range(ns) for g in range(N_LANE_GROUPS)]
    r, lw, k, v, a, b = ([ref[g, s] for s, g in chains]
                         for ref in (r_ref, lw_ref, k_ref, v_ref, a_ref, b_ref))
    st = [st_ref[s, g] for s, g in chains]

    def cumsum(x):
        hi, lo = _hilo(x)
        return _dot_nn(tril_ones, hi) + _dot_nn(tril_ones, lo)

    c = each(cumsum, lw)
    c_last = [x[L - 1:L, :] for x in c]
    e_neg = [jnp.exp(-x) for x in c]
    e_end = each(lambda cl, x: jnp.exp(cl - x), c_last, c)
    at = each(lambda a_, x, w_: a_ * jnp.exp(x - w_), a, c, lw)
    rt = each(lambda r_, x: r_ * jnp.exp(x), r, c)
    bt = each(jnp.multiply, b, e_neg)
    kt = each(jnp.multiply, k, e_neg)
    bend = each(jnp.multiply, b, e_end)
    kend = each(jnp.multiply, k, e_end)

    def per_head(xs):
        return [x * hm_ for x in xs for hm_ in head_masks]

    def per_group(xs):
        return [x for x in xs for _ in head_masks]

    at_h, rt_h, v_h = per_head(at), per_head(rt), per_head(v)
    q2 = each(lambda x, y: jnp.concatenate([x, y], axis=0), at_h, rt_h)
    gb = mm(_dot_nt, q2, per_group(bt))
    gk = mm(_dot_nt, q2, per_group(kt))
    n = [jnp.where(strict, x[:L], 0.0) for x in gb]
    pb = [jnp.where(incl, x[L:], 0.0) for x in gb]
    m_ = [jnp.where(strict, x[:L], 0.0) for x in gk]
    pk = [jnp.where(incl, x[L:], 0.0) for x in gk]
    t_inv = [eye_l + jnp.where(levels[0], x, 0.0) for x in n]
    for lev in levels[1:]:
        cm = [jnp.where(lev, x, 0.0) for x in n]
        t_inv = each(jnp.add, t_inv, mm(_dot_nn, mm(_dot_nn, t_inv, cm), t_inv))
    abar_h = mm(_dot_nn, t_inv, at_h)
    uv_h = mm(_dot_nn, mm(_dot_nn, t_inv, m_), v_h)
    rbar_h = mm(_dot_nn, pb, abar_h)
    yv_h = each(jnp.add, mm(_dot_nn, pb, uv_h), mm(_dot_nn, pk, v_h))

    def head_total(xs):
        return [xs[2 * i] + xs[2 * i + 1] for i in range(len(chains))]

    abar, uv, yv = head_total(abar_h), head_total(uv_h), head_total(yv_h)
    rbar = each(jnp.add, rt, head_total(rbar_h))
    phi = each(lambda x, cl: jnp.where(same_head, x, 0.0) + jnp.where(eye_g, jnp.exp(cl), 0.0),
               mm(_dot_tn, abar, bend), c_last)
    psi = each(lambda x, y: jnp.where(same_head, x + y, 0.0), mm(_dot_tn, uv, bend), mm(_dot_tn, v, kend))
    y = each(jnp.add, mm(_dot_nt, rbar, st, WKV_PASSES_STATE), yv)
    st_new = each(jnp.add, mm(_dot_nn, st, phi, WKV_PASSES_STATE), psi)
    for (s, g), y_, st_ in zip(chains, y, st_new):
        y_ref[g, s] = y_
        st_ref[s, g] = st_


def _wkv_chunked(ops, s0, nseq, tseq):
    ns = 2
    L = WKV_CHUNK
    ops4 = [o.reshape(N_LANE_GROUPS, nseq, tseq, LANES) for o in ops]
    blk = pl.BlockSpec((N_LANE_GROUPS, ns, L, LANES), lambda i, j: (0, i, j, 0))
    sblk = pl.BlockSpec((ns, N_LANE_GROUPS, LANES, LANES), lambda i, j: (i, 0, 0, 0))
    y, st = pl.pallas_call(
        functools.partial(_wkv_chunk_kernel, ns),
        grid=(nseq // ns, tseq // L),
        in_specs=[blk] * 6 + [sblk],
        out_specs=[blk, sblk],
        out_shape=[jax.ShapeDtypeStruct((N_LANE_GROUPS, nseq, tseq, LANES), F32),
                   jax.ShapeDtypeStruct((nseq, N_LANE_GROUPS, LANES, LANES), F32)],
        compiler_params=_cparams(("parallel", "arbitrary")),
    )(*ops4, s0)
    return y.reshape(N_LANE_GROUPS, nseq * tseq, LANES), st


def _rope(x, c_ref, s1_ref, s2_ref):
    parts = []
    for h in range(x.shape[1] // LANES):
        xh = x[:, h * LANES:(h + 1) * LANES]
        half = QK_ROPE // 2
        parts.append(xh * c_ref[...] + pltpu.roll(xh, half, axis=1) * s1_ref[...]
                     + pltpu.roll(xh, LANES - half, axis=1) * s2_ref[...])
    return parts[0] if len(parts) == 1 else jnp.concatenate(parts, axis=1)


SOFTMAX_SCALE = (QK_NOPE + QK_ROPE) ** -0.5
LOG2E = math.log2(math.e)


def _mla_common(qa_ref, kva_ref, kpe_ref, qn_ref, kvn_ref, wq_ref, cq_ref, ck_ref, s1_ref, s2_ref,
                scale=SOFTMAX_SCALE):
    qn = _rms(qa_ref[...], qn_ref[...]).astype(BF16)
    q = jnp.dot(qn, wq_ref[...], preferred_element_type=F32)
    q = _rope(q, cq_ref, s1_ref, s2_ref) * scale
    ckv = _rms(kva_ref[...], kvn_ref[...])
    kr = _rope(kpe_ref[...], ck_ref, s1_ref, s2_ref)
    return q, ckv, kr


def _mla_prep_prompt_kernel(qa_ref, kva_ref, kpe_ref, qn_ref, kvn_ref, wq_ref, wk_ref, wv_ref,
                            cq_ref, ck_ref, s1_ref, s2_ref, q_o, k_o, v_o, ckv_o, kr_o):
    q, ckv, kr = _mla_common(qa_ref, kva_ref, kpe_ref, qn_ref, kvn_ref, wq_ref, cq_ref, ck_ref,
                             s1_ref, s2_ref, scale=SOFTMAX_SCALE * LOG2E)
    ckv_o[...] = ckv
    kr_o[...] = kr
    cb = ckv.astype(BF16)
    kn = jnp.dot(cb, wk_ref[...], preferred_element_type=F32)
    vv = jnp.dot(cb, wv_ref[...], preferred_element_type=F32)
    for h in range(H_C):
        sl = slice(h * LANES, (h + 1) * LANES)
        q_o[h] = q[:, sl].astype(BF16)
        k_o[h] = (kn[:, sl] + kr).T.astype(BF16)
        v_o[h] = vv[:, sl].astype(BF16)


def _small_specs(tm, npos_blocks):
    return [pl.BlockSpec((tm, Q_LORA), lambda i: (i, COL_QA // Q_LORA)),
            pl.BlockSpec((tm, KV_LORA), lambda i: (i, COL_KVA // KV_LORA)),
            pl.BlockSpec((tm, LANES), lambda i: (i, COL_KPE // LANES)),
            pl.BlockSpec((1, Q_LORA), lambda i: (0, 0)),
            pl.BlockSpec((1, KV_LORA), lambda i: (0, 0)),
            pl.BlockSpec((Q_LORA, H_C * LANES), lambda i: (0, 0))], \
           [pl.BlockSpec((tm, LANES), lambda i: (i % npos_blocks, 0))] * 4


def _mla_prep_prompt(proj, lw, tabs, seq):
    t = proj.shape[0]
    tm = min(seq, 512)
    head, tab = _small_specs(tm, seq // tm)
    wfull = pl.BlockSpec((KV_LORA, H_C * LANES), lambda i: (0, 0))
    row = lambda c: pl.BlockSpec((tm, c), lambda i: (i, 0))
    return pl.pallas_call(
        _mla_prep_prompt_kernel,
        grid=(t // tm,),
        in_specs=head + [wfull, wfull] + tab,
        out_specs=[pl.BlockSpec((H_C, tm, LANES), lambda i: (0, i, 0)),
                   pl.BlockSpec((H_C, LANES, tm), lambda i: (0, 0, i)),
                   pl.BlockSpec((H_C, tm, LANES), lambda i: (0, i, 0)), row(KV_LORA), row(LANES)],
        out_shape=[jax.ShapeDtypeStruct((H_C, t, LANES), BF16), jax.ShapeDtypeStruct((H_C, LANES, t), BF16),
                   jax.ShapeDtypeStruct((H_C, t, LANES), BF16)]
                  + [jax.ShapeDtypeStruct((t, KV_LORA), F32), jax.ShapeDtypeStruct((t, LANES), F32)],
        compiler_params=_cparams(("parallel",)),
    )(proj, proj, proj, lw["q_a_norm"], lw["kv_a_norm"], lw["wq"], lw["wk"], lw["wv"], *tabs)


def _mla_prep_sample_kernel(qa_ref, kva_ref, kpe_ref, qn_ref, kvn_ref, wq_ref, wuk_ref, sel_ref,
                            cq_ref, ck_ref, s1_ref, s2_ref, ql_o, qp_o, ckv_o, kr_o):
    q, ckv, kr = _mla_common(qa_ref, kva_ref, kpe_ref, qn_ref, kvn_ref, wq_ref, cq_ref, ck_ref,
                             s1_ref, s2_ref)
    ckv_o[...] = ckv
    kr_o[...] = kr
    qb = q.astype(BF16)
    for h in range(H_C):
        qh = qb[:, h * LANES:(h + 1) * LANES]
        ql_o[h] = jnp.dot(qh, wuk_ref[h], preferred_element_type=F32)
        qp_o[h] = jnp.dot(qh, sel_ref[...], preferred_element_type=F32)


def _mla_prep_sample(proj, lw, tabs):
    t = proj.shape[0]
    tm = min(t, 256)
    head, tab = _small_specs(tm, 1)
    row = lambda c: pl.BlockSpec((tm, c), lambda i: (i, 0))
    return pl.pallas_call(
        _mla_prep_sample_kernel,
        grid=(t // tm,),
        in_specs=head + [pl.BlockSpec((H_C, LANES, KV_LORA), lambda i: (0, 0, 0)),
                         pl.BlockSpec((LANES, LANES), lambda i: (0, 0))] + tab,
        out_specs=[pl.BlockSpec((H_C, tm, KV_LORA), lambda i: (0, i, 0)),
                   pl.BlockSpec((H_C, tm, LANES), lambda i: (0, i, 0)),
                   row(KV_LORA), row(LANES)],
        out_shape=[jax.ShapeDtypeStruct((H_C, t, KV_LORA), F32),
                   jax.ShapeDtypeStruct((H_C, t, LANES), F32),
                   jax.ShapeDtypeStruct((t, KV_LORA), F32), jax.ShapeDtypeStruct((t, LANES), F32)],
        compiler_params=_cparams(("parallel",)),
    )(proj, proj, proj, lw["q_a_norm"], lw["kv_a_norm"], lw["wq"], lw["wuk_abs"], lw["sel_rope"], *tabs)


NEG = -1e30


def _lane_fold(x, op):
    parts = [x[:, j * LANES:(j + 1) * LANES] for j in range(x.shape[1] // LANES)]
    while len(parts) > 1:
        parts = [op(parts[i], parts[i + 1]) for i in range(0, len(parts) - 1, 2)] + (
            [parts[-1]] if len(parts) % 2 else [])
    return parts[0]


def _flash_kernel(tq, sub, qi_ref, ki_ref, q_ref, kt_ref, v_ref, o_ref, m_ref, l_ref, acc_ref):
    step = pl.program_id(2)
    qi = qi_ref[step]
    ki = ki_ref[step]
    nsub = tq // sub

    @pl.when(ki == 0)
    def _():
        m_ref[...] = jnp.full(m_ref.shape, NEG, F32)
        l_ref[...] = jnp.zeros(l_ref.shape, F32)
        acc_ref[...] = jnp.zeros(acc_ref.shape, F32)

    def scores(r, diagonal):
        nk = (r + 1) * sub if diagonal else tq
        s = jnp.dot(q_ref[0, pl.ds(r * sub, sub), :], kt_ref[0, :, 0:nk], preferred_element_type=F32)
        if diagonal:
            qpos = r * sub + lax.broadcasted_iota(jnp.int32, s.shape, 0)
            kpos = lax.broadcasted_iota(jnp.int32, s.shape, 1)
            s = jnp.where(kpos <= qpos, s, NEG)
        return s

    def block(diagonal):
        s_next = scores(0, diagonal)
        for r in range(nsub):
            s = s_next
            if r + 1 < nsub:
                s_next = scores(r + 1, diagonal)
            rows = pl.ds(r * sub, sub)
            nk = s.shape[1]
            m_old = m_ref[rows, :]
            m_new = jnp.maximum(m_old, jnp.max(_lane_fold(s, jnp.maximum), axis=-1, keepdims=True))
            alpha = jnp.exp2(m_old - m_new)
            p = jnp.exp2(s - m_new)
            l_ref[rows, :] = alpha * l_ref[rows, :] + jnp.sum(_lane_fold(p, jnp.add), axis=-1,
                                                              keepdims=True)
            acc_ref[rows, :] = alpha * acc_ref[rows, :] + jnp.dot(
                p.astype(BF16), v_ref[0, 0:nk, :], preferred_element_type=F32)
            m_ref[rows, :] = m_new

    @pl.when(ki < qi)
    def _():
        block(False)

    @pl.when(ki == qi)
    def _():
        block(True)
        o_ref[0] = acc_ref[...] / l_ref[...]


def _flash_prompt(q, kt, v, nbatch, seq):
    tq = min(seq, 1024)
    nq = seq // tq
    pairs = [(a, c) for a in range(nq) for c in range(a + 1)]
    qi_tab = jnp.asarray([p[0] for p in pairs], jnp.int32)
    ki_tab = jnp.asarray([p[1] for p in pairs], jnp.int32)
    qspec = pl.BlockSpec((1, tq, LANES), lambda b, h, t, qt, kt: (h, b * nq + qt[t], 0))
    kspec = pl.BlockSpec((1, tq, LANES), lambda b, h, t, qt, kt: (h, b * nq + kt[t], 0))
    ktspec = pl.BlockSpec((1, LANES, tq), lambda b, h, t, qt, kt: (h, 0, b * nq + kt[t]))
    grid_spec = pltpu.PrefetchScalarGridSpec(
        num_scalar_prefetch=2,
        grid=(nbatch, H_C, len(pairs)),
        in_specs=[qspec, ktspec, kspec],
        out_specs=qspec,
        scratch_shapes=[pltpu.VMEM((tq, 1), F32), pltpu.VMEM((tq, 1), F32), pltpu.VMEM((tq, LANES), F32)],
    )
    return pl.pallas_call(
        functools.partial(_flash_kernel, tq, min(tq, 256)),
        grid_spec=grid_spec,
        out_shape=jax.ShapeDtypeStruct((H_C, nbatch * seq, LANES), F32),
        compiler_params=_cparams(("parallel", "parallel", "arbitrary")),
    )(qi_tab, ki_tab, q, kt, v)


PAGES_PER_STEP = 16


def _joint_softmax_update(tiles, values, m_ref, l_ref, acc_ref):
    def tree(xs, op):
        while len(xs) > 1:
            xs = [op(xs[i], xs[i + 1]) for i in range(0, len(xs) - 1, 2)] + (
                [xs[-1]] if len(xs) % 2 else [])
        return xs[0]

    m_old = m_ref[...]
    m_new = jnp.maximum(m_old, jnp.max(tree(list(tiles), jnp.maximum), axis=-1, keepdims=True))
    alpha = jnp.exp(m_old - m_new)
    ps = [jnp.exp(s - m_new) for s in tiles]
    pvs = [jnp.dot(p.astype(BF16), c, preferred_element_type=F32) for p, c in zip(ps, values)]
    l_ref[...] = alpha * l_ref[...] + jnp.sum(tree(ps, jnp.add), axis=-1, keepdims=True)
    acc_ref[...] = alpha * acc_ref[...] + tree(pvs, jnp.add)
    m_ref[...] = m_new


def _paged_kernel(snew, npg, pt_ref, ql_ref, qp_ref, *rest):
    ckv_refs = rest[:npg]
    kr_refs = rest[npg:2 * npg]
    cn_ref, kn_ref, wv_ref, o_ref, m_ref, l_ref, acc_ref = rest[2 * npg:]
    pg = pl.program_id(1)
    rows = H_C * snew
    ql = ql_ref[...].reshape(rows, KV_LORA).astype(BF16)
    qp = qp_ref[...].reshape(rows, LANES).astype(BF16)
    page = ckv_refs[0].shape[2]
    per_row = LANES // QK_ROPE
    er = lax.broadcasted_iota(jnp.int32, (page, page // per_row), 0)
    ec = lax.broadcasted_iota(jnp.int32, (page, page // per_row), 1)
    expand = ((er // per_row) == ec).astype(BF16)
    wr = lax.broadcasted_iota(jnp.int32, (page, LANES), 0)
    wc = lax.broadcasted_iota(jnp.int32, (page, LANES), 1)
    own_block = (wc // QK_ROPE) == (wr % per_row)


    @pl.when(pg == 0)
    def _():
        m_ref[...] = jnp.full(m_ref.shape, NEG, F32)
        l_ref[...] = jnp.zeros(l_ref.shape, F32)
        acc_ref[...] = jnp.zeros(acc_ref.shape, F32)

    def scores(cs, krs):
        lat = [_dot_nt(ql, c) for c in cs]
        rot = [_dot_nt(qp, r) for r in krs]
        return [a + b for a, b in zip(lat, rot)]

    cs = [ckv_refs[j][0, 0].astype(BF16) for j in range(npg)]
    packed = [kr_refs[j][0, 0].astype(BF16) for j in range(npg)]
    rows4 = [jnp.dot(expand, x, preferred_element_type=F32) for x in packed]
    krs = [jnp.where(own_block, x, 0.0).astype(BF16) for x in rows4]
    _joint_softmax_update(scores(cs, krs), cs, m_ref, l_ref, acc_ref)

    @pl.when(pg == pl.num_programs(1) - 1)
    def _():
        pad = lambda x: jnp.concatenate(
            [x, jnp.zeros((page - snew, x.shape[1]), F32)], axis=0).astype(BF16)
        c = pad(cn_ref[...])
        s = scores([c], [pad(kn_ref[...])])[0]
        tok = lax.broadcasted_iota(jnp.int32, s.shape, 0) % snew
        key = lax.broadcasted_iota(jnp.int32, s.shape, 1)
        s = jnp.where(key <= tok, s, NEG)
        _joint_softmax_update([s], [c], m_ref, l_ref, acc_ref)
        o_lat = acc_ref[...] / l_ref[...]
        for h in range(H_C):
            o_ref[:, h * V_HEAD:(h + 1) * V_HEAD] = jnp.dot(
                o_lat[h * snew:(h + 1) * snew].astype(BF16), wv_ref[h], preferred_element_type=F32)


def _paged_attention(ql, qp, cache_ckv, cache_kr, layer, ckv_new, kr_new, wv, page_table, snew):
    nb, n_pages = page_table.shape
    page = cache_ckv.shape[2]
    packed_rows = page * QK_ROPE // LANES
    cache_kr = cache_kr.reshape(cache_kr.shape[0], cache_kr.shape[1], packed_rows, LANES)
    npg = min(PAGES_PER_STEP, n_pages)
    ckv_specs = [pl.BlockSpec((1, 1, page, KV_LORA),
                              lambda b, g, pt, j=j: (layer, pt[b, g * npg + j], 0, 0)) for j in range(npg)]
    kr_specs = [pl.BlockSpec((1, 1, packed_rows, LANES),
                             lambda b, g, pt, j=j: (layer, pt[b, g * npg + j], 0, 0)) for j in range(npg)]
    rows = H_C * snew
    grid_spec = pltpu.PrefetchScalarGridSpec(
        num_scalar_prefetch=1,
        grid=(nb, n_pages // npg),
        in_specs=[pl.BlockSpec((H_C, snew, KV_LORA), lambda b, g, pt: (0, b, 0)),
                  pl.BlockSpec((H_C, snew, LANES), lambda b, g, pt: (0, b, 0))]
                 + ckv_specs + kr_specs
                 + [pl.BlockSpec((snew, KV_LORA), lambda b, g, pt: (b, 0)),
                    pl.BlockSpec((snew, LANES), lambda b, g, pt: (b, 0)),
                    pl.BlockSpec((H_C, KV_LORA, V_HEAD), lambda b, g, pt: (0, 0, 0))],
        out_specs=pl.BlockSpec((snew, D_BR), lambda b, g, pt: (b, 0)),
        scratch_shapes=[pltpu.VMEM((rows, 1), F32), pltpu.VMEM((rows, 1), F32),
                        pltpu.VMEM((rows, KV_LORA), F32)],
    )
    return pl.pallas_call(
        functools.partial(_paged_kernel, snew, npg),
        grid_spec=grid_spec,
        out_shape=jax.ShapeDtypeStruct((nb * snew, D_BR), F32),
        compiler_params=_cparams(("parallel", "arbitrary")),
    )(page_table, ql, qp, *([cache_ckv] * npg), *([cache_kr] * npg), ckv_new, kr_new, wv)


def _merge_kernel(nb, tq, y_grouped, o_grouped, y_ref, vkr_ref, bonus_ref, x_ref, o_ref, ga_ref, cb_ref, cc_ref, cx_ref, gb_ref,
                  gc_ref, ma_ref, mb_ref, mc_ref, hc_ref, hx_ref, lnw_ref, lnb_ref, cw_ref, np_ref,
                  wbr_ref, wout_ref, xo_ref, zt_ref):
    hm = _head_ones()
    tm = nb * tq
    if y_grouped:
        y = jnp.concatenate([y_ref[g] for g in range(N_LANE_GROUPS)], axis=1)
    else:
        y = y_ref[...] + vkr_ref[...]
    mu = _head_sum(y, hm) * (1.0 / HEAD_A)
    yc = y - mu
    var = _head_sum(yc * yc, hm) * (1.0 / HEAD_A)
    out_a = (yc * lax.rsqrt(var + GN_EPS) * lnw_ref[...] + lnb_ref[...] + bonus_ref[...]) * _silu(ga_ref[...])
    z = cc_ref[...] * cx_ref[...]
    zh = hc_ref[...] * hx_ref[...]
    z1 = _shift_rows(z, zh[:, 1:2, :], 1, nb, tq)
    z2 = _shift_rows(z, zh, 2, nb, tq)
    cw = cw_ref[...]
    conv = cw[0:1] * z2 + cw[1:2] * z1 + cw[2:3] * z
    out_b = cb_ref[...] * conv * _silu(gb_ref[...])
    if nb == 1:
        zt_ref[0] = z[tm - SUBLANES:, :]
    else:
        zt_ref[...] = z.reshape(nb, tq, D_BR)
    if o_grouped:
        o = jnp.concatenate([o_ref[h] for h in range(H_C)], axis=1)
    else:
        o = o_ref[...]
    out_c = o * _silu(gc_ref[...])
    mixed = (_sigmoid(ma_ref[...]) * jnp.dot(out_a.astype(BF16), wbr_ref[0], preferred_element_type=F32)
             + _sigmoid(mb_ref[...]) * jnp.dot(out_b.astype(BF16), wbr_ref[1], preferred_element_type=F32)
             + _sigmoid(mc_ref[...]) * jnp.dot(out_c.astype(BF16), wbr_ref[2], preferred_element_type=F32))
    res = jnp.dot(mixed.astype(BF16), wout_ref[...], preferred_element_type=F32)
    xo_ref[...] = x_ref[...] + _rms(res, np_ref[...])


def _merge(y, vkr, bonus, x, o, proj, halo_c, halo_x, lw, nb, tq):
    t = x.shape[0]
    y_grouped, o_grouped = y.ndim == 3, o.ndim == 3
    tm = nb * tq
    nseg = t // tq
    row = pl.BlockSpec((tm, D_BR), lambda i: (i, 0))
    prow = lambda cb: pl.BlockSpec((tm, D_BR), lambda i, cb=cb: (i, cb))
    halo = pl.BlockSpec((nb, 2, D_BR), lambda i: (i, 0, 0))
    vec = pl.BlockSpec((1, D_BR), lambda i: (0, 0))
    grouped = pl.BlockSpec((N_LANE_GROUPS, tm, LANES), lambda i: (0, i, 0))
    zt_rows = min(tq, SUBLANES)
    return pl.pallas_call(
        functools.partial(_merge_kernel, nb, tq, y_grouped, o_grouped),
        grid=(t // tm,),
        in_specs=[grouped if y_grouped else row, row, row, row, grouped if o_grouped else row] + [prow(c) for c in range(3, 12)] + [halo, halo, vec, vec,
                  pl.BlockSpec((3, D_BR), lambda i: (0, 0)), vec,
                  pl.BlockSpec((3, D_BR, D_MODEL), lambda i: (0, 0, 0)),
                  pl.BlockSpec((D_MODEL, D_MODEL), lambda i: (0, 0))],
        out_specs=[row, pl.BlockSpec((nb, zt_rows, D_BR), lambda i: (i, 0, 0))],
        out_shape=[jax.ShapeDtypeStruct((t, D_MODEL), F32),
                   jax.ShapeDtypeStruct((nseg, zt_rows, D_BR), F32)],
        compiler_params=_cparams(("parallel",)),
    )(y, vkr, bonus, x, o, *([proj] * 9), halo_c, halo_x, lw["ln_x_w"], lw["ln_x_b"], lw["conv_w"],
      lw["norm_post"], lw["w_branch"], lw["w_out"])


def _layer_weights(l, p):
    w_in = p["w_in"][l]
    z = lambda n: jnp.zeros((D_MODEL, n), F32)
    o_ga = 3 * D_BR + DECAY_LORA + ICLR_LORA
    o_qa = o_ga + 5 * D_BR
    o_kpe = o_qa + Q_LORA + KV_LORA
    o_gc = o_kpe + QK_ROPE
    w_re = jnp.concatenate([
        w_in[:, :3 * D_BR], w_in[:, o_ga:o_qa], w_in[:, o_gc:o_gc + 4 * D_BR],
        w_in[:, o_qa:o_kpe], w_in[:, 3 * D_BR:o_ga],
        z(ROPE_LANE0), w_in[:, o_kpe:o_gc], z(LANES - ROPE_LANE0 - QK_ROPE),
        z(N_PROJ - COL_KPE - LANES)], axis=1).astype(BF16)
    mu = p["mu_shift"][l]
    r2 = lambda a: a.reshape(1, -1)
    w_uq = p["w_uq"][l]
    wq = jnp.concatenate([w_uq, jnp.zeros((Q_LORA, H_C, LANES - QK_NOPE - QK_ROPE), F32)], axis=2)
    w_uk = p["w_uk"][l]
    wk = jnp.concatenate([w_uk, jnp.zeros((KV_LORA, H_C, LANES - QK_NOPE), F32)], axis=2)
    wuk_abs = jnp.concatenate([jnp.transpose(w_uk, (1, 2, 0)),
                               jnp.zeros((H_C, LANES - QK_NOPE, KV_LORA), F32)], axis=1)
    sel = np.zeros((LANES, LANES), np.float32)
    for blk in range(LANES // QK_ROPE):
        sel[ROPE_LANE0 + np.arange(QK_ROPE), blk * QK_ROPE + np.arange(QK_ROPE)] = 1.0
    zl = jnp.zeros((DECAY_LORA, D_BR), F32)
    return {
        "w_in": w_re, "norm_pre": r2(p["norm_pre"][l]), "norm_post": r2(p["norm_post"][l]),
        "mu_r": r2(mu[:D_BR]), "mu_k": r2(mu[D_BR:2 * D_BR]), "mu_v": r2(mu[2 * D_BR:3 * D_BR]),
        "mu_lora": r2(mu[3 * D_BR:]),
        "w0": r2(p["w0"][l]), "a0": r2(p["a0"][l]),
        "w2p": jnp.concatenate([p["w2"][l], zl], axis=0).astype(BF16),
        "a2p": jnp.concatenate([zl, p["a2"][l]], axis=0).astype(BF16),
        "k_k": r2(p["k_k"][l]), "k_a": r2(p["k_a"][l]), "r_k": r2(p["r_k"][l]),
        "ln_x_w": r2(p["ln_x_w"][l]), "ln_x_b": r2(p["ln_x_b"][l]), "conv_w": p["conv_w"][l],
        "q_a_norm": r2(p["q_a_norm"][l]), "kv_a_norm": r2(p["kv_a_norm"][l]),
        "wq": wq.reshape(Q_LORA, H_C * LANES).astype(BF16),
        "wk": wk.reshape(KV_LORA, H_C * LANES).astype(BF16),
        "wv": p["w_uv"][l].reshape(KV_LORA, H_C * V_HEAD).astype(BF16),
        "wuk_abs": wuk_abs.astype(BF16),
        "wv_heads": jnp.transpose(p["w_uv"][l], (1, 0, 2)).astype(BF16),
        "sel_rope": jnp.asarray(sel, BF16),
        "w_branch": p["w_branch"][l].astype(BF16), "w_out": p["w_out"][l].astype(BF16),
    }


def _rope_tables(pos):
    half = QK_ROPE // 2
    freqs = ROPE_THETA ** (-jnp.arange(half, dtype=F32) / half)
    ang = pos.astype(F32)[:, None] * freqs[None, :]
    cos, sin = jnp.cos(ang), jnp.sin(ang)
    n = pos.shape[0]
    lo = jnp.zeros((n, ROPE_LANE0), F32)
    hi = jnp.zeros((n, LANES - ROPE_LANE0 - QK_ROPE), F32)
    zh = jnp.zeros((n, half), F32)
    ck = jnp.concatenate([lo, cos, cos, hi], axis=1)
    cq = jnp.concatenate([lo + 1.0, cos, cos, hi], axis=1)
    s1 = jnp.concatenate([lo, zh, sin, hi], axis=1)
    s2 = jnp.concatenate([lo, -sin, zh, hi], axis=1)
    return cq, ck, s1, s2


def _prev_row_halos(proj, first_rows, nseq, tseq, tq):
    nseg_per = tseq // tq
    out = {}
    for name, (c0, c1) in {"r": (COL_R, COL_R + D_BR), "k": (COL_K, COL_K + D_BR),
                           "v": (COL_V, COL_V + D_BR), "lora": (COL_LORA, COL_LORA + LANES)}.items():
        first = first_rows[name].reshape(nseq, 1, c1 - c0)
        if nseg_per > 1:
            p3 = proj.reshape(nseq, tseq, N_PROJ)
            inner = p3[:, tq - 1:tseq - 1:tq, c0:c1]
            first = jnp.concatenate([first, inner], axis=1)
        out[name] = first.reshape(nseq * nseg_per, 1, c1 - c0)
    return out


def _conv_halos(proj, conv_prev, nseq, tseq, tq):
    nseg_per = tseq // tq
    hc = conv_prev
    hx = jnp.ones_like(conv_prev)
    if nseg_per > 1:
        p3 = proj.reshape(nseq, nseg_per, tq, N_PROJ)[:, :-1, tq - 2:, :]
        hc = jnp.concatenate([hc[:, None], p3[..., COL_CC:COL_CC + D_BR]], axis=1)
        hx = jnp.concatenate([hx[:, None], p3[..., COL_CX:COL_CX + D_BR]], axis=1)
    return hc.reshape(nseq * nseg_per, 2, D_BR), hx.reshape(nseq * nseg_per, 2, D_BR)


def _layer(x, lw, tabs, nseq, tseq, shift_prev, conv_prev, wkv_prev, attn):
    t = nseq * tseq
    proj = _inproj(x, lw["norm_pre"], lw["w_in"])
    if tseq >= 128:
        nb, tq = 1, 128
    else:
        nb, tq = min(nseq, 128 // tseq), tseq
    first = {"r": shift_prev[:, :D_BR], "k": shift_prev[:, D_BR:2 * D_BR],
             "v": shift_prev[:, 2 * D_BR:3 * D_BR], "lora": shift_prev[:, 3 * D_BR:]}
    chunked = tseq % WKV_CHUNK == 0
    ops = _rwkv_prep(proj, _prev_row_halos(proj, first, nseq, tseq, tq), lw, nb, tq, chunked)
    if chunked:
        bonus = ops[6]
        vkr = bonus
        w6 = wkv_prev.reshape(nseq, N_LANE_GROUPS, 2, HEAD_A, HEAD_A)
        s0 = jnp.zeros((nseq, N_LANE_GROUPS, 2, HEAD_A, 2, HEAD_A), F32)
        s0 = s0.at[:, :, 0, :, 0, :].set(w6[:, :, 0]).at[:, :, 1, :, 1, :].set(w6[:, :, 1])
        y, st = _wkv_chunked(ops[:6], s0.reshape(nseq, N_LANE_GROUPS, LANES, LANES), nseq, tseq)
        st6 = st.reshape(nseq, N_LANE_GROUPS, 2, HEAD_A, 2, HEAD_A)
        wkv_new = jnp.stack([st6[:, :, 0, :, 0, :], st6[:, :, 1, :, 1, :]], axis=2).reshape(
            nseq, H_A, HEAD_A, HEAD_A)
    else:
        vkr, bonus = ops[7], ops[8]
        s0 = jnp.transpose(wkv_prev, (0, 2, 1, 3)).reshape(nseq, HEAD_A, D_BR)
        y, st = _wkv_scan(ops[:7], s0, nseq, tseq)
        wkv_new = jnp.transpose(st.reshape(nseq, HEAD_A, H_A, HEAD_A), (0, 2, 1, 3))
    o, ckv, kr128 = attn(proj, lw, tabs)
    hc, hx = _conv_halos(proj, conv_prev, nseq, tseq, tq)
    x_new, zt = _merge(y, vkr, bonus, x, o, proj, hc, hx, lw, nb, tq)
    p3 = proj.reshape(nseq, tseq, N_PROJ)[:, -1, :]
    new_shift = jnp.concatenate([p3[:, :3 * D_BR], p3[:, COL_LORA:COL_LORA + LANES]], axis=1)
    new_conv = zt.reshape(nseq, tseq // tq, -1, D_BR)[:, -1, -2:, :]
    kr = kr128[:, ROPE_LANE0:ROPE_LANE0 + QK_ROPE]
    return x_new, (ckv.reshape(nseq, tseq, KV_LORA), kr.reshape(nseq, tseq, QK_ROPE), wkv_new,
                   new_shift, new_conv)


def kernel(x_prompt, x_sample, cache_ckv, cache_krope, state_wkv, state_shift, state_conv, page_table,
           norm_pre, norm_post, w_in, mu_shift, w0, w2, a0, a2, k_k, k_a, r_k, ln_x_w, ln_x_b,
           conv_w, q_a_norm, w_uq, kv_a_norm, w_uk, w_uv, w_branch, w_out):
    params = dict(norm_pre=norm_pre, norm_post=norm_post, w_in=w_in, mu_shift=mu_shift, w0=w0, w2=w2,
                  a0=a0, a2=a2, k_k=k_k, k_a=k_a, r_k=r_k.reshape(r_k.shape[0], -1), ln_x_w=ln_x_w,
                  ln_x_b=ln_x_b, conv_w=conv_w, q_a_norm=q_a_norm, w_uq=w_uq, kv_a_norm=kv_a_norm,
                  w_uk=w_uk, w_uv=w_uv, w_branch=w_branch, w_out=w_out)
    bp, sp, _ = x_prompt.shape
    bd, sd, _ = x_sample.shape
    depth = w_in.shape[0]
    past_len = page_table.shape[1] * cache_ckv.shape[2]
    tabs_p = _rope_tables(jnp.arange(sp))
    tm_s = min(bd * sd, 256)
    tabs_s = _rope_tables(past_len + (jnp.arange(tm_s) % sd))

    xp = x_prompt.reshape(bp * sp, D_MODEL)
    xs = x_sample.reshape(bd * sd, D_MODEL)
    acc_p, acc_s = [], []
    for l in range(depth):
        lw = _layer_weights(l, params)

        def attn_prompt(proj, lw, tabs):
            q, k, v, ckv, kr128 = _mla_prep_prompt(proj, lw, tabs, sp)
            return _flash_prompt(q, k, v, bp, sp), ckv, kr128

        def attn_sample(proj, lw, tabs, l=l):
            ql, qp, ckv, kr128 = _mla_prep_sample(proj, lw, tabs)
            o = _paged_attention(ql, qp, cache_ckv, cache_krope, l, ckv, kr128, lw["wv_heads"],
                                 page_table, sd)
            return o, ckv, kr128

        xp, st_p = _layer(xp, lw, tabs_p, bp, sp, jnp.zeros((bp, state_shift.shape[2]), F32),
                          jnp.zeros((bp, 2, D_BR), F32), jnp.zeros((bp, H_A, HEAD_A, HEAD_A), F32),
                          attn_prompt)
        xs, st_s = _layer(xs, lw, tabs_s, bd, sd, state_shift[l], state_conv[l], state_wkv[l],
                          attn_sample)
        acc_p.append(st_p)
        acc_s.append(st_s)
    outs_p = [jnp.stack([a[i] for a in acc_p], axis=0) for i in range(5)]
    outs_s = [jnp.stack([a[i] for a in acc_s], axis=0) for i in range(5)]
    return (xp.reshape(bp, sp, D_MODEL), xs.reshape(bd, sd, D_MODEL), *outs_p, *outs_s)
```

```python
import functools
import math

import numpy as np
import jax
import jax.numpy as jnp
from jax import lax
from jax.experimental import pallas as pl
from jax.experimental.pallas import tpu as pltpu

F32 = jnp.float32
BF16 = jnp.bfloat16

D_MODEL = 1024
D_BR = D_MODEL
HEAD_A = 64
H_A = D_BR // HEAD_A
DECAY_LORA = 64
ICLR_LORA = 64
GN_EPS = 64e-5
H_C = 8
QK_NOPE = 64
QK_ROPE = 32
V_HEAD = D_BR // H_C
Q_LORA = D_MODEL // 4
KV_LORA = D_MODEL // 4
ROPE_THETA = 10000.0
NORM_EPS = 1e-6
LANES = 128
SUBLANES = 8
N_LANE_GROUPS = D_BR // LANES
ROPE_LANE0 = 64
VMEM_LIMIT = 56 * 1024 * 1024
ROW_TILE = 256

COL_R, COL_K, COL_V, COL_GA, COL_CB, COL_CC, COL_CX, COL_GB, COL_GC, COL_MA, COL_MB, COL_MC = (
    i * D_BR for i in range(12))
COL_QA = 12 * D_BR
COL_KVA = COL_QA + Q_LORA
COL_LORA = COL_KVA + KV_LORA
COL_KPE = COL_LORA + LANES
N_PROJ = 13 * D_BR


def _cparams(sem):
    return pltpu.CompilerParams(dimension_semantics=sem, vmem_limit_bytes=VMEM_LIMIT)


def _head_ones():
    r = lax.broadcasted_iota(jnp.int32, (LANES, LANES), 0) // HEAD_A
    c = lax.broadcasted_iota(jnp.int32, (LANES, LANES), 1) // HEAD_A
    return (r == c).astype(BF16)


def _split_dot(x, w):
    hi = x.astype(BF16)
    lo = (x - hi.astype(F32)).astype(BF16)
    return (jnp.dot(hi, w, preferred_element_type=F32) + jnp.dot(lo, w, preferred_element_type=F32))


def _split_dot_rhs(w, x):
    hi = x.astype(BF16)
    lo = (x - hi.astype(F32)).astype(BF16)
    return (jnp.dot(w, hi, preferred_element_type=F32) + jnp.dot(w, lo, preferred_element_type=F32))


def _head_sum(x, hm):
    parts = [_split_dot(x[:, g * LANES:(g + 1) * LANES], hm) for g in range(x.shape[1] // LANES)]
    return jnp.concatenate(parts, axis=1)


def _shift_rows(p, halo, k, nb, tq):
    c = p.shape[1]
    prev = pltpu.roll(p, k, axis=0)
    pos = lax.broadcasted_iota(jnp.int32, p.shape, 0) % tq
    out = prev
    for j in range(k):
        hb = jnp.broadcast_to(halo[:, j:j + 1, :], (nb, tq, c)).reshape(nb * tq, c)
        out = jnp.where(pos == j, hb, out)
    return out


def _sigmoid(x):
    return 1.0 / (1.0 + jnp.exp(-x))


def _silu(x):
    return x * _sigmoid(x)


def _rms(x, g):
    return x * lax.rsqrt(jnp.mean(x * x, axis=-1, keepdims=True) + NORM_EPS) * g


def _inproj_kernel(x_ref, g_ref, w_ref, o_ref, h_ref):
    @pl.when(pl.program_id(1) == 0)
    def _():
        h_ref[...] = _rms(x_ref[...], g_ref[...]).astype(BF16)

    o_ref[...] = jnp.dot(h_ref[...], w_ref[...], preferred_element_type=F32)


def _inproj(x, g, w):
    t = x.shape[0]
    tm = min(t, 1024)
    tn = 1024
    return pl.pallas_call(
        _inproj_kernel,
        grid=(t // tm, N_PROJ // tn),
        in_specs=[pl.BlockSpec((tm, D_MODEL), lambda i, j: (i, 0)),
                  pl.BlockSpec((1, D_MODEL), lambda i, j: (0, 0)),
                  pl.BlockSpec((D_MODEL, tn), lambda i, j: (0, j))],
        out_specs=pl.BlockSpec((tm, tn), lambda i, j: (i, j)),
        out_shape=jax.ShapeDtypeStruct((t, N_PROJ), F32),
        scratch_shapes=[pltpu.VMEM((tm, D_MODEL), BF16)],
        compiler_params=_cparams(("parallel", "arbitrary")),
    )(x, g, w)


def _rwkv_prep_kernel(nb, tq, chunked, r_ref, k_ref, v_ref, lo_ref, hr_ref, hk_ref, hv_ref, hl_ref,
                      mur_ref, muk_ref, muv_ref, mul_ref, w0_ref, w2_ref, a0_ref, a2_ref,
                      kk_ref, ka_ref, rk_ref, *outs):
    hm = _head_ones()

    def mix(ref, halo_ref, mu_ref):
        p = ref[...]
        prev = _shift_rows(p, halo_ref[...], 1, nb, tq)
        return p + (prev - p) * mu_ref[...]

    r = mix(r_ref, hr_ref, mur_ref)
    k = mix(k_ref, hk_ref, muk_ref)
    v = mix(v_ref, hv_ref, muv_ref)
    lo = mix(lo_ref, hl_ref, mul_ref)
    wf = w0_ref[...] + jnp.dot(jnp.tanh(lo).astype(BF16), w2_ref[...], preferred_element_type=F32)
    log_decay = -math.exp(-0.5) * _sigmoid(wf)
    a = _sigmoid(a0_ref[...] + jnp.dot(lo.astype(BF16), a2_ref[...], preferred_element_type=F32))
    kk = k * kk_ref[...]
    kk = kk / jnp.maximum(jnp.sqrt(_head_sum(kk * kk, hm)), 1e-12)
    kmod = k * (1.0 + (a - 1.0) * ka_ref[...])
    b = kk * a
    bonus = _head_sum(r * kmod * rk_ref[...], hm) * v
    if chunked:
        *group_outs, bonus_o = outs
        bonus_o[...] = bonus
        for o_ref, val in zip(group_outs, (r, log_decay, kmod, v, -kk, b)):
            for g in range(N_LANE_GROUPS):
                o_ref[g] = val[:, g * LANES:(g + 1) * LANES]
        return
    wr_o, a_o, w_o, b_o, k_o, v_o, br_o, vkr_o, bonus_o = outs
    decay = jnp.exp(log_decay)
    wr_o[...] = decay * r
    a_o[...] = -kk
    w_o[...] = decay
    b_o[...] = b
    k_o[...] = kmod
    v_o[...] = v
    br_o[...] = _head_sum(b * r, hm)
    vkr_o[...] = _head_sum(kmod * r, hm) * v
    bonus_o[...] = bonus


def _rwkv_prep(proj, halos, lw, nb, tq, chunked):
    t = proj.shape[0]
    tm = nb * tq
    row_out = pl.BlockSpec((tm, D_BR), lambda i: (i, 0))
    if chunked:
        out_specs = [pl.BlockSpec((N_LANE_GROUPS, tm, LANES), lambda i: (0, i, 0))] * 6 + [row_out]
        outs = [jax.ShapeDtypeStruct((N_LANE_GROUPS, t, LANES), F32)] * 6 + [
            jax.ShapeDtypeStruct((t, D_BR), F32)]
    else:
        out_specs = [row_out] * 9
        outs = [jax.ShapeDtypeStruct((t, D_BR), F32)] * 9
    row = lambda cb: pl.BlockSpec((tm, D_BR), lambda i, cb=cb: (i, cb))
    halo = lambda c: pl.BlockSpec((nb, 1, c), lambda i: (i, 0, 0))
    vec = lambda c: pl.BlockSpec((1, c), lambda i: (0, 0))
    mat = pl.BlockSpec((LANES, D_BR), lambda i: (0, 0))
    return pl.pallas_call(
        functools.partial(_rwkv_prep_kernel, nb, tq, chunked),
        grid=(t // tm,),
        in_specs=[row(0), row(1), row(2),
                  pl.BlockSpec((tm, LANES), lambda i: (i, COL_LORA // LANES)),
                  halo(D_BR), halo(D_BR), halo(D_BR), halo(LANES),
                  vec(D_BR), vec(D_BR), vec(D_BR), vec(LANES),
                  vec(D_BR), mat, vec(D_BR), mat, vec(D_BR), vec(D_BR), vec(D_BR)],
        out_specs=out_specs,
        out_shape=outs,
        compiler_params=_cparams(("parallel",)),
    )(proj, proj, proj, proj, halos["r"], halos["k"], halos["v"], halos["lora"],
      lw["mu_r"], lw["mu_k"], lw["mu_v"], lw["mu_lora"], lw["w0"], lw["w2p"], lw["a0"], lw["a2p"],
      lw["k_k"], lw["k_a"], lw["r_k"])


def _wkv_kernel(ns, tb, wr_ref, a_ref, w_ref, b_ref, k_ref, v_ref, br_ref, s0_ref,
                y_ref, st_ref, p_ref):
    @pl.when(pl.program_id(1) == 0)
    def _():
        st_ref[...] = s0_ref[...]

    hm = _head_ones()
    ri = lax.broadcasted_iota(jnp.int32, (HEAD_A, LANES), 0)
    ci = lax.broadcasted_iota(jnp.int32, (HEAD_A, LANES), 1)
    diag = ((ci % HEAD_A) == ri).astype(F32)

    er = lax.broadcasted_iota(jnp.int32, (SUBLANES, HEAD_A), 0)
    ec = lax.broadcasted_iota(jnp.int32, (SUBLANES, HEAD_A), 1)
    tok_sum = ((ec // SUBLANES) == er).astype(BF16)

    def step(u, carry, base):
        t = base + u
        for s in range(ns):
            row = lambda ref: ref[s, pl.ds(t, 1), :]
            wr, a, w, b, k, v, br = (row(x) for x in (wr_ref, a_ref, w_ref, b_ref, k_ref, v_ref, br_ref))
            for g in range(N_LANE_GROUPS):
                sl = slice(g * LANES, (g + 1) * LANES)
                st = st_ref[s, :, sl]
                m1 = st * a[:, sl]
                m1_hi = m1.astype(BF16)
                m1_lo = (m1 - m1_hi.astype(F32)).astype(BF16)
                m2 = (st * wr[:, sl]).astype(BF16)
                m3 = (diag * v[:, sl]).astype(BF16)
                res = jnp.dot(jnp.concatenate([m1_hi, m1_lo, m2, m3], axis=0), hm,
                              preferred_element_type=F32)
                sa = res[0:HEAD_A] + res[HEAD_A:2 * HEAD_A]
                y2 = res[2 * HEAD_A:3 * HEAD_A]
                vc = res[3 * HEAD_A:4 * HEAD_A]
                st_ref[s, :, sl] = st * w[:, sl] + sa * b[:, sl] + vc * k[:, sl]
                yd = (y2 + sa * br[:, sl]) * diag
                p_ref[s, u, :, sl] = jnp.sum(yd.reshape(HEAD_A // SUBLANES, SUBLANES, LANES), axis=0)
        return carry

    def block8(t8, carry):
        base = pl.multiple_of(t8 * SUBLANES, SUBLANES)
        lax.fori_loop(0, SUBLANES, functools.partial(step, base=base), 0)
        for s in range(ns):
            y_ref[s, pl.ds(base, SUBLANES), :] = _split_dot_rhs(
                tok_sum, p_ref[s].reshape(SUBLANES * SUBLANES, D_BR))
        return carry

    lax.fori_loop(0, tb // SUBLANES, block8, 0)


def _wkv_scan(ops, s0, nseq, tseq):
    ns = 4 if nseq % 4 == 0 else 2
    tb = min(tseq, 128)
    ops3 = [o.reshape(nseq, tseq, D_BR) for o in ops]
    blk = pl.BlockSpec((ns, tb, D_BR), lambda i, j: (i, j, 0))
    sblk = pl.BlockSpec((ns, HEAD_A, D_BR), lambda i, j: (i, 0, 0))
    y, st = pl.pallas_call(
        functools.partial(_wkv_kernel, ns, tb),
        grid=(nseq // ns, tseq // tb),
        in_specs=[blk] * 7 + [sblk],
        out_specs=[blk, sblk],
        out_shape=[jax.ShapeDtypeStruct((nseq, tseq, D_BR), F32),
                   jax.ShapeDtypeStruct((nseq, HEAD_A, D_BR), F32)],
        scratch_shapes=[pltpu.VMEM((ns, SUBLANES, SUBLANES, D_BR), F32)],
        compiler_params=_cparams(("parallel", "arbitrary")),
    )(*ops3, s0)
    return y.reshape(nseq * tseq, D_BR), st


WKV_CHUNK = 64
WKV_PASSES_LOCAL = 1
WKV_PASSES_STATE = 3


def _dot_nn(a, b):
    return jnp.dot(a, b, preferred_element_type=F32)


def _dot_nt(a, b):
    return lax.dot_general(a, b, (((1,), (1,)), ((), ())), preferred_element_type=F32)


def _dot_tn(a, b):
    return lax.dot_general(a, b, (((0,), (0,)), ((), ())), preferred_element_type=F32)


def _hilo(x):
    hi = x.astype(BF16)
    return hi, (x - hi.astype(F32)).astype(BF16)


def _mm(dot, a, b, passes):
    ah, al = _hilo(a)
    bh, bl = _hilo(b)
    out = dot(ah, bh)
    if passes >= 3:
        out = out + dot(al, bh) + dot(ah, bl)
    return out


def _wkv_chunk_kernel(ns, r_ref, lw_ref, k_ref, v_ref, a_ref, b_ref, s0_ref, y_ref, st_ref):
    L = r_ref.shape[2]

    @pl.when(pl.program_id(1) == 0)
    def _():
        st_ref[...] = s0_ref[...]

    ti = lax.broadcasted_iota(jnp.int32, (L, L), 0)
    si = lax.broadcasted_iota(jnp.int32, (L, L), 1)
    incl = si <= ti
    strict = si < ti
    eye_l = (si == ti).astype(F32)
    tril_ones = incl.astype(BF16)
    levels = []
    m = 1
    while m < L:
        levels.append(((ti // (2 * m)) == (si // (2 * m))) & ((ti % (2 * m)) >= m) & ((si % (2 * m)) < m))
        m *= 2
    lane = lax.broadcasted_iota(jnp.int32, (1, LANES), 1)
    head_masks = [(lane < HEAD_A).astype(F32), (lane >= HEAD_A).astype(F32)]
    gr = lax.broadcasted_iota(jnp.int32, (LANES, LANES), 0)
    gc = lax.broadcasted_iota(jnp.int32, (LANES, LANES), 1)
    same_head = (gr // HEAD_A) == (gc // HEAD_A)
    eye_g = gr == gc

    def each(fn, *lists):
        return [fn(*xs) for xs in zip(*lists)]

    def mm(dot, xs, ys, passes=WKV_PASSES_LOCAL):
        return each(lambda x, y: _mm(dot, x, y, passes), xs, ys)

    chains = [(s, g) for s in range(ns) for g in range(N_LANE_GROUPS)]
    r, lw, k, v, a, b = ([ref[g, s] for s, g in chains]
                         for ref in (r_ref, lw_ref, k_ref, v_ref, a_ref, b_ref))
    st = [st_ref[s, g] for s, g in chains]

    def cumsum(x):
        hi, lo = _hilo(x)
        return _dot_nn(tril_ones, hi) + _dot_nn(tril_ones, lo)

    c = each(cumsum, lw)
    c_last = [x[L - 1:L, :] for x in c]
    e_neg = [jnp.exp(-x) for x in c]
    e_end = each(lambda cl, x: jnp.exp(cl - x), c_last, c)
    at = each(lambda a_, x, w_: a_ * jnp.exp(x - w_), a, c, lw)
    rt = each(lambda r_, x: r_ * jnp.exp(x), r, c)
    bt = each(jnp.multiply, b, e_neg)
    kt = each(jnp.multiply, k, e_neg)
    bend = each(jnp.multiply, b, e_end)
    kend = each(jnp.multiply, k, e_end)

    def per_head(xs):
        return [x * hm_ for x in xs for hm_ in head_masks]

    def per_group(xs):
        return [x for x in xs for _ in head_masks]

    at_h, rt_h, v_h = per_head(at), per_head(rt), per_head(v)
    q2 = each(lambda x, y: jnp.concatenate([x, y], axis=0), at_h, rt_h)
    gb = mm(_dot_nt, q2, per_group(bt))
    gk = mm(_dot_nt, q2, per_group(kt))
    n = [jnp.where(strict, x[:L], 0.0) for x in gb]
    pb = [jnp.where(incl, x[L:], 0.0) for x in gb]
    m_ = [jnp.where(strict, x[:L], 0.0) for x in gk]
    pk = [jnp.where(incl, x[L:], 0.0) for x in gk]
    t_inv = [eye_l + jnp.where(levels[0], x, 0.0) for x in n]
    for lev in levels[1:]:
        cm = [jnp.where(lev, x, 0.0) for x in n]
        t_inv = each(jnp.add, t_inv, mm(_dot_nn, mm(_dot_nn, t_inv, cm), t_inv))
    abar_h = mm(_dot_nn, t_inv, at_h)
    uv_h = mm(_dot_nn, mm(_dot_nn, t_inv, m_), v_h)
    rbar_h = mm(_dot_nn, pb, abar_h)
    yv_h = each(jnp.add, mm(_dot_nn, pb, uv_h), mm(_dot_nn, pk, v_h))

    def head_total(xs):
        return [xs[2 * i] + xs[2 * i + 1] for i in range(len(chains))]

    abar, uv, yv = head_total(abar_h), head_total(uv_h), head_total(yv_h)
    rbar = each(jnp.add, rt, head_total(rbar_h))
    phi = each(lambda x, cl: jnp.where(same_head, x, 0.0) + jnp.where(eye_g, jnp.exp(cl), 0.0),
               mm(_dot_tn, abar, bend), c_last)
    psi = each(lambda x, y: jnp.where(same_head, x + y, 0.0), mm(_dot_tn, uv, bend), mm(_dot_tn, v, kend))
    y = each(jnp.add, mm(_dot_nt, rbar, st, WKV_PASSES_STATE), yv)
    st_new = each(jnp.add, mm(_dot_nn, st, phi, WKV_PASSES_STATE), psi)
    for (s, g), y_, st_ in zip(chains, y, st_new):
        y_ref[g, s] = y_
        st_ref[s, g] = st_


def _wkv_chunked(ops, s0, nseq, tseq):
    ns = 2
    L = WKV_CHUNK
    ops4 = [o.reshape(N_LANE_GROUPS, nseq, tseq, LANES) for o in ops]
    blk = pl.BlockSpec((N_LANE_GROUPS, ns, L, LANES), lambda i, j: (0, i, j, 0))
    sblk = pl.BlockSpec((ns, N_LANE_GROUPS, LANES, LANES), lambda i, j: (i, 0, 0, 0))
    y, st = pl.pallas_call(
        functools.partial(_wkv_chunk_kernel, ns),
        grid=(nseq // ns, tseq // L),
        in_specs=[blk] * 6 + [sblk],
        out_specs=[blk, sblk],
        out_shape=[jax.ShapeDtypeStruct((N_LANE_GROUPS, nseq, tseq, LANES), F32),
                   jax.ShapeDtypeStruct((nseq, N_LANE_GROUPS, LANES, LANES), F32)],
        compiler_params=_cparams(("parallel", "arbitrary")),
    )(*ops4, s0)
    return y.reshape(N_LANE_GROUPS, nseq * tseq, LANES), st


def _rope(x, c_ref, s1_ref, s2_ref):
    parts = []
    for h in range(x.shape[1] // LANES):
        xh = x[:, h * LANES:(h + 1) * LANES]
        half = QK_ROPE // 2
        parts.append(xh * c_ref[...] + pltpu.roll(xh, half, axis=1) * s1_ref[...]
                     + pltpu.roll(xh, LANES - half, axis=1) * s2_ref[...])
    return parts[0] if len(parts) == 1 else jnp.concatenate(parts, axis=1)


SOFTMAX_SCALE = (QK_NOPE + QK_ROPE) ** -0.5
LOG2E = math.log2(math.e)


def _mla_common(qa_ref, kva_ref, kpe_ref, qn_ref, kvn_ref, wq_ref, cq_ref, ck_ref, s1_ref, s2_ref,
                scale=SOFTMAX_SCALE):
    qn = _rms(qa_ref[...], qn_ref[...]).astype(BF16)
    q = jnp.dot(qn, wq_ref[...], preferred_element_type=F32)
    q = _rope(q, cq_ref, s1_ref, s2_ref) * scale
    ckv = _rms(kva_ref[...], kvn_ref[...])
    kr = _rope(kpe_ref[...], ck_ref, s1_ref, s2_ref)
    return q, ckv, kr


def _mla_prep_prompt_kernel(qa_ref, kva_ref, kpe_ref, qn_ref, kvn_ref, wq_ref, wk_ref, wv_ref,
                            cq_ref, ck_ref, s1_ref, s2_ref, q_o, k_o, v_o, ckv_o, kr_o):
    q, ckv, kr = _mla_common(qa_ref, kva_ref, kpe_ref, qn_ref, kvn_ref, wq_ref, cq_ref, ck_ref,
                             s1_ref, s2_ref, scale=SOFTMAX_SCALE * LOG2E)
    ckv_o[...] = ckv
    kr_o[...] = kr
    cb = ckv.astype(BF16)
    kn = jnp.dot(cb, wk_ref[...], preferred_element_type=F32)
    vv = jnp.dot(cb, wv_ref[...], preferred_element_type=F32)
    for h in range(H_C):
        sl = slice(h * LANES, (h + 1) * LANES)
        q_o[h] = q[:, sl].astype(BF16)
        k_o[h] = (kn[:, sl] + kr).T.astype(BF16)
        v_o[h] = vv[:, sl].astype(BF16)


def _small_specs(tm, npos_blocks):
    return [pl.BlockSpec((tm, Q_LORA), lambda i: (i, COL_QA // Q_LORA)),
            pl.BlockSpec((tm, KV_LORA), lambda i: (i, COL_KVA // KV_LORA)),
            pl.BlockSpec((tm, LANES), lambda i: (i, COL_KPE // LANES)),
            pl.BlockSpec((1, Q_LORA), lambda i: (0, 0)),
            pl.BlockSpec((1, KV_LORA), lambda i: (0, 0)),
            pl.BlockSpec((Q_LORA, H_C * LANES), lambda i: (0, 0))], \
           [pl.BlockSpec((tm, LANES), lambda i: (i % npos_blocks, 0))] * 4


def _mla_prep_prompt(proj, lw, tabs, seq):
    t = proj.shape[0]
    tm = min(seq, 512)
    head, tab = _small_specs(tm, seq // tm)
    wfull = pl.BlockSpec((KV_LORA, H_C * LANES), lambda i: (0, 0))
    row = lambda c: pl.BlockSpec((tm, c), lambda i: (i, 0))
    return pl.pallas_call(
        _mla_prep_prompt_kernel,
        grid=(t // tm,),
        in_specs=head + [wfull, wfull] + tab,
        out_specs=[pl.BlockSpec((H_C, tm, LANES), lambda i: (0, i, 0)),
                   pl.BlockSpec((H_C, LANES, tm), lambda i: (0, 0, i)),
                   pl.BlockSpec((H_C, tm, LANES), lambda i: (0, i, 0)), row(KV_LORA), row(LANES)],
        out_shape=[jax.ShapeDtypeStruct((H_C, t, LANES), BF16), jax.ShapeDtypeStruct((H_C, LANES, t), BF16),
                   jax.ShapeDtypeStruct((H_C, t, LANES), BF16)]
                  + [jax.ShapeDtypeStruct((t, KV_LORA), F32), jax.ShapeDtypeStruct((t, LANES), F32)],
        compiler_params=_cparams(("parallel",)),
    )(proj, proj, proj, lw["q_a_norm"], lw["kv_a_norm"], lw["wq"], lw["wk"], lw["wv"], *tabs)


def _mla_prep_sample_kernel(qa_ref, kva_ref, kpe_ref, qn_ref, kvn_ref, wq_ref, wuk_ref, sel_ref,
                            cq_ref, ck_ref, s1_ref, s2_ref, ql_o, qp_o, ckv_o, kr_o):
    q, ckv, kr = _mla_common(qa_ref, kva_ref, kpe_ref, qn_ref, kvn_ref, wq_ref, cq_ref, ck_ref,
                             s1_ref, s2_ref)
    ckv_o[...] = ckv
    kr_o[...] = kr
    qb = q.astype(BF16)
    for h in range(H_C):
        qh = qb[:, h * LANES:(h + 1) * LANES]
        ql_o[h] = jnp.dot(qh, wuk_ref[h], preferred_element_type=F32)
        qp_o[h] = jnp.dot(qh, sel_ref[...], preferred_element_type=F32)


def _mla_prep_sample(proj, lw, tabs):
    t = proj.shape[0]
    tm = min(t, 256)
    head, tab = _small_specs(tm, 1)
    row = lambda c: pl.BlockSpec((tm, c), lambda i: (i, 0))
    return pl.pallas_call(
        _mla_prep_sample_kernel,
        grid=(t // tm,),
        in_specs=head + [pl.BlockSpec((H_C, LANES, KV_LORA), lambda i: (0, 0, 0)),
                         pl.BlockSpec((LANES, LANES), lambda i: (0, 0))] + tab,
        out_specs=[pl.BlockSpec((H_C, tm, KV_LORA), lambda i: (0, i, 0)),
                   pl.BlockSpec((H_C, tm, LANES), lambda i: (0, i, 0)),
                   row(KV_LORA), row(LANES)],
        out_shape=[jax.ShapeDtypeStruct((H_C, t, KV_LORA), F32),
                   jax.ShapeDtypeStruct((H_C, t, LANES), F32),
                   jax.ShapeDtypeStruct((t, KV_LORA), F32), jax.ShapeDtypeStruct((t, LANES), F32)],
        compiler_params=_cparams(("parallel",)),
    )(proj, proj, proj, lw["q_a_norm"], lw["kv_a_norm"], lw["wq"], lw["wuk_abs"], lw["sel_rope"], *tabs)


NEG = -1e30


def _lane_fold(x, op):
    parts = [x[:, j * LANES:(j + 1) * LANES] for j in range(x.shape[1] // LANES)]
    while len(parts) > 1:
        parts = [op(parts[i], parts[i + 1]) for i in range(0, len(parts) - 1, 2)] + (
            [parts[-1]] if len(parts) % 2 else [])
    return parts[0]


def _flash_kernel(tq, sub, qi_ref, ki_ref, q_ref, kt_ref, v_ref, o_ref, m_ref, l_ref, acc_ref):
    step = pl.program_id(2)
    qi = qi_ref[step]
    ki = ki_ref[step]
    nsub = tq // sub

    @pl.when(ki == 0)
    def _():
        m_ref[...] = jnp.full(m_ref.shape, NEG, F32)
        l_ref[...] = jnp.zeros(l_ref.shape, F32)
        acc_ref[...] = jnp.zeros(acc_ref.shape, F32)

    def scores(r, diagonal):
        nk = (r + 1) * sub if diagonal else tq
        s = jnp.dot(q_ref[0, pl.ds(r * sub, sub), :], kt_ref[0, :, 0:nk], preferred_element_type=F32)
        if diagonal:
            qpos = r * sub + lax.broadcasted_iota(jnp.int32, s.shape, 0)
            kpos = lax.broadcasted_iota(jnp.int32, s.shape, 1)
            s = jnp.where(kpos <= qpos, s, NEG)
        return s

    def block(diagonal):
        s_next = scores(0, diagonal)
        for r in range(nsub):
            s = s_next
            if r + 1 < nsub:
                s_next = scores(r + 1, diagonal)
            rows = pl.ds(r * sub, sub)
            nk = s.shape[1]
            m_old = m_ref[rows, :]
            m_new = jnp.maximum(m_old, jnp.max(_lane_fold(s, jnp.maximum), axis=-1, keepdims=True))
            alpha = jnp.exp2(m_old - m_new)
            p = jnp.exp2(s - m_new)
            l_ref[rows, :] = alpha * l_ref[rows, :] + jnp.sum(_lane_fold(p, jnp.add), axis=-1,
                                                              keepdims=True)
            acc_ref[rows, :] = alpha * acc_ref[rows, :] + jnp.dot(
                p.astype(BF16), v_ref[0, 0:nk, :], preferred_element_type=F32)
            m_ref[rows, :] = m_new

    @pl.when(ki < qi)
    def _():
        block(False)

    @pl.when(ki == qi)
    def _():
        block(True)
        o_ref[0] = acc_ref[...] / l_ref[...]


def _flash_prompt(q, kt, v, nbatch, seq):
    tq = min(seq, 2048)
    nq = seq // tq
    pairs =[(a, c) for a in range(nq) for c in range(a + 1)]
    qi_tab = jnp.asarray([p[0] for p in pairs], jnp.int32)
    ki_tab = jnp.asarray([p[1] for p in pairs], jnp.int32)
    qspec = pl.BlockSpec((1, tq, LANES), lambda b, h, t, qt, kt: (h, b * nq + qt[t], 0))
    kspec = pl.BlockSpec((1, tq, LANES), lambda b, h, t, qt, kt: (h, b * nq + kt[t], 0))
    ktspec = pl.BlockSpec((1, LANES, tq), lambda b, h, t, qt, kt: (h, 0, b * nq + kt[t]))
    grid_spec = pltpu.PrefetchScalarGridSpec(
        num_scalar_prefetch=2,
        grid=(nbatch, H_C, len(pairs)),
        in_specs=[qspec, ktspec, kspec],
        out_specs=qspec,
        scratch_shapes=[pltpu.VMEM((tq, 1), F32), pltpu.VMEM((tq, 1), F32), pltpu.VMEM((tq, LANES), F32)],
    )
    return pl.pallas_call(
        functools.partial(_flash_kernel, tq, min(tq, 256)),
        grid_spec=grid_spec,
        out_shape=jax.ShapeDtypeStruct((H_C, nbatch * seq, LANES), F32),
        compiler_params=_cparams(("parallel", "parallel", "arbitrary")),
    )(qi_tab, ki_tab, q, kt, v)


PAGES_PER_STEP = 16


def _joint_softmax_update(tiles, values, m_ref, l_ref, acc_ref):
    def tree(xs, op):
        while len(xs) > 1:
            xs = [op(xs[i], xs[i + 1]) for i in range(0, len(xs) - 1, 2)] + (
                [xs[-1]] if len(xs) % 2 else [])
        return xs[0]

    m_old = m_ref[...]
    m_new = jnp.maximum(m_old, jnp.max(tree(list(tiles), jnp.maximum), axis=-1, keepdims=True))
    alpha = jnp.exp(m_old - m_new)
    ps = [jnp.exp(s - m_new) for s in tiles]
    pvs = [jnp.dot(p.astype(BF16), c, preferred_element_type=F32) for p, c in zip(ps, values)]
    l_ref[...] = alpha * l_ref[...] + jnp.sum(tree(ps, jnp.add), axis=-1, keepdims=True)
    acc_ref[...] = alpha * acc_ref[...] + tree(pvs, jnp.add)
    m_ref[...] = m_new


def _paged_kernel(snew, npg, pt_ref, ql_ref, qp_ref, *rest):
    ckv_refs = rest[:npg]
    kr_refs = rest[npg:2 * npg]
    cn_ref, kn_ref, wv_ref, o_ref, m_ref, l_ref, acc_ref = rest[2 * npg:]
    pg = pl.program_id(1)
    rows = H_C * snew
    ql = ql_ref[...].reshape(rows, KV_LORA).astype(BF16)
    qp = qp_ref[...].reshape(rows, LANES).astype(BF16)
    page = ckv_refs[0].shape[2]
    per_row = LANES // QK_ROPE
    er = lax.broadcasted_iota(jnp.int32, (page, page // per_row), 0)
    ec = lax.broadcasted_iota(jnp.int32, (page, page // per_row), 1)
    expand = ((er // per_row) == ec).astype(BF16)
    wr = lax.broadcasted_iota(jnp.int32, (page, LANES), 0)
    wc = lax.broadcasted_iota(jnp.int32, (page, LANES), 1)
    own_block = (wc // QK_ROPE) == (wr % per_row)


    @pl.when(pg == 0)
    def _():
        m_ref[...] = jnp.full(m_ref.shape, NEG, F32)
        l_ref[...] = jnp.zeros(l_ref.shape, F32)
        acc_ref[...] = jnp.zeros(acc_ref.shape, F32)

    def scores(cs, krs):
        lat = [_dot_nt(ql, c) for c in cs]
        rot = [_dot_nt(qp, r) for r in krs]
        return [a + b for a, b in zip(lat, rot)]

    cs = [ckv_refs[j][0, 0].astype(BF16) for j in range(npg)]
    packed = [kr_refs[j][0, 0].astype(BF16) for j in range(npg)]
    rows4 = [jnp.dot(expand, x, preferred_element_type=F32) for x in packed]
    krs = [jnp.where(own_block, x, 0.0).astype(BF16) for x in rows4]
    _joint_softmax_update(scores(cs, krs), cs, m_ref, l_ref, acc_ref)

    @pl.when(pg == pl.num_programs(1) - 1)
    def _():
        pad = lambda x: jnp.concatenate(
            [x, jnp.zeros((page - snew, x.shape[1]), F32)], axis=0).astype(BF16)
        c = pad(cn_ref[...])
        s = scores([c], [pad(kn_ref[...])])[0]
        tok = lax.broadcasted_iota(jnp.int32, s.shape, 0) % snew
        key = lax.broadcasted_iota(jnp.int32, s.shape, 1)
        s = jnp.where(key <= tok, s, NEG)
        _joint_softmax_update([s], [c], m_ref, l_ref, acc_ref)
        o_lat = acc_ref[...] / l_ref[...]
        for h in range(H_C):
            o_ref[:, h * V_HEAD:(h + 1) * V_HEAD] = jnp.dot(
                o_lat[h * snew:(h + 1) * snew].astype(BF16), wv_ref[h], preferred_element_type=F32)


def _paged_attention(ql, qp, cache_ckv, cache_kr, layer, ckv_new, kr_new, wv, page_table, snew):
    nb, n_pages = page_table.shape
    page = cache_ckv.shape[2]
    packed_rows = cache_kr.shape[2]
    npg = min(PAGES_PER_STEP, n_pages)
    ckv_specs = [pl.BlockSpec((1, 1, page, KV_LORA),
                              lambda b, g, pt, j=j: (layer, pt[b, g * npg + j], 0, 0)) for j in range(npg)]
    kr_specs = [pl.BlockSpec((1, 1, packed_rows, LANES),
                             lambda b, g, pt, j=j: (layer, pt[b, g * npg + j], 0, 0)) for j in range(npg)]
    rows = H_C * snew
    grid_spec = pltpu.PrefetchScalarGridSpec(
        num_scalar_prefetch=1,
        grid=(nb, n_pages // npg),
        in_specs=[pl.BlockSpec((H_C, snew, KV_LORA), lambda b, g, pt: (0, b, 0)),
                  pl.BlockSpec((H_C, snew, LANES), lambda b, g, pt: (0, b, 0))]
                 + ckv_specs + kr_specs
                 + [pl.BlockSpec((snew, KV_LORA), lambda b, g, pt: (b, 0)),
                    pl.BlockSpec((snew, LANES), lambda b, g, pt: (b, 0)),
                    pl.BlockSpec((H_C, KV_LORA, V_HEAD), lambda b, g, pt: (0, 0, 0))],
        out_specs=pl.BlockSpec((snew, D_BR), lambda b, g, pt: (b, 0)),
        scratch_shapes=[pltpu.VMEM((rows, 1), F32), pltpu.VMEM((rows, 1), F32),
                        pltpu.VMEM((rows, KV_LORA), F32)],
    )
    return pl.pallas_call(
        functools.partial(_paged_kernel, snew, npg),
        grid_spec=grid_spec,
        out_shape=jax.ShapeDtypeStruct((nb * snew, D_BR), F32),
        compiler_params=_cparams(("parallel", "arbitrary")),
    )(page_table, ql, qp, *([cache_ckv] * npg), *([cache_kr] * npg), ckv_new, kr_new, wv)


def _merge_kernel(nb, tq, y_grouped, o_grouped, y_ref, vkr_ref, bonus_ref, x_ref, o_ref, ga_ref, cb_ref, cc_ref, cx_ref, gb_ref,
                  gc_ref, ma_ref, mb_ref, mc_ref, hc_ref, hx_ref, lnw_ref, lnb_ref, cw_ref, np_ref,
                  wbr_ref, wout_ref, xo_ref, zt_ref):
    hm = _head_ones()
    tm = nb * tq
    if y_grouped:
        y = jnp.concatenate([y_ref[g] for g in range(N_LANE_GROUPS)], axis=1)
    else:
        y = y_ref[...] + vkr_ref[...]
    mu = _head_sum(y, hm) * (1.0 / HEAD_A)
    yc = y - mu
    var = _head_sum(yc * yc, hm) * (1.0 / HEAD_A)
    out_a = (yc * lax.rsqrt(var + GN_EPS) * lnw_ref[...] + lnb_ref[...] + bonus_ref[...]) * _silu(ga_ref[...])
    z = cc_ref[...] * cx_ref[...]
    zh = hc_ref[...] * hx_ref[...]
    z1 = _shift_rows(z, zh[:, 1:2, :], 1, nb, tq)
    z2 = _shift_rows(z, zh, 2, nb, tq)
    cw = cw_ref[...]
    conv = cw[0:1] * z2 + cw[1:2] * z1 + cw[2:3] * z
    out_b = cb_ref[...] * conv * _silu(gb_ref[...])
    if nb == 1:
        zt_ref[0] = z[tm - SUBLANES:, :]
    else:
        zt_ref[...] = z.reshape(nb, tq, D_BR)
    if o_grouped:
        o = jnp.concatenate([o_ref[h] for h in range(H_C)], axis=1)
    else:
        o = o_ref[...]
    out_c = o * _silu(gc_ref[...])
    mixed = (_sigmoid(ma_ref[...]) * jnp.dot(out_a.astype(BF16), wbr_ref[0], preferred_element_type=F32)
             + _sigmoid(mb_ref[...]) * jnp.dot(out_b.astype(BF16), wbr_ref[1], preferred_element_type=F32)
             + _sigmoid(mc_ref[...]) * jnp.dot(out_c.astype(BF16), wbr_ref[2], preferred_element_type=F32))
    res = jnp.dot(mixed.astype(BF16), wout_ref[...], preferred_element_type=F32)
    xo_ref[...] = x_ref[...] + _rms(res, np_ref[...])


def _merge(y, vkr, bonus, x, o, proj, halo_c, halo_x, lw, nb, tq):
    t = x.shape[0]
    y_grouped, o_grouped = y.ndim == 3, o.ndim == 3
    tm = nb * tq
    nseg = t // tq
    row = pl.BlockSpec((tm, D_BR), lambda i: (i, 0))
    prow = lambda cb: pl.BlockSpec((tm, D_BR), lambda i, cb=cb: (i, cb))
    halo = pl.BlockSpec((nb, 2, D_BR), lambda i: (i, 0, 0))
    vec = pl.BlockSpec((1, D_BR), lambda i: (0, 0))
    grouped = pl.BlockSpec((N_LANE_GROUPS, tm, LANES), lambda i: (0, i, 0))
    zt_rows = min(tq, SUBLANES)
    return pl.pallas_call(
        functools.partial(_merge_kernel, nb, tq, y_grouped, o_grouped),
        grid=(t // tm,),
        in_specs=[grouped if y_grouped else row, row, row, row, grouped if o_grouped else row] + [prow(c) for c in range(3, 12)] + [halo, halo, vec, vec,
                  pl.BlockSpec((3, D_BR), lambda i: (0, 0)), vec,
                  pl.BlockSpec((3, D_BR, D_MODEL), lambda i: (0, 0, 0)),
                  pl.BlockSpec((D_MODEL, D_MODEL), lambda i: (0, 0))],
        out_specs=[row, pl.BlockSpec((nb, zt_rows, D_BR), lambda i: (i, 0, 0))],
        out_shape=[jax.ShapeDtypeStruct((t, D_MODEL), F32),
                   jax.ShapeDtypeStruct((nseg, zt_rows, D_BR), F32)],
        compiler_params=_cparams(("parallel",)),
    )(y, vkr, bonus, x, o, *([proj] * 9), halo_c, halo_x, lw["ln_x_w"], lw["ln_x_b"], lw["conv_w"],
      lw["norm_post"], lw["w_branch"], lw["w_out"])


def _layer_weights(l, p):
    w_in = p["w_in"][l]
    z = lambda n: jnp.zeros((D_MODEL, n), F32)
    o_ga = 3 * D_BR + DECAY_LORA + ICLR_LORA
    o_qa = o_ga + 5 * D_BR
    o_kpe = o_qa + Q_LORA + KV_LORA
    o_gc = o_kpe + QK_ROPE
    w_re = jnp.concatenate([
        w_in[:, :3 * D_BR], w_in[:, o_ga:o_qa], w_in[:, o_gc:o_gc + 4 * D_BR],
        w_in[:, o_qa:o_kpe], w_in[:, 3 * D_BR:o_ga],
        z(ROPE_LANE0), w_in[:, o_kpe:o_gc], z(LANES - ROPE_LANE0 - QK_ROPE),
        z(N_PROJ - COL_KPE - LANES)], axis=1).astype(BF16)
    mu = p["mu_shift"][l]
    r2 = lambda a: a.reshape(1, -1)
    w_uq = p["w_uq"][l]
    wq = jnp.concatenate([w_uq, jnp.zeros((Q_LORA, H_C, LANES - QK_NOPE - QK_ROPE), F32)], axis=2)
    w_uk = p["w_uk"][l]
    wk = jnp.concatenate([w_uk, jnp.zeros((KV_LORA, H_C, LANES - QK_NOPE), F32)], axis=2)
    wuk_abs = jnp.concatenate([jnp.transpose(w_uk, (1, 2, 0)),
                               jnp.zeros((H_C, LANES - QK_NOPE, KV_LORA), F32)], axis=1)
    sel = np.zeros((LANES, LANES), np.float32)
    for blk in range(LANES // QK_ROPE):
        sel[ROPE_LANE0 + np.arange(QK_ROPE), blk * QK_ROPE + np.arange(QK_ROPE)] = 1.0
    zl = jnp.zeros((DECAY_LORA, D_BR), F32)
    return {
        "w_in": w_re, "norm_pre": r2(p["norm_pre"][l]), "norm_post": r2(p["norm_post"][l]),
        "mu_r": r2(mu[:D_BR]), "mu_k": r2(mu[D_BR:2 * D_BR]), "mu_v": r2(mu[2 * D_BR:3 * D_BR]),
        "mu_lora": r2(mu[3 * D_BR:]),
        "w0": r2(p["w0"][l]), "a0": r2(p["a0"][l]),
        "w2p": jnp.concatenate([p["w2"][l], zl], axis=0).astype(BF16),
        "a2p": jnp.concatenate([zl, p["a2"][l]], axis=0).astype(BF16),
        "k_k": r2(p["k_k"][l]), "k_a": r2(p["k_a"][l]), "r_k": r2(p["r_k"][l]),
        "ln_x_w": r2(p["ln_x_w"][l]), "ln_x_b": r2(p["ln_x_b"][l]), "conv_w": p["conv_w"][l],
        "q_a_norm": r2(p["q_a_norm"][l]), "kv_a_norm": r2(p["kv_a_norm"][l]),
        "wq": wq.reshape(Q_LORA, H_C * LANES).astype(BF16),
        "wk": wk.reshape(KV_LORA, H_C * LANES).astype(BF16),
        "wv": p["w_uv"][l].reshape(KV_LORA, H_C * V_HEAD).astype(BF16),
        "wuk_abs": wuk_abs.astype(BF16),
        "wv_heads": jnp.transpose(p["w_uv"][l], (1, 0, 2)).astype(BF16),
        "sel_rope": jnp.asarray(sel, BF16),
        "w_branch": p["w_branch"][l].astype(BF16), "w_out": p["w_out"][l].astype(BF16),
    }


def _rope_tables(pos):
    half = QK_ROPE // 2
    freqs = ROPE_THETA ** (-jnp.arange(half, dtype=F32) / half)
    ang = pos.astype(F32)[:, None] * freqs[None, :]
    cos, sin = jnp.cos(ang), jnp.sin(ang)
    n = pos.shape[0]
    lo = jnp.zeros((n, ROPE_LANE0), F32)
    hi = jnp.zeros((n, LANES - ROPE_LANE0 - QK_ROPE), F32)
    zh = jnp.zeros((n, half), F32)
    ck = jnp.concatenate([lo, cos, cos, hi], axis=1)
    cq = jnp.concatenate([lo + 1.0, cos, cos, hi], axis=1)
    s1 = jnp.concatenate([lo, zh, sin, hi], axis=1)
    s2 = jnp.concatenate([lo, -sin, zh, hi], axis=1)
    return cq, ck, s1, s2


def _prev_row_halos(proj, first_rows, nseq, tseq, tq):
    nseg_per = tseq // tq
    out = {}
    for name, (c0, c1) in {"r": (COL_R, COL_R + D_BR), "k": (COL_K, COL_K + D_BR),
                           "v": (COL_V, COL_V + D_BR), "lora": (COL_LORA, COL_LORA + LANES)}.items():
        first = first_rows[name].reshape(nseq, 1, c1 - c0)
        if nseg_per > 1:
            p3 = proj.reshape(nseq, tseq, N_PROJ)
            inner = p3[:, tq - 1:tseq - 1:tq, c0:c1]
            first = jnp.concatenate([first, inner], axis=1)
        out[name] = first.reshape(nseq * nseg_per, 1, c1 - c0)
    return out


def _conv_halos(proj, conv_prev, nseq, tseq, tq):
    nseg_per = tseq // tq
    hc = conv_prev
    hx = jnp.ones_like(conv_prev)
    if nseg_per > 1:
        p3 = proj.reshape(nseq, nseg_per, tq, N_PROJ)[:, :-1, tq - 2:, :]
        hc = jnp.concatenate([hc[:, None], p3[..., COL_CC:COL_CC + D_BR]], axis=1)
        hx = jnp.concatenate([hx[:, None], p3[..., COL_CX:COL_CX + D_BR]], axis=1)
    return hc.reshape(nseq * nseg_per, 2, D_BR), hx.reshape(nseq * nseg_per, 2, D_BR)


def _layer(x, lw, tabs, nseq, tseq, shift_prev, conv_prev, wkv_prev, attn):
    t = nseq * tseq
    proj = _inproj(x, lw["norm_pre"], lw["w_in"])
    if tseq >= ROW_TILE:
        nb, tq = 1, ROW_TILE
    else:
        nb, tq = min(nseq, ROW_TILE // tseq), tseq
    first = {"r": shift_prev[:, :D_BR], "k": shift_prev[:, D_BR:2 * D_BR],
             "v": shift_prev[:, 2 * D_BR:3 * D_BR], "lora": shift_prev[:, 3 * D_BR:]}
    chunked = tseq % WKV_CHUNK == 0
    ops = _rwkv_prep(proj, _prev_row_halos(proj, first, nseq, tseq, tq), lw, nb, tq, chunked)
    if chunked:
        bonus = ops[6]
        vkr = bonus
        w6 = wkv_prev.reshape(nseq, N_LANE_GROUPS, 2, HEAD_A, HEAD_A)
        s0 = jnp.zeros((nseq, N_LANE_GROUPS, 2, HEAD_A, 2, HEAD_A), F32)
        s0 = s0.at[:, :, 0, :, 0, :].set(w6[:, :, 0]).at[:, :, 1, :, 1, :].set(w6[:, :, 1])
        y, st = _wkv_chunked(ops[:6], s0.reshape(nseq, N_LANE_GROUPS, LANES, LANES), nseq, tseq)
        st6 = st.reshape(nseq, N_LANE_GROUPS, 2, HEAD_A, 2, HEAD_A)
        wkv_new = jnp.stack([st6[:, :, 0, :, 0, :], st6[:, :, 1, :, 1, :]], axis=2).reshape(
            nseq, H_A, HEAD_A, HEAD_A)
    else:
        vkr, bonus = ops[7], ops[8]
        s0 = jnp.transpose(wkv_prev, (0, 2, 1, 3)).reshape(nseq, HEAD_A, D_BR)
        y, st = _wkv_scan(ops[:7], s0, nseq, tseq)
        wkv_new = jnp.transpose(st.reshape(nseq, HEAD_A, H_A, HEAD_A), (0, 2, 1, 3))
    o, ckv, kr128 = attn(proj, lw, tabs)
    hc, hx = _conv_halos(proj, conv_prev, nseq, tseq, tq)
    x_new, zt = _merge(y, vkr, bonus, x, o, proj, hc, hx, lw, nb, tq)
    p3 = proj.reshape(nseq, tseq, N_PROJ)[:, -1, :]
    new_shift = jnp.concatenate([p3[:, :3 * D_BR], p3[:, COL_LORA:COL_LORA + LANES]], axis=1)
    new_conv = zt.reshape(nseq, tseq // tq, -1, D_BR)[:, -1, -2:, :]
    kr = kr128[:, ROPE_LANE0:ROPE_LANE0 + QK_ROPE]
    return x_new, (ckv.reshape(nseq, tseq, KV_LORA), kr.reshape(nseq, tseq, QK_ROPE), wkv_new,
                   new_shift, new_conv)


def kernel(x_prompt, x_sample, cache_ckv, cache_krope, state_wkv, state_shift, state_conv, page_table,
           norm_pre, norm_post, w_in, mu_shift, w0, w2, a0, a2, k_k, k_a, r_k, ln_x_w, ln_x_b,
           conv_w, q_a_norm, w_uq, kv_a_norm, w_uk, w_uv, w_branch, w_out):
    params = dict(norm_pre=norm_pre, norm_post=norm_post, w_in=w_in, mu_shift=mu_shift, w0=w0, w2=w2,
                  a0=a0, a2=a2, k_k=k_k, k_a=k_a, r_k=r_k.reshape(r_k.shape[0], -1), ln_x_w=ln_x_w,
                  ln_x_b=ln_x_b, conv_w=conv_w, q_a_norm=q_a_norm, w_uq=w_uq, kv_a_norm=kv_a_norm,
                  w_uk=w_uk, w_uv=w_uv, w_branch=w_branch, w_out=w_out)
    bp, sp, _ = x_prompt.shape
    bd, sd, _ = x_sample.shape
    depth = w_in.shape[0]
    past_len = page_table.shape[1] * cache_ckv.shape[2]
    tabs_p = _rope_tables(jnp.arange(sp))
    tm_s = min(bd * sd, 256)
    tabs_s = _rope_tables(past_len + (jnp.arange(tm_s) % sd))
    cache_krope = cache_krope.reshape(cache_krope.shape[0], cache_krope.shape[1],
                                      cache_krope.shape[2] * QK_ROPE // LANES, LANES)

    xp = x_prompt.reshape(bp * sp, D_MODEL)
    xs = x_sample.reshape(bd * sd, D_MODEL)
    acc_p, acc_s = [], []
    for l in range(depth):
        lw = _layer_weights(l, params)

        def attn_prompt(proj, lw, tabs):
            q, k, v, ckv, kr128 = _mla_prep_prompt(proj, lw, tabs, sp)
            return _flash_prompt(q, k, v, bp, sp), ckv, kr128

        def attn_sample(proj, lw, tabs, l=l):
            ql, qp, ckv, kr128 = _mla_prep_sample(proj, lw, tabs)
            o = _paged_attention(ql, qp, cache_ckv, cache_krope, l, ckv, kr128, lw["wv_heads"],
                                 page_table, sd)
            return o, ckv, kr128

        xp, st_p = _layer(xp, lw, tabs_p, bp, sp, jnp.zeros((bp, state_shift.shape[2]), F32),
                          jnp.zeros((bp, 2, D_BR), F32), jnp.zeros((bp, H_A, HEAD_A, HEAD_A), F32),
                          attn_prompt)
        xs, st_s = _layer(xs, lw, tabs_s, bd, sd, state_shift[l], state_conv[l], state_wkv[l],
                          attn_sample)
        acc_p.append(st_p)
        acc_s.append(st_s)
    outs_p = [jnp.stack([a[i] for a in acc_p], axis=0) for i in range(5)]
    outs_s = [jnp.stack([a[i] for a in acc_s], axis=0) for i in range(5)]
    return (xp.reshape(bp, sp, D_MODEL), xs.reshape(bd, sd, D_MODEL), *outs_p, *outs_s)
```

```python
import functools
import math

import numpy as np
import jax
import jax.numpy as jnp
from jax import lax
from jax.experimental import pallas as pl
from jax.experimental.pallas import tpu as pltpu

F32 = jnp.float32
BF16 = jnp.bfloat16

D_MODEL = 1024
D_BR = D_MODEL
HEAD_A = 64
H_A = D_BR // HEAD_A
DECAY_LORA = 64
ICLR_LORA = 64
GN_EPS = 64e-5
H_C = 8
QK_NOPE = 64
QK_ROPE = 32
V_HEAD = D_BR // H_C
Q_LORA = D_MODEL // 4
KV_LORA = D_MODEL // 4
ROPE_THETA = 10000.0
NORM_EPS = 1e-6
LANES = 128
SUBLANES = 8
N_LANE_GROUPS = D_BR // LANES
ROPE_LANE0 = 64
VMEM_LIMIT = 56 * 1024 * 1024
ROW_TILE = 256

COL_R, COL_K, COL_V, COL_GA, COL_CB, COL_CC, COL_CX, COL_GB, COL_GC, COL_MA, COL_MB, COL_MC = (
    i * D_BR for i in range(12))
COL_QA = 12 * D_BR
COL_KVA = COL_QA + Q_LORA
COL_LORA = COL_KVA + KV_LORA
COL_KPE = COL_LORA + LANES
N_PROJ = 13 * D_BR


def _cparams(sem):
    return pltpu.CompilerParams(dimension_semantics=sem, vmem_limit_bytes=VMEM_LIMIT)


def _head_ones():
    r = lax.broadcasted_iota(jnp.int32, (LANES, LANES), 0) // HEAD_A
    c = lax.broadcasted_iota(jnp.int32, (LANES, LANES), 1) // HEAD_A
    return (r == c).astype(BF16)


def _split_dot(x, w):
    hi = x.astype(BF16)
    lo = (x - hi.astype(F32)).astype(BF16)
    return (jnp.dot(hi, w, preferred_element_type=F32) + jnp.dot(lo, w, preferred_element_type=F32))


def _split_dot_rhs(w, x):
    hi = x.astype(BF16)
    lo = (x - hi.astype(F32)).astype(BF16)
    return (jnp.dot(w, hi, preferred_element_type=F32) + jnp.dot(w, lo, preferred_element_type=F32))


def _head_sum(x, hm):
    parts = [_split_dot(x[:, g * LANES:(g + 1) * LANES], hm) for g in range(x.shape[1] // LANES)]
    return jnp.concatenate(parts, axis=1)


def _shift_rows(p, halo, k, nb, tq):
    c = p.shape[1]
    prev = pltpu.roll(p, k, axis=0)
    pos = lax.broadcasted_iota(jnp.int32, p.shape, 0) % tq
    out = prev
    for j in range(k):
        hb = jnp.broadcast_to(halo[:, j:j + 1, :], (nb, tq, c)).reshape(nb * tq, c)
        out = jnp.where(pos == j, hb, out)
    return out


def _sigmoid(x):
    return 1.0 / (1.0 + jnp.exp(-x))


def _silu(x):
    return x * _sigmoid(x)


def _rms(x, g):
    return x * lax.rsqrt(jnp.mean(x * x, axis=-1, keepdims=True) + NORM_EPS) * g


def _inproj_kernel(x_ref, g_ref, w_ref, o_ref, h_ref):
    @pl.when(pl.program_id(1) == 0)
    def _():
        h_ref[...] = _rms(x_ref[...], g_ref[...]).astype(BF16)

    o_ref[...] = jnp.dot(h_ref[...], w_ref[...], preferred_element_type=F32)


def _inproj(x, g, w):
    t = x.shape[0]
    tm = min(t, 1024)
    tn = 1024
    return pl.pallas_call(
        _inproj_kernel,
        grid=(t // tm, N_PROJ // tn),
        in_specs=[pl.BlockSpec((tm, D_MODEL), lambda i, j: (i, 0)),
                  pl.BlockSpec((1, D_MODEL), lambda i, j: (0, 0)),
                  pl.BlockSpec((D_MODEL, tn), lambda i, j: (0, j))],
        out_specs=pl.BlockSpec((tm, tn), lambda i, j: (i, j)),
        out_shape=jax.ShapeDtypeStruct((t, N_PROJ), F32),
        scratch_shapes=[pltpu.VMEM((tm, D_MODEL), BF16)],
        compiler_params=_cparams(("parallel", "arbitrary")),
    )(x, g, w)


def _rwkv_prep_kernel(nb, tq, chunked, r_ref, k_ref, v_ref, lo_ref, hr_ref, hk_ref, hv_ref, hl_ref,
                      mur_ref, muk_ref, muv_ref, mul_ref, w0_ref, w2_ref, a0_ref, a2_ref,
                      kk_ref, ka_ref, rk_ref, *outs):
    hm = _head_ones()

    def mix(ref, halo_ref, mu_ref):
        p = ref[...]
        prev = _shift_rows(p, halo_ref[...], 1, nb, tq)
        return p + (prev - p) * mu_ref[...]

    r = mix(r_ref, hr_ref, mur_ref)
    k = mix(k_ref, hk_ref, muk_ref)
    v = mix(v_ref, hv_ref, muv_ref)
    lo = mix(lo_ref, hl_ref, mul_ref)
    wf = w0_ref[...] + jnp.dot(jnp.tanh(lo).astype(BF16), w2_ref[...], preferred_element_type=F32)
    log_decay = -math.exp(-0.5) * _sigmoid(wf)
    a = _sigmoid(a0_ref[...] + jnp.dot(lo.astype(BF16), a2_ref[...], preferred_element_type=F32))
    kk = k * kk_ref[...]
    kk = kk / jnp.maximum(jnp.sqrt(_head_sum(kk * kk, hm)), 1e-12)
    kmod = k * (1.0 + (a - 1.0) * ka_ref[...])
    b = kk * a
    bonus = _head_sum(r * kmod * rk_ref[...], hm) * v
    if chunked:
        *group_outs, bonus_o = outs
        bonus_o[...] = bonus
        for o_ref, val in zip(group_outs, (r, log_decay, kmod, v, -kk, b)):
            for g in range(N_LANE_GROUPS):
                o_ref[g] = val[:, g * LANES:(g + 1) * LANES]
        return
    wr_o, a_o, w_o, b_o, k_o, v_o, br_o, vkr_o, bonus_o = outs
    decay = jnp.exp(log_decay)
    wr_o[...] = decay * r
    a_o[...] = -kk
    w_o[...] = decay
    b_o[...] = b
    k_o[...] = kmod
    v_o[...] = v
    br_o[...] = _head_sum(b * r, hm)
    vkr_o[...] = _head_sum(kmod * r, hm) * v
    bonus_o[...] = bonus


def _rwkv_prep(proj, halos, lw, nb, tq, chunked):
    t = proj.shape[0]
    tm = nb * tq
    row_out = pl.BlockSpec((tm, D_BR), lambda i: (i, 0))
    if chunked:
        out_specs = [pl.BlockSpec((N_LANE_GROUPS, tm, LANES), lambda i: (0, i, 0))] * 6 + [row_out]
        outs = [jax.ShapeDtypeStruct((N_LANE_GROUPS, t, LANES), F32)] * 6 + [
            jax.ShapeDtypeStruct((t, D_BR), F32)]
    else:
        out_specs = [row_out] * 9
        outs = [jax.ShapeDtypeStruct((t, D_BR), F32)] * 9
    row = lambda cb: pl.BlockSpec((tm, D_BR), lambda i, cb=cb: (i, cb))
    halo = lambda c: pl.BlockSpec((nb, 1, c), lambda i: (i, 0, 0))
    vec = lambda c: pl.BlockSpec((1, c), lambda i: (0, 0))
    mat = pl.BlockSpec((LANES, D_BR), lambda i: (0, 0))
    return pl.pallas_call(
        functools.partial(_rwkv_prep_kernel, nb, tq, chunked),
        grid=(t // tm,),
        in_specs=[row(0), row(1), row(2),
                  pl.BlockSpec((tm, LANES), lambda i: (i, COL_LORA // LANES)),
                  halo(D_BR), halo(D_BR), halo(D_BR), halo(LANES),
                  vec(D_BR), vec(D_BR), vec(D_BR), vec(LANES),
                  vec(D_BR), mat, vec(D_BR), mat, vec(D_BR), vec(D_BR), vec(D_BR)],
        out_specs=out_specs,
        out_shape=outs,
        compiler_params=_cparams(("parallel",)),
    )(proj, proj, proj, proj, halos["r"], halos["k"], halos["v"], halos["lora"],
      lw["mu_r"], lw["mu_k"], lw["mu_v"], lw["mu_lora"], lw["w0"], lw["w2p"], lw["a0"], lw["a2p"],
      lw["k_k"], lw["k_a"], lw["r_k"])


def _wkv_kernel(ns, tb, wr_ref, a_ref, w_ref, b_ref, k_ref, v_ref, br_ref, s0_ref,
                y_ref, st_ref, p_ref):
    @pl.when(pl.program_id(1) == 0)
    def _():
        st_ref[...] = s0_ref[...]

    hm = _head_ones()
    ri = lax.broadcasted_iota(jnp.int32, (HEAD_A, LANES), 0)
    ci = lax.broadcasted_iota(jnp.int32, (HEAD_A, LANES), 1)
    diag = ((ci % HEAD_A) == ri).astype(F32)

    er = lax.broadcasted_iota(jnp.int32, (SUBLANES, HEAD_A), 0)
    ec = lax.broadcasted_iota(jnp.int32, (SUBLANES, HEAD_A), 1)
    tok_sum = ((ec // SUBLANES) == er).astype(BF16)

    def step(u, carry, base):
        t = base + u
        for s in range(ns):
            row = lambda ref: ref[s, pl.ds(t, 1), :]
            wr, a, w, b, k, v, br = (row(x) for x in (wr_ref, a_ref, w_ref, b_ref, k_ref, v_ref, br_ref))
            for g in range(N_LANE_GROUPS):
                sl = slice(g * LANES, (g + 1) * LANES)
                st = st_ref[s, :, sl]
                m1 = st * a[:, sl]
                m1_hi = m1.astype(BF16)
                m1_lo = (m1 - m1_hi.astype(F32)).astype(BF16)
                m2 = (st * wr[:, sl]).astype(BF16)
                m3 = (diag * v[:, sl]).astype(BF16)
                res = jnp.dot(jnp.concatenate([m1_hi, m1_lo, m2, m3], axis=0), hm,
                              preferred_element_type=F32)
                sa = res[0:HEAD_A] + res[HEAD_A:2 * HEAD_A]
                y2 = res[2 * HEAD_A:3 * HEAD_A]
                vc = res[3 * HEAD_A:4 * HEAD_A]
                st_ref[s, :, sl] = st * w[:, sl] + sa * b[:, sl] + vc * k[:, sl]
                yd = (y2 + sa * br[:, sl]) * diag
                p_ref[s, u, :, sl] = jnp.sum(yd.reshape(HEAD_A // SUBLANES, SUBLANES, LANES), axis=0)
        return carry

    def block8(t8, carry):
        base = pl.multiple_of(t8 * SUBLANES, SUBLANES)
        lax.fori_loop(0, SUBLANES, functools.partial(step, base=base), 0)
        for s in range(ns):
            y_ref[s, pl.ds(base, SUBLANES), :] = _split_dot_rhs(
                tok_sum, p_ref[s].reshape(SUBLANES * SUBLANES, D_BR))
        return carry

    lax.fori_loop(0, tb // SUBLANES, block8, 0)


def _wkv_scan(ops, s0, nseq, tseq):
    ns = 4 if nseq % 4 == 0 else 2
    tb = min(tseq, 128)
    ops3 = [o.reshape(nseq, tseq, D_BR) for o in ops]
    blk = pl.BlockSpec((ns, tb, D_BR), lambda i, j: (i, j, 0))
    sblk = pl.BlockSpec((ns, HEAD_A, D_BR), lambda i, j: (i, 0, 0))
    y, st = pl.pallas_call(
        functools.partial(_wkv_kernel, ns, tb),
        grid=(nseq // ns, tseq // tb),
        in_specs=[blk] * 7 + [sblk],
        out_specs=[blk, sblk],
        out_shape=[jax.ShapeDtypeStruct((nseq, tseq, D_BR), F32),
                   jax.ShapeDtypeStruct((nseq, HEAD_A, D_BR), F32)],
        scratch_shapes=[pltpu.VMEM((ns, SUBLANES, SUBLANES, D_BR), F32)],
        compiler_params=_cparams(("parallel", "arbitrary")),
    )(*ops3, s0)
    return y.reshape(nseq * tseq, D_BR), st


WKV_CHUNK = 64
WKV_PASSES_LOCAL = 1
WKV_PASSES_STATE = 3


def _dot_nn(a, b):
    return jnp.dot(a, b, preferred_element_type=F32)


def _dot_nt(a, b):
    return lax.dot_general(a, b, (((1,), (1,)), ((), ())), preferred_element_type=F32)


def _dot_tn(a, b):
    return lax.dot_general(a, b, (((0,), (0,)), ((), ())), preferred_element_type=F32)


def _hilo(x):
    hi = x.astype(BF16)
    return hi, (x - hi.astype(F32)).astype(BF16)


def _mm(dot, a, b, passes):
    ah, al = _hilo(a)
    bh, bl = _hilo(b)
    out = dot(ah, bh)
    if passes >= 3:
        out = out + dot(al, bh) + dot(ah, bl)
    return out


def _wkv_chunk_kernel(ns, r_ref, lw_ref, k_ref, v_ref, a_ref, b_ref, s0_ref, y_ref, st_ref):
    L = r_ref.shape[2]

    @pl.when(pl.program_id(1) == 0)
    def _():
        st_ref[...] = s0_ref[...]

    ti = lax.broadcasted_iota(jnp.int32, (L, L), 0)
    si = lax.broadcasted_iota(jnp.int32, (L, L), 1)
    incl = si <= ti
    strict = si < ti
    eye_l = (si == ti).astype(F32)
    tril_ones = incl.astype(BF16)
    levels = []
    m = 1
    while m < L:
        levels.append(((ti // (2 * m)) == (si // (2 * m))) & ((ti % (2 * m)) >= m) & ((si % (2 * m)) < m))
        m *= 2
    lane = lax.broadcasted_iota(jnp.int32, (1, LANES), 1)
    head_masks = [(lane < HEAD_A).astype(F32), (lane >= HEAD_A).astype(F32)]
    gr = lax.broadcasted_iota(jnp.int32, (LANES, LANES), 0)
    gc = lax.broadcasted_iota(jnp.int32, (LANES, LANES), 1)
    same_head = (gr // HEAD_A) == (gc // HEAD_A)
    eye_g = gr == gc

    def each(fn, *lists):
        return [fn(*xs) for xs in zip(*lists)]

    def mm(dot, xs, ys, passes=WKV_PASSES_LOCAL):
        return each(lambda x, y: _mm(dot, x, y, passes), xs, ys)

    chains = [(s, g) for s in range(ns) for g in range(N_LANE_GROUPS)]
    r, lw, k, v, a, b = ([ref[g, s] for s, g in chains]
                         for ref in (r_ref, lw_ref, k_ref, v_ref, a_ref, b_ref))
    st = [st_ref[s, g] for s, g in chains]

    def cumsum(x):
        hi, lo = _hilo(x)
        return _dot_nn(tril_ones, hi) + _dot_nn(tril_ones, lo)

    c = each(cumsum, lw)
    c_last = [x[L - 1:L, :] for x in c]
    e_neg = [jnp.exp(-x) for x in c]
    e_end = each(lambda cl, x: jnp.exp(cl - x), c_last, c)
    at = each(lambda a_, x, w_: a_ * jnp.exp(x - w_), a, c, lw)
    rt = each(lambda r_, x: r_ * jnp.exp(x), r, c)
    bt = each(jnp.multiply, b, e_neg)
    kt = each(jnp.multiply, k, e_neg)
    bend = each(jnp.multiply, b, e_end)
    kend = each(jnp.multiply, k, e_end)

    def per_head(xs):
        return [x * hm_ for x in xs for hm_ in head_masks]

    def per_group(xs):
        return [x for x in xs for _ in head_masks]

    at_h, rt_h, v_h = per_head(at), per_head(rt), per_head(v)
    q2 = each(lambda x, y: jnp.concatenate([x, y], axis=0), at_h, rt_h)
    gb = mm(_dot_nt, q2, per_group(bt))
    gk = mm(_dot_nt, q2, per_group(kt))
    n = [jnp.where(strict, x[:L], 0.0) for x in gb]
    pb = [jnp.where(incl, x[L:], 0.0) for x in gb]
    m_ = [jnp.where(strict, x[:L], 0.0) for x in gk]
    pk = [jnp.where(incl, x[L:], 0.0) for x in gk]
    t_inv = [eye_l + jnp.where(levels[0], x, 0.0) for x in n]
    for lev in levels[1:]:
        cm = [jnp.where(lev, x, 0.0) for x in n]
        t_inv = each(jnp.add, t_inv, mm(_dot_nn, mm(_dot_nn, t_inv, cm), t_inv))
    abar_h = mm(_dot_nn, t_inv, at_h)
    uv_h = mm(_dot_nn, mm(_dot_nn, t_inv, m_), v_h)
    rbar_h = mm(_dot_nn, pb, abar_h)
    yv_h = each(jnp.add, mm(_dot_nn, pb, uv_h), mm(_dot_nn, pk, v_h))

    def head_total(xs):
        return [xs[2 * i] + xs[2 * i + 1] for i in range(len(chains))]

    abar, uv, yv = head_total(abar_h), head_total(uv_h), head_total(yv_h)
    rbar = each(jnp.add, rt, head_total(rbar_h))
    phi = each(lambda x, cl: jnp.where(same_head, x, 0.0) + jnp.where(eye_g, jnp.exp(cl), 0.0),
               mm(_dot_tn, abar, bend), c_last)
    psi = each(lambda x, y: jnp.where(same_head, x + y, 0.0), mm(_dot_tn, uv, bend), mm(_dot_tn, v, kend))
    y = each(jnp.add, mm(_dot_nt, rbar, st, WKV_PASSES_STATE), yv)
    st_new = each(jnp.add, mm(_dot_nn, st, phi, WKV_PASSES_STATE), psi)
    for (s, g), y_, st_ in zip(chains, y, st_new):
        y_ref[g, s] = y_
        st_ref[s, g] = st_


def _wkv_chunked(ops, s0, nseq, tseq):
    ns = 2
    L = WKV_CHUNK
    ops4 = [o.reshape(N_LANE_GROUPS, nseq, tseq, LANES) for o in ops]
    blk = pl.BlockSpec((N_LANE_GROUPS, ns, L, LANES), lambda i, j: (0, i, j, 0))
    sblk = pl.BlockSpec((ns, N_LANE_GROUPS, LANES, LANES), lambda i, j: (i, 0, 0, 0))
    y, st = pl.pallas_call(
        functools.partial(_wkv_chunk_kernel, ns),
        grid=(nseq // ns, tseq // L),
        in_specs=[blk] * 6 + [sblk],
        out_specs=[blk, sblk],
        out_shape=[jax.ShapeDtypeStruct((N_LANE_GROUPS, nseq, tseq, LANES), F32),
                   jax.ShapeDtypeStruct((nseq, N_LANE_GROUPS, LANES, LANES), F32)],
        compiler_params=_cparams(("parallel", "arbitrary")),
    )(*ops4, s0)
    return y.reshape(N_LANE_GROUPS, nseq * tseq, LANES), st


def _rope(x, c_ref, s1_ref, s2_ref):
    parts = []
    for h in range(x.shape[1] // LANES):
        xh = x[:, h * LANES:(h + 1) * LANES]
        half = QK_ROPE // 2
        parts.append(xh * c_ref[...] + pltpu.roll(xh, half, axis=1) * s1_ref[...]
                     + pltpu.roll(xh, LANES - half, axis=1) * s2_ref[...])
    return parts[0] if len(parts) == 1 else jnp.concatenate(parts, axis=1)


SOFTMAX_SCALE = (QK_NOPE + QK_ROPE) ** -0.5
LOG2E = math.log2(math.e)


def _mla_common(qa_ref, kva_ref, kpe_ref, qn_ref, kvn_ref, wq_ref, cq_ref, ck_ref, s1_ref, s2_ref,
                scale=SOFTMAX_SCALE):
    qn = _rms(qa_ref[...], qn_ref[...]).astype(BF16)
    q = jnp.dot(qn, wq_ref[...], preferred_element_type=F32)
    q = _rope(q, cq_ref, s1_ref, s2_ref) * scale
    ckv = _rms(kva_ref[...], kvn_ref[...])
    kr = _rope(kpe_ref[...], ck_ref, s1_ref, s2_ref)
    return q, ckv, kr


def _mla_prep_prompt_kernel(qa_ref, kva_ref, kpe_ref, qn_ref, kvn_ref, wq_ref, wk_ref, wv_ref,
                            cq_ref, ck_ref, s1_ref, s2_ref, q_o, k_o, v_o, ckv_o, kr_o):
    q, ckv, kr = _mla_common(qa_ref, kva_ref, kpe_ref, qn_ref, kvn_ref, wq_ref, cq_ref, ck_ref,
                             s1_ref, s2_ref, scale=SOFTMAX_SCALE * LOG2E)
    ckv_o[...] = ckv
    kr_o[...] = kr
    cb = ckv.astype(BF16)
    kn = jnp.dot(cb, wk_ref[...], preferred_element_type=F32)
    vv = jnp.dot(cb, wv_ref[...], preferred_element_type=F32)
    for h in range(H_C):
        sl = slice(h * LANES, (h + 1) * LANES)
        q_o[h] = q[:, sl].astype(BF16)
        k_o[h] = (kn[:, sl] + kr).T.astype(BF16)
        v_o[h] = vv[:, sl].astype(BF16)


def _small_specs(tm, npos_blocks):
    return [pl.BlockSpec((tm, Q_LORA), lambda i: (i, COL_QA // Q_LORA)),
            pl.BlockSpec((tm, KV_LORA), lambda i: (i, COL_KVA // KV_LORA)),
            pl.BlockSpec((tm, LANES), lambda i: (i, COL_KPE // LANES)),
            pl.BlockSpec((1, Q_LORA), lambda i: (0, 0)),
            pl.BlockSpec((1, KV_LORA), lambda i: (0, 0)),
            pl.BlockSpec((Q_LORA, H_C * LANES), lambda i: (0, 0))], \
           [pl.BlockSpec((tm, LANES), lambda i: (i % npos_blocks, 0))] * 4


def _mla_prep_prompt(proj, lw, tabs, seq):
    t = proj.shape[0]
    tm = min(seq, 512)
    head, tab = _small_specs(tm, seq // tm)
    wfull = pl.BlockSpec((KV_LORA, H_C * LANES), lambda i: (0, 0))
    row = lambda c: pl.BlockSpec((tm, c), lambda i: (i, 0))
    return pl.pallas_call(
        _mla_prep_prompt_kernel,
        grid=(t // tm,),
        in_specs=head + [wfull, wfull] + tab,
        out_specs=[pl.BlockSpec((H_C, tm, LANES), lambda i: (0, i, 0)),
                   pl.BlockSpec((H_C, LANES, tm), lambda i: (0, 0, i)),
                   pl.BlockSpec((H_C, tm, LANES), lambda i: (0, i, 0)), row(KV_LORA), row(LANES)],
        out_shape=[jax.ShapeDtypeStruct((H_C, t, LANES), BF16), jax.ShapeDtypeStruct((H_C, LANES, t), BF16),
                   jax.ShapeDtypeStruct((H_C, t, LANES), BF16)]
                  + [jax.ShapeDtypeStruct((t, KV_LORA), F32), jax.ShapeDtypeStruct((t, LANES), F32)],
        compiler_params=_cparams(("parallel",)),
    )(proj, proj, proj, lw["q_a_norm"], lw["kv_a_norm"], lw["wq"], lw["wk"], lw["wv"], *tabs)


def _mla_prep_sample_kernel(qa_ref, kva_ref, kpe_ref, qn_ref, kvn_ref, wq_ref, wuk_ref, sel_ref,
                            cq_ref, ck_ref, s1_ref, s2_ref, ql_o, qp_o, ckv_o, kr_o):
    q, ckv, kr = _mla_common(qa_ref, kva_ref, kpe_ref, qn_ref, kvn_ref, wq_ref, cq_ref, ck_ref,
                             s1_ref, s2_ref)
    ckv_o[...] = ckv
    kr_o[...] = kr
    qb = q.astype(BF16)
    for h in range(H_C):
        qh = qb[:, h * LANES:(h + 1) * LANES]
        ql_o[h] = jnp.dot(qh, wuk_ref[h], preferred_element_type=F32)
        qp_o[h] = jnp.dot(qh, sel_ref[...], preferred_element_type=F32)


def _mla_prep_sample(proj, lw, tabs):
    t = proj.shape[0]
    tm = min(t, 256)
    head, tab = _small_specs(tm, 1)
    row = lambda c: pl.BlockSpec((tm, c), lambda i: (i, 0))
    return pl.pallas_call(
        _mla_prep_sample_kernel,
        grid=(t // tm,),
        in_specs=head + [pl.BlockSpec((H_C, LANES, KV_LORA), lambda i: (0, 0, 0)),
                         pl.BlockSpec((LANES, QK_ROPE), lambda i: (0, 0))] + tab,
        out_specs=[pl.BlockSpec((H_C, tm, KV_LORA), lambda i: (0, i, 0)),
                   pl.BlockSpec((H_C, tm, QK_ROPE), lambda i: (0, i, 0)),
                   row(KV_LORA), row(LANES)],
        out_shape=[jax.ShapeDtypeStruct((H_C, t, KV_LORA), F32),
                   jax.ShapeDtypeStruct((H_C, t, QK_ROPE), F32),
                   jax.ShapeDtypeStruct((t, KV_LORA), F32), jax.ShapeDtypeStruct((t, LANES), F32)],
        compiler_params=_cparams(("parallel",)),
    )(proj, proj, proj, lw["q_a_norm"], lw["kv_a_norm"], lw["wq"], lw["wuk_abs"], lw["sel_rope"], *tabs)


NEG = -1e30


def _lane_fold(x, op):
    parts = [x[:, j * LANES:(j + 1) * LANES] for j in range(x.shape[1] // LANES)]
    while len(parts) > 1:
        parts = [op(parts[i], parts[i + 1]) for i in range(0, len(parts) - 1, 2)] + (
            [parts[-1]] if len(parts) % 2 else [])
    return parts[0]


def _flash_kernel(tq, sub, qi_ref, ki_ref, q_ref, kt_ref, v_ref, o_ref, m_ref, l_ref, acc_ref):
    step = pl.program_id(2)
    qi = qi_ref[step]
    ki = ki_ref[step]
    nsub = tq // sub

    @pl.when(ki == 0)
    def _():
        m_ref[...] = jnp.full(m_ref.shape, NEG, F32)
        l_ref[...] = jnp.zeros(l_ref.shape, F32)
        acc_ref[...] = jnp.zeros(acc_ref.shape, F32)

    def scores(r, diagonal):
        nk = (r + 1) * sub if diagonal else tq
        s = jnp.dot(q_ref[0, pl.ds(r * sub, sub), :], kt_ref[0, :, 0:nk], preferred_element_type=F32)
        if diagonal:
            qpos = r * sub + lax.broadcasted_iota(jnp.int32, s.shape, 0)
            kpos = lax.broadcasted_iota(jnp.int32, s.shape, 1)
            s = jnp.where(kpos <= qpos, s, NEG)
        return s

    def block(diagonal):
        s_next = scores(0, diagonal)
        for r in range(nsub):
            s = s_next
            if r + 1 < nsub:
                s_next = scores(r + 1, diagonal)
            rows = pl.ds(r * sub, sub)
            nk = s.shape[1]
            m_old = m_ref[rows, :]
            m_new = jnp.maximum(m_old, jnp.max(_lane_fold(s, jnp.maximum), axis=-1, keepdims=True))
            alpha = jnp.exp2(m_old - m_new)
            p = jnp.exp2(s - m_new)
            l_ref[rows, :] = alpha * l_ref[rows, :] + jnp.sum(_lane_fold(p, jnp.add), axis=-1,
                                                              keepdims=True)
            acc_ref[rows, :] = alpha * acc_ref[rows, :] + jnp.dot(
                p.astype(BF16), v_ref[0, 0:nk, :], preferred_element_type=F32)
            m_ref[rows, :] = m_new

    @pl.when(ki < qi)
    def _():
        block(False)

    @pl.when(ki == qi)
    def _():
        block(True)
        o_ref[0] = acc_ref[...] / l_ref[...]


def _flash_prompt(q, kt, v, nbatch, seq):
    tq = min(seq, 2048)
    nq = seq // tq
    pairs =[(a, c) for a in range(nq) for c in range(a + 1)]
    qi_tab = jnp.asarray([p[0] for p in pairs], jnp.int32)
    ki_tab = jnp.asarray([p[1] for p in pairs], jnp.int32)
    qspec = pl.BlockSpec((1, tq, LANES), lambda b, h, t, qt, kt: (h, b * nq + qt[t], 0))
    kspec = pl.BlockSpec((1, tq, LANES), lambda b, h, t, qt, kt: (h, b * nq + kt[t], 0))
    ktspec = pl.BlockSpec((1, LANES, tq), lambda b, h, t, qt, kt: (h, 0, b * nq + kt[t]))
    grid_spec = pltpu.PrefetchScalarGridSpec(
        num_scalar_prefetch=2,
        grid=(nbatch, H_C, len(pairs)),
        in_specs=[qspec, ktspec, kspec],
        out_specs=qspec,
        scratch_shapes=[pltpu.VMEM((tq, 1), F32), pltpu.VMEM((tq, 1), F32), pltpu.VMEM((tq, LANES), F32)],
    )
    return pl.pallas_call(
        functools.partial(_flash_kernel, tq, min(tq, 256)),
        grid_spec=grid_spec,
        out_shape=jax.ShapeDtypeStruct((H_C, nbatch * seq, LANES), F32),
        compiler_params=_cparams(("parallel", "parallel", "arbitrary")),
    )(qi_tab, ki_tab, q, kt, v)


PAGES_PER_STEP = 16


def _joint_softmax_update(tiles, values, m_ref, l_ref, acc_ref):
    def tree(xs, op):
        while len(xs) > 1:
            xs = [op(xs[i], xs[i + 1]) for i in range(0, len(xs) - 1, 2)] + (
                [xs[-1]] if len(xs) % 2 else [])
        return xs[0]

    m_old = m_ref[...]
    m_new = jnp.maximum(m_old, jnp.max(tree(list(tiles), jnp.maximum), axis=-1, keepdims=True))
    alpha = jnp.exp(m_old - m_new)
    ps = [jnp.exp(s - m_new) for s in tiles]
    pvs = [jnp.dot(p.astype(BF16), c, preferred_element_type=F32) for p, c in zip(ps, values)]
    l_ref[...] = alpha * l_ref[...] + jnp.sum(tree(ps, jnp.add), axis=-1, keepdims=True)
    acc_ref[...] = alpha * acc_ref[...] + tree(pvs, jnp.add)
    m_ref[...] = m_new


def _paged_kernel(snew, npg, pt_ref, ql_ref, qp_ref, *rest):
    ckv_refs = rest[:npg]
    kr_refs = rest[npg:2 * npg]
    cn_ref, kn_ref, wv_ref, o_ref, m_ref, l_ref, acc_ref = rest[2 * npg:]
    pg = pl.program_id(1)
    rows = H_C * snew
    ql = ql_ref[...].reshape(rows, KV_LORA).astype(BF16)
    qp = qp_ref[...].reshape(rows, QK_ROPE).astype(BF16)
    page = ckv_refs[0].shape[2]

    @pl.when(pg == 0)
    def _():
        m_ref[...] = jnp.full(m_ref.shape, NEG, F32)
        l_ref[...] = jnp.zeros(l_ref.shape, F32)
        acc_ref[...] = jnp.zeros(acc_ref.shape, F32)

    def scores(cs, krts):
        lat = [_dot_nt(ql, c) for c in cs]
        rot = [_dot_nn(qp, r) for r in krts]
        return [a + b for a, b in zip(lat, rot)]

    cs = [ckv_refs[j][0, 0].astype(BF16) for j in range(npg)]
    krts = [kr_refs[j][0, 0].astype(BF16) for j in range(npg)]
    _joint_softmax_update(scores(cs, krts), cs, m_ref, l_ref, acc_ref)

    @pl.when(pg == pl.num_programs(1) - 1)
    def _():
        pad = lambda x: jnp.concatenate(
            [x, jnp.zeros((page - snew, x.shape[1]), F32)], axis=0).astype(BF16)
        c = pad(cn_ref[...])
        kn_t = pad(kn_ref[...]).T[ROPE_LANE0:ROPE_LANE0 + QK_ROPE, :]
        s = scores([c], [kn_t])[0]
        tok = lax.broadcasted_iota(jnp.int32, s.shape, 0) % snew
        key = lax.broadcasted_iota(jnp.int32, s.shape, 1)
        s = jnp.where(key <= tok, s, NEG)
        _joint_softmax_update([s], [c], m_ref, l_ref, acc_ref)
        o_lat = acc_ref[...] / l_ref[...]
        for h in range(H_C):
            o_ref[:, h * V_HEAD:(h + 1) * V_HEAD] = jnp.dot(
                o_lat[h * snew:(h + 1) * snew].astype(BF16), wv_ref[h], preferred_element_type=F32)


def _paged_attention(ql, qp, cache_ckv, cache_kr, layer, ckv_new, kr_new, wv, page_table, snew):
    nb, n_pages = page_table.shape
    page = cache_ckv.shape[2]
    npg = min(PAGES_PER_STEP, n_pages)
    ckv_specs = [pl.BlockSpec((1, 1, page, KV_LORA),
                              lambda b, g, pt, j=j: (layer, pt[b, g * npg + j], 0, 0)) for j in range(npg)]
    kr_specs = [pl.BlockSpec((1, 1, QK_ROPE, page),
                             lambda b, g, pt, j=j: (layer, pt[b, g * npg + j], 0, 0)) for j in range(npg)]
    rows = H_C * snew
    grid_spec = pltpu.PrefetchScalarGridSpec(
        num_scalar_prefetch=1,
        grid=(nb, n_pages // npg),
        in_specs=[pl.BlockSpec((H_C, snew, KV_LORA), lambda b, g, pt: (0, b, 0)),
                  pl.BlockSpec((H_C, snew, QK_ROPE), lambda b, g, pt: (0, b, 0))]
                 + ckv_specs + kr_specs
                 + [pl.BlockSpec((snew, KV_LORA), lambda b, g, pt: (b, 0)),
                    pl.BlockSpec((snew, LANES), lambda b, g, pt: (b, 0)),
                    pl.BlockSpec((H_C, KV_LORA, V_HEAD), lambda b, g, pt: (0, 0, 0))],
        out_specs=pl.BlockSpec((snew, D_BR), lambda b, g, pt: (b, 0)),
        scratch_shapes=[pltpu.VMEM((rows, 1), F32), pltpu.VMEM((rows, 1), F32),
                        pltpu.VMEM((rows, KV_LORA), F32)],
    )
    return pl.pallas_call(
        functools.partial(_paged_kernel, snew, npg),
        grid_spec=grid_spec,
        out_shape=jax.ShapeDtypeStruct((nb * snew, D_BR), F32),
        compiler_params=_cparams(("parallel", "arbitrary")),
    )(page_table, ql, qp, *([cache_ckv] * npg), *([cache_kr] * npg), ckv_new, kr_new, wv)


def _merge_kernel(nb, tq, y_grouped, o_grouped, y_ref, vkr_ref, bonus_ref, x_ref, o_ref, ga_ref, cb_ref, cc_ref, cx_ref, gb_ref,
                  gc_ref, ma_ref, mb_ref, mc_ref, hc_ref, hx_ref, lnw_ref, lnb_ref, cw_ref, np_ref,
                  wbr_ref, wout_ref, xo_ref, zt_ref):
    hm = _head_ones()
    tm = nb * tq
    if y_grouped:
        y = jnp.concatenate([y_ref[g] for g in range(N_LANE_GROUPS)], axis=1)
    else:
        y = y_ref[...] + vkr_ref[...]
    mu = _head_sum(y, hm) * (1.0 / HEAD_A)
    yc = y - mu
    var = _head_sum(yc * yc, hm) * (1.0 / HEAD_A)
    out_a = (yc * lax.rsqrt(var + GN_EPS) * lnw_ref[...] + lnb_ref[...] + bonus_ref[...]) * _silu(ga_ref[...])
    z = cc_ref[...] * cx_ref[...]
    zh = hc_ref[...] * hx_ref[...]
    z1 = _shift_rows(z, zh[:, 1:2, :], 1, nb, tq)
    z2 = _shift_rows(z, zh, 2, nb, tq)
    cw = cw_ref[...]
    conv = cw[0:1] * z2 + cw[1:2] * z1 + cw[2:3] * z
    out_b = cb_ref[...] * conv * _silu(gb_ref[...])
    if nb == 1:
        zt_ref[0] = z[tm - SUBLANES:, :]
    else:
        zt_ref[...] = z.reshape(nb, tq, D_BR)
    if o_grouped:
        o = jnp.concatenate([o_ref[h] for h in range(H_C)], axis=1)
    else:
        o = o_ref[...]
    out_c = o * _silu(gc_ref[...])
    mixed = (_sigmoid(ma_ref[...]) * jnp.dot(out_a.astype(BF16), wbr_ref[0], preferred_element_type=F32)
             + _sigmoid(mb_ref[...]) * jnp.dot(out_b.astype(BF16), wbr_ref[1], preferred_element_type=F32)
             + _sigmoid(mc_ref[...]) * jnp.dot(out_c.astype(BF16), wbr_ref[2], preferred_element_type=F32))
    res = jnp.dot(mixed.astype(BF16), wout_ref[...], preferred_element_type=F32)
    xo_ref[...] = x_ref[...] + _rms(res, np_ref[...])


def _merge(y, vkr, bonus, x, o, proj, halo_c, halo_x, lw, nb, tq):
    t = x.shape[0]
    y_grouped, o_grouped = y.ndim == 3, o.ndim == 3
    tm = nb * tq
    nseg = t // tq
    row = pl.BlockSpec((tm, D_BR), lambda i: (i, 0))
    prow = lambda cb: pl.BlockSpec((tm, D_BR), lambda i, cb=cb: (i, cb))
    halo = pl.BlockSpec((nb, 2, D_BR), lambda i: (i, 0, 0))
    vec = pl.BlockSpec((1, D_BR), lambda i: (0, 0))
    grouped = pl.BlockSpec((N_LANE_GROUPS, tm, LANES), lambda i: (0, i, 0))
    zt_rows = min(tq, SUBLANES)
    return pl.pallas_call(
        functools.partial(_merge_kernel, nb, tq, y_grouped, o_grouped),
        grid=(t // tm,),
        in_specs=[grouped if y_grouped else row, row, row, row, grouped if o_grouped else row] + [prow(c) for c in range(3, 12)] + [halo, halo, vec, vec,
                  pl.BlockSpec((3, D_BR), lambda i: (0, 0)), vec,
                  pl.BlockSpec((3, D_BR, D_MODEL), lambda i: (0, 0, 0)),
                  pl.BlockSpec((D_MODEL, D_MODEL), lambda i: (0, 0))],
        out_specs=[row, pl.BlockSpec((nb, zt_rows, D_BR), lambda i: (i, 0, 0))],
        out_shape=[jax.ShapeDtypeStruct((t, D_MODEL), F32),
                   jax.ShapeDtypeStruct((nseg, zt_rows, D_BR), F32)],
        compiler_params=_cparams(("parallel",)),
    )(y, vkr, bonus, x, o, *([proj] * 9), halo_c, halo_x, lw["ln_x_w"], lw["ln_x_b"], lw["conv_w"],
      lw["norm_post"], lw["w_branch"], lw["w_out"])


def _layer_weights(l, p):
    w_in = p["w_in"][l]
    z = lambda n: jnp.zeros((D_MODEL, n), F32)
    o_ga = 3 * D_BR + DECAY_LORA + ICLR_LORA
    o_qa = o_ga + 5 * D_BR
    o_kpe = o_qa + Q_LORA + KV_LORA
    o_gc = o_kpe + QK_ROPE
    w_re = jnp.concatenate([
        w_in[:, :3 * D_BR], w_in[:, o_ga:o_qa], w_in[:, o_gc:o_gc + 4 * D_BR],
        w_in[:, o_qa:o_kpe], w_in[:, 3 * D_BR:o_ga],
        z(ROPE_LANE0), w_in[:, o_kpe:o_gc], z(LANES - ROPE_LANE0 - QK_ROPE),
        z(N_PROJ - COL_KPE - LANES)], axis=1).astype(BF16)
    mu = p["mu_shift"][l]
    r2 = lambda a: a.reshape(1, -1)
    w_uq = p["w_uq"][l]
    wq = jnp.concatenate([w_uq, jnp.zeros((Q_LORA, H_C, LANES - QK_NOPE - QK_ROPE), F32)], axis=2)
    w_uk = p["w_uk"][l]
    wk = jnp.concatenate([w_uk, jnp.zeros((KV_LORA, H_C, LANES - QK_NOPE), F32)], axis=2)
    wuk_abs = jnp.concatenate([jnp.transpose(w_uk, (1, 2, 0)),
                               jnp.zeros((H_C, LANES - QK_NOPE, KV_LORA), F32)], axis=1)
    sel = np.zeros((LANES, QK_ROPE), np.float32)
    sel[ROPE_LANE0 + np.arange(QK_ROPE), np.arange(QK_ROPE)] = 1.0
    zl = jnp.zeros((DECAY_LORA, D_BR), F32)
    return {
        "w_in": w_re, "norm_pre": r2(p["norm_pre"][l]), "norm_post": r2(p["norm_post"][l]),
        "mu_r": r2(mu[:D_BR]), "mu_k": r2(mu[D_BR:2 * D_BR]), "mu_v": r2(mu[2 * D_BR:3 * D_BR]),
        "mu_lora": r2(mu[3 * D_BR:]),
        "w0": r2(p["w0"][l]), "a0": r2(p["a0"][l]),
        "w2p": jnp.concatenate([p["w2"][l], zl], axis=0).astype(BF16),
        "a2p": jnp.concatenate([zl, p["a2"][l]], axis=0).astype(BF16),
        "k_k": r2(p["k_k"][l]), "k_a": r2(p["k_a"][l]), "r_k": r2(p["r_k"][l]),
        "ln_x_w": r2(p["ln_x_w"][l]), "ln_x_b": r2(p["ln_x_b"][l]), "conv_w": p["conv_w"][l],
        "q_a_norm": r2(p["q_a_norm"][l]), "kv_a_norm": r2(p["kv_a_norm"][l]),
        "wq": wq.reshape(Q_LORA, H_C * LANES).astype(BF16),
        "wk": wk.reshape(KV_LORA, H_C * LANES).astype(BF16),
        "wv": p["w_uv"][l].reshape(KV_LORA, H_C * V_HEAD).astype(BF16),
        "wuk_abs": wuk_abs.astype(BF16),
        "wv_heads": jnp.transpose(p["w_uv"][l], (1, 0, 2)).astype(BF16),
        "sel_rope": jnp.asarray(sel, BF16),
        "w_branch": p["w_branch"][l].astype(BF16), "w_out": p["w_out"][l].astype(BF16),
    }


def _rope_tables(pos):
    half = QK_ROPE // 2
    freqs = ROPE_THETA ** (-jnp.arange(half, dtype=F32) / half)
    ang = pos.astype(F32)[:, None] * freqs[None, :]
    cos, sin = jnp.cos(ang), jnp.sin(ang)
    n = pos.shape[0]
    lo = jnp.zeros((n, ROPE_LANE0), F32)
    hi = jnp.zeros((n, LANES - ROPE_LANE0 - QK_ROPE), F32)
    zh = jnp.zeros((n, half), F32)
    ck = jnp.concatenate([lo, cos, cos, hi], axis=1)
    cq = jnp.concatenate([lo + 1.0, cos, cos, hi], axis=1)
    s1 = jnp.concatenate([lo, zh, sin, hi], axis=1)
    s2 = jnp.concatenate([lo, -sin, zh, hi], axis=1)
    return cq, ck, s1, s2


def _prev_row_halos(proj, first_rows, nseq, tseq, tq):
    nseg_per = tseq // tq
    out = {}
    for name, (c0, c1) in {"r": (COL_R, COL_R + D_BR), "k": (COL_K, COL_K + D_BR),
                           "v": (COL_V, COL_V + D_BR), "lora": (COL_LORA, COL_LORA + LANES)}.items():
        first = first_rows[name].reshape(nseq, 1, c1 - c0)
        if nseg_per > 1:
            p3 = proj.reshape(nseq, tseq, N_PROJ)
            inner = p3[:, tq - 1:tseq - 1:tq, c0:c1]
            first = jnp.concatenate([first, inner], axis=1)
        out[name] = first.reshape(nseq * nseg_per, 1, c1 - c0)
    return out


def _conv_halos(proj, conv_prev, nseq, tseq, tq):
    nseg_per = tseq // tq
    hc = conv_prev
    hx = jnp.ones_like(conv_prev)
    if nseg_per > 1:
        p3 = proj.reshape(nseq, nseg_per, tq, N_PROJ)[:, :-1, tq - 2:, :]
        hc = jnp.concatenate([hc[:, None], p3[..., COL_CC:COL_CC + D_BR]], axis=1)
        hx = jnp.concatenate([hx[:, None], p3[..., COL_CX:COL_CX + D_BR]], axis=1)
    return hc.reshape(nseq * nseg_per, 2, D_BR), hx.reshape(nseq * nseg_per, 2, D_BR)


def _layer(x, lw, tabs, nseq, tseq, shift_prev, conv_prev, wkv_prev, attn):
    t = nseq * tseq
    proj = _inproj(x, lw["norm_pre"], lw["w_in"])
    if tseq >= ROW_TILE:
        nb, tq = 1, ROW_TILE
    else:
        nb, tq = min(nseq, ROW_TILE // tseq), tseq
    first = {"r": shift_prev[:, :D_BR], "k": shift_prev[:, D_BR:2 * D_BR],
             "v": shift_prev[:, 2 * D_BR:3 * D_BR], "lora": shift_prev[:, 3 * D_BR:]}
    chunked = tseq % WKV_CHUNK == 0
    ops = _rwkv_prep(proj, _prev_row_halos(proj, first, nseq, tseq, tq), lw, nb, tq, chunked)
    if chunked:
        bonus = ops[6]
        vkr = bonus
        w6 = wkv_prev.reshape(nseq, N_LANE_GROUPS, 2, HEAD_A, HEAD_A)
        s0 = jnp.zeros((nseq, N_LANE_GROUPS, 2, HEAD_A, 2, HEAD_A), F32)
        s0 = s0.at[:, :, 0, :, 0, :].set(w6[:, :, 0]).at[:, :, 1, :, 1, :].set(w6[:, :, 1])
        y, st = _wkv_chunked(ops[:6], s0.reshape(nseq, N_LANE_GROUPS, LANES, LANES), nseq, tseq)
        st6 = st.reshape(nseq, N_LANE_GROUPS, 2, HEAD_A, 2, HEAD_A)
        wkv_new = jnp.stack([st6[:, :, 0, :, 0, :], st6[:, :, 1, :, 1, :]], axis=2).reshape(
            nseq, H_A, HEAD_A, HEAD_A)
    else:
        vkr, bonus = ops[7], ops[8]
        s0 = jnp.transpose(wkv_prev, (0, 2, 1, 3)).reshape(nseq, HEAD_A, D_BR)
        y, st = _wkv_scan(ops[:7], s0, nseq, tseq)
        wkv_new = jnp.transpose(st.reshape(nseq, HEAD_A, H_A, HEAD_A), (0, 2, 1, 3))
    o, ckv, kr128 = attn(proj, lw, tabs)
    hc, hx = _conv_halos(proj, conv_prev, nseq, tseq, tq)
    x_new, zt = _merge(y, vkr, bonus, x, o, proj, hc, hx, lw, nb, tq)
    p3 = proj.reshape(nseq, tseq, N_PROJ)[:, -1, :]
    new_shift = jnp.concatenate([p3[:, :3 * D_BR], p3[:, COL_LORA:COL_LORA + LANES]], axis=1)
    new_conv = zt.reshape(nseq, tseq // tq, -1, D_BR)[:, -1, -2:, :]
    kr = kr128[:, ROPE_LANE0:ROPE_LANE0 + QK_ROPE]
    return x_new, (ckv.reshape(nseq, tseq, KV_LORA), kr.reshape(nseq, tseq, QK_ROPE), wkv_new,
                   new_shift, new_conv)


def kernel(x_prompt, x_sample, cache_ckv, cache_krope, state_wkv, state_shift, state_conv, page_table,
           norm_pre, norm_post, w_in, mu_shift, w0, w2, a0, a2, k_k, k_a, r_k, ln_x_w, ln_x_b,
           conv_w, q_a_norm, w_uq, kv_a_norm, w_uk, w_uv, w_branch, w_out):
    params = dict(norm_pre=norm_pre, norm_post=norm_post, w_in=w_in, mu_shift=mu_shift, w0=w0, w2=w2,
                  a0=a0, a2=a2, k_k=k_k, k_a=k_a, r_k=r_k.reshape(r_k.shape[0], -1), ln_x_w=ln_x_w,
                  ln_x_b=ln_x_b, conv_w=conv_w, q_a_norm=q_a_norm, w_uq=w_uq, kv_a_norm=kv_a_norm,
                  w_uk=w_uk, w_uv=w_uv, w_branch=w_branch, w_out=w_out)
    bp, sp, _ = x_prompt.shape
    bd, sd, _ = x_sample.shape
    depth = w_in.shape[0]
    past_len = page_table.shape[1] * cache_ckv.shape[2]
    tabs_p = _rope_tables(jnp.arange(sp))
    tm_s = min(bd * sd, 256)
    tabs_s = _rope_tables(past_len + (jnp.arange(tm_s) % sd))
    cache_krope = jnp.swapaxes(cache_krope, 2, 3)

    xp = x_prompt.reshape(bp * sp, D_MODEL)
    xs = x_sample.reshape(bd * sd, D_MODEL)
    acc_p, acc_s = [], []
    for l in range(depth):
        lw = _layer_weights(l, params)

        def attn_prompt(proj, lw, tabs):
            q, k, v, ckv, kr128 = _mla_prep_prompt(proj, lw, tabs, sp)
            return _flash_prompt(q, k, v, bp, sp), ckv, kr128

        def attn_sample(proj, lw, tabs, l=l):
            ql, qp, ckv, kr128 = _mla_prep_sample(proj, lw, tabs)
            o = _paged_attention(ql, qp, cache_ckv, cache_krope, l, ckv, kr128, lw["wv_heads"],
                                 page_table, sd)
            return o, ckv, kr128

        xp, st_p = _layer(xp, lw, tabs_p, bp, sp, jnp.zeros((bp, state_shift.shape[2]), F32),
                          jnp.zeros((bp, 2, D_BR), F32), jnp.zeros((bp, H_A, HEAD_A, HEAD_A), F32),
                          attn_prompt)
        xs, st_s = _layer(xs, lw, tabs_s, bd, sd, state_shift[l], state_conv[l], state_wkv[l],
                          attn_sample)
        acc_p.append(st_p)
        acc_s.append(st_s)
    outs_p = [jnp.stack([a[i] for a in acc_p], axis=0) for i in range(5)]
    outs_s = [jnp.stack([a[i] for a in acc_s], axis=0) for i in range(5)]
    return (xp.reshape(bp, sp, D_MODEL), xs.reshape(bd, sd, D_MODEL), *outs_p, *outs_s)
```

```python
import functools
import math

import numpy as np
import jax
import jax.numpy as jnp
from jax import lax
from jax.experimental import pallas as pl
from jax.experimental.pallas import tpu as pltpu

F32 = jnp.float32
BF16 = jnp.bfloat16

D_MODEL = 1024
D_BR = D_MODEL
HEAD_A = 64
H_A = D_BR // HEAD_A
DECAY_LORA = 64
ICLR_LORA = 64
GN_EPS = 64e-5
H_C = 8
QK_NOPE = 64
QK_ROPE = 32
V_HEAD = D_BR // H_C
Q_LORA = D_MODEL // 4
KV_LORA = D_MODEL // 4
ROPE_THETA = 10000.0
NORM_EPS = 1e-6
LANES = 128
SUBLANES = 8
N_LANE_GROUPS = D_BR // LANES
ROPE_LANE0 = 64
VMEM_LIMIT = 56 * 1024 * 1024
ROW_TILE = 256

COL_R, COL_K, COL_V, COL_GA, COL_CB, COL_CC, COL_CX, COL_GB, COL_GC, COL_MA, COL_MB, COL_MC = (
    i * D_BR for i in range(12))
COL_QA = 12 * D_BR
COL_KVA = COL_QA + Q_LORA
COL_LORA = COL_KVA + KV_LORA
COL_KPE = COL_LORA + LANES
N_PROJ = 13 * D_BR


def _cparams(sem):
    return pltpu.CompilerParams(dimension_semantics=sem, vmem_limit_bytes=VMEM_LIMIT)


def _head_ones():
    r = lax.broadcasted_iota(jnp.int32, (LANES, LANES), 0) // HEAD_A
    c = lax.broadcasted_iota(jnp.int32, (LANES, LANES), 1) // HEAD_A
    return (r == c).astype(BF16)


def _split_dot(x, w):
    hi = x.astype(BF16)
    lo = (x - hi.astype(F32)).astype(BF16)
    return (jnp.dot(hi, w, preferred_element_type=F32) + jnp.dot(lo, w, preferred_element_type=F32))


def _split_dot_rhs(w, x):
    hi = x.astype(BF16)
    lo = (x - hi.astype(F32)).astype(BF16)
    return (jnp.dot(w, hi, preferred_element_type=F32) + jnp.dot(w, lo, preferred_element_type=F32))


def _head_sum(x, hm):
    parts = [_split_dot(x[:, g * LANES:(g + 1) * LANES], hm) for g in range(x.shape[1] // LANES)]
    return jnp.concatenate(parts, axis=1)


def _shift_rows(p, halo, k, nb, tq):
    c = p.shape[1]
    prev = pltpu.roll(p, k, axis=0)
    pos = lax.broadcasted_iota(jnp.int32, p.shape, 0) % tq
    out = prev
    for j in range(k):
        hb = jnp.broadcast_to(halo[:, j:j + 1, :], (nb, tq, c)).reshape(nb * tq, c)
        out = jnp.where(pos == j, hb, out)
    return out


def _sigmoid(x):
    return 1.0 / (1.0 + jnp.exp(-x))


def _silu(x):
    return x * _sigmoid(x)


def _rms(x, g):
    return x * lax.rsqrt(jnp.mean(x * x, axis=-1, keepdims=True) + NORM_EPS) * g


def _inproj_kernel(x_ref, g_ref, w_ref, o_ref, h_ref):
    @pl.when(pl.program_id(1) == 0)
    def _():
        h_ref[...] = _rms(x_ref[...], g_ref[...]).astype(BF16)

    o_ref[...] = jnp.dot(h_ref[...], w_ref[...], preferred_element_type=F32)


def _inproj(x, g, w):
    t = x.shape[0]
    tm = min(t, 1024)
    tn = 1024
    return pl.pallas_call(
        _inproj_kernel,
        grid=(t // tm, N_PROJ // tn),
        in_specs=[pl.BlockSpec((tm, D_MODEL), lambda i, j: (i, 0)),
                  pl.BlockSpec((1, D_MODEL), lambda i, j: (0, 0)),
                  pl.BlockSpec((D_MODEL, tn), lambda i, j: (0, j))],
        out_specs=pl.BlockSpec((tm, tn), lambda i, j: (i, j)),
        out_shape=jax.ShapeDtypeStruct((t, N_PROJ), F32),
        scratch_shapes=[pltpu.VMEM((tm, D_MODEL), BF16)],
        compiler_params=_cparams(("parallel", "arbitrary")),
    )(x, g, w)


def _rwkv_prep_kernel(nb, tq, chunked, r_ref, k_ref, v_ref, lo_ref, hr_ref, hk_ref, hv_ref, hl_ref,
                      mur_ref, muk_ref, muv_ref, mul_ref, w0_ref, w2_ref, a0_ref, a2_ref,
                      kk_ref, ka_ref, rk_ref, *outs):
    hm = _head_ones()

    def mix(ref, halo_ref, mu_ref):
        p = ref[...]
        prev = _shift_rows(p, halo_ref[...], 1, nb, tq)
        return p + (prev - p) * mu_ref[...]

    r = mix(r_ref, hr_ref, mur_ref)
    k = mix(k_ref, hk_ref, muk_ref)
    v = mix(v_ref, hv_ref, muv_ref)
    lo = mix(lo_ref, hl_ref, mul_ref)
    wf = w0_ref[...] + jnp.dot(jnp.tanh(lo).astype(BF16), w2_ref[...], preferred_element_type=F32)
    log_decay = -math.exp(-0.5) * _sigmoid(wf)
    a = _sigmoid(a0_ref[...] + jnp.dot(lo.astype(BF16), a2_ref[...], preferred_element_type=F32))
    kk = k * kk_ref[...]
    kk = kk / jnp.maximum(jnp.sqrt(_head_sum(kk * kk, hm)), 1e-12)
    kmod = k * (1.0 + (a - 1.0) * ka_ref[...])
    b = kk * a
    bonus = _head_sum(r * kmod * rk_ref[...], hm) * v
    if chunked:
        *group_outs, bonus_o = outs
        bonus_o[...] = bonus
        for o_ref, val in zip(group_outs, (r, log_decay, kmod, v, -kk, b)):
            for g in range(N_LANE_GROUPS):
                o_ref[g] = val[:, g * LANES:(g + 1) * LANES]
        return
    wr_o, a_o, w_o, b_o, k_o, v_o, br_o, vkr_o, bonus_o = outs
    decay = jnp.exp(log_decay)
    wr_o[...] = decay * r
    a_o[...] = -kk
    w_o[...] = decay
    b_o[...] = b
    k_o[...] = kmod
    v_o[...] = v
    br_o[...] = _head_sum(b * r, hm)
    vkr_o[...] = _head_sum(kmod * r, hm) * v
    bonus_o[...] = bonus


def _rwkv_prep(proj, halos, lw, nb, tq, chunked):
    t = proj.shape[0]
    tm = nb * tq
    row_out = pl.BlockSpec((tm, D_BR), lambda i: (i, 0))
    if chunked:
        out_specs = [pl.BlockSpec((N_LANE_GROUPS, tm, LANES), lambda i: (0, i, 0))] * 6 + [row_out]
        outs = [jax.ShapeDtypeStruct((N_LANE_GROUPS, t, LANES), F32)] * 6 + [
            jax.ShapeDtypeStruct((t, D_BR), F32)]
    else:
        out_specs = [row_out] * 9
        outs = [jax.ShapeDtypeStruct((t, D_BR), F32)] * 9
    row = lambda cb: pl.BlockSpec((tm, D_BR), lambda i, cb=cb: (i, cb))
    halo = lambda c: pl.BlockSpec((nb, 1, c), lambda i: (i, 0, 0))
    vec = lambda c: pl.BlockSpec((1, c), lambda i: (0, 0))
    mat = pl.BlockSpec((LANES, D_BR), lambda i: (0, 0))
    return pl.pallas_call(
        functools.partial(_rwkv_prep_kernel, nb, tq, chunked),
        grid=(t // tm,),
        in_specs=[row(0), row(1), row(2),
                  pl.BlockSpec((tm, LANES), lambda i: (i, COL_LORA // LANES)),
                  halo(D_BR), halo(D_BR), halo(D_BR), halo(LANES),
                  vec(D_BR), vec(D_BR), vec(D_BR), vec(LANES),
                  vec(D_BR), mat, vec(D_BR), mat, vec(D_BR), vec(D_BR), vec(D_BR)],
        out_specs=out_specs,
        out_shape=outs,
        compiler_params=_cparams(("parallel",)),
    )(proj, proj, proj, proj, halos["r"], halos["k"], halos["v"], halos["lora"],
      lw["mu_r"], lw["mu_k"], lw["mu_v"], lw["mu_lora"], lw["w0"], lw["w2p"], lw["a0"], lw["a2p"],
      lw["k_k"], lw["k_a"], lw["r_k"])


def _wkv_kernel(ns, tb, wr_ref, a_ref, w_ref, b_ref, k_ref, v_ref, br_ref, s0_ref,
                y_ref, st_ref, p_ref):
    @pl.when(pl.program_id(1) == 0)
    def _():
        st_ref[...] = s0_ref[...]

    hm = _head_ones()
    ri = lax.broadcasted_iota(jnp.int32, (HEAD_A, LANES), 0)
    ci = lax.broadcasted_iota(jnp.int32, (HEAD_A, LANES), 1)
    diag = ((ci % HEAD_A) == ri).astype(F32)

    er = lax.broadcasted_iota(jnp.int32, (SUBLANES, HEAD_A), 0)
    ec = lax.broadcasted_iota(jnp.int32, (SUBLANES, HEAD_A), 1)
    tok_sum = ((ec // SUBLANES) == er).astype(BF16)

    def step(u, carry, base):
        t = base + u
        for s in range(ns):
            row = lambda ref: ref[s, pl.ds(t, 1), :]
            wr, a, w, b, k, v, br = (row(x) for x in (wr_ref, a_ref, w_ref, b_ref, k_ref, v_ref, br_ref))
            for g in range(N_LANE_GROUPS):
                sl = slice(g * LANES, (g + 1) * LANES)
                st = st_ref[s, :, sl]
                m1 = st * a[:, sl]
                m1_hi = m1.astype(BF16)
                m1_lo = (m1 - m1_hi.astype(F32)).astype(BF16)
                m2 = (st * wr[:, sl]).astype(BF16)
                m3 = (diag * v[:, sl]).astype(BF16)
                res = jnp.dot(jnp.concatenate([m1_hi, m1_lo, m2, m3], axis=0), hm,
                              preferred_element_type=F32)
                sa = res[0:HEAD_A] + res[HEAD_A:2 * HEAD_A]
                y2 = res[2 * HEAD_A:3 * HEAD_A]
                vc = res[3 * HEAD_A:4 * HEAD_A]
                st_ref[s, :, sl] = st * w[:, sl] + sa * b[:, sl] + vc * k[:, sl]
                yd = (y2 + sa * br[:, sl]) * diag
                p_ref[s, u, :, sl] = jnp.sum(yd.reshape(HEAD_A // SUBLANES, SUBLANES, LANES), axis=0)
        return carry

    def block8(t8, carry):
        base = pl.multiple_of(t8 * SUBLANES, SUBLANES)
        lax.fori_loop(0, SUBLANES, functools.partial(step, base=base), 0)
        for s in range(ns):
            y_ref[s, pl.ds(base, SUBLANES), :] = _split_dot_rhs(
                tok_sum, p_ref[s].reshape(SUBLANES * SUBLANES, D_BR))
        return carry

    lax.fori_loop(0, tb // SUBLANES, block8, 0)


def _wkv_scan(ops, s0, nseq, tseq):
    ns = 8 if nseq % 8 == 0 else 2
    tb = min(tseq, 128)
    ops3 = [o.reshape(nseq, tseq, D_BR) for o in ops]
    blk = pl.BlockSpec((ns, tb, D_BR), lambda i, j: (i, j, 0))
    sblk = pl.BlockSpec((ns, HEAD_A, D_BR), lambda i, j: (i, 0, 0))
    y, st = pl.pallas_call(
        functools.partial(_wkv_kernel, ns, tb),
        grid=(nseq // ns, tseq // tb),
        in_specs=[blk] * 7 + [sblk],
        out_specs=[blk, sblk],
        out_shape=[jax.ShapeDtypeStruct((nseq, tseq, D_BR), F32),
                   jax.ShapeDtypeStruct((nseq, HEAD_A, D_BR), F32)],
        scratch_shapes=[pltpu.VMEM((ns, SUBLANES, SUBLANES, D_BR), F32)],
        compiler_params=_cparams(("parallel", "arbitrary")),
    )(*ops3, s0)
    return y.reshape(nseq * tseq, D_BR), st


WKV_CHUNK = 64
WKV_PASSES_LOCAL = 1
WKV_PASSES_STATE = 3


def _dot_nn(a, b):
    return jnp.dot(a, b, preferred_element_type=F32)


def _dot_nt(a, b):
    return lax.dot_general(a, b, (((1,), (1,)), ((), ())), preferred_element_type=F32)


def _dot_tn(a, b):
    return lax.dot_general(a, b, (((0,), (0,)), ((), ())), preferred_element_type=F32)


def _hilo(x):
    hi = x.astype(BF16)
    return hi, (x - hi.astype(F32)).astype(BF16)


def _mm(dot, a, b, passes):
    ah, al = _hilo(a)
    bh, bl = _hilo(b)
    out = dot(ah, bh)
    if passes >= 3:
        out = out + dot(al, bh) + dot(ah, bl)
    return out


def _wkv_chunk_kernel(ns, r_ref, lw_ref, k_ref, v_ref, a_ref, b_ref, s0_ref, y_ref, st_ref):
    L = r_ref.shape[2]

    @pl.when(pl.program_id(1) == 0)
    def _():
        st_ref[...] = s0_ref[...]

    ti = lax.broadcasted_iota(jnp.int32, (L, L), 0)
    si = lax.broadcasted_iota(jnp.int32, (L, L), 1)
    incl = si <= ti
    strict = si < ti
    eye_l = (si == ti).astype(F32)
    tril_ones = incl.astype(BF16)
    levels = []
    m = 1
    while m < L:
        levels.append(((ti // (2 * m)) == (si // (2 * m))) & ((ti % (2 * m)) >= m) & ((si % (2 * m)) < m))
        m *= 2
    lane = lax.broadcasted_iota(jnp.int32, (1, LANES), 1)
    head_masks = [(lane < HEAD_A).astype(F32), (lane >= HEAD_A).astype(F32)]
    gr = lax.broadcasted_iota(jnp.int32, (LANES, LANES), 0)
    gc = lax.broadcasted_iota(jnp.int32, (LANES, LANES), 1)
    same_head = (gr // HEAD_A) == (gc // HEAD_A)
    eye_g = gr == gc

    def each(fn, *lists):
        return [fn(*xs) for xs in zip(*lists)]

    def mm(dot, xs, ys, passes=WKV_PASSES_LOCAL):
        return each(lambda x, y: _mm(dot, x, y, passes), xs, ys)

    chains = [(s, g) for s in range(ns) for g in range(N_LANE_GROUPS)]
    r, lw, k, v, a, b = ([ref[g, s] for s, g in chains]
                         for ref in (r_ref, lw_ref, k_ref, v_ref, a_ref, b_ref))
    st = [st_ref[s, g] for s, g in chains]

    def cumsum(x):
        hi, lo = _hilo(x)
        return _dot_nn(tril_ones, hi) + _dot_nn(tril_ones, lo)

    c = each(cumsum, lw)
    c_last = [x[L - 1:L, :] for x in c]
    e_neg = [jnp.exp(-x) for x in c]
    e_end = each(lambda cl, x: jnp.exp(cl - x), c_last, c)
    at = each(lambda a_, x, w_: a_ * jnp.exp(x - w_), a, c, lw)
    rt = each(lambda r_, x: r_ * jnp.exp(x), r, c)
    bt = each(jnp.multiply, b, e_neg)
    kt = each(jnp.multiply, k, e_neg)
    bend = each(jnp.multiply, b, e_end)
    kend = each(jnp.multiply, k, e_end)

    def per_head(xs):
        return [x * hm_ for x in xs for hm_ in head_masks]

    def per_group(xs):
        return [x for x in xs for _ in head_masks]

    at_h, rt_h, v_h = per_head(at), per_head(rt), per_head(v)
    q2 = each(lambda x, y: jnp.concatenate([x, y], axis=0), at_h, rt_h)
    gb = mm(_dot_nt, q2, per_group(bt))
    gk = mm(_dot_nt, q2, per_group(kt))
    n = [jnp.where(strict, x[:L], 0.0) for x in gb]
    pb = [jnp.where(incl, x[L:], 0.0) for x in gb]
    m_ = [jnp.where(strict, x[:L], 0.0) for x in gk]
    pk = [jnp.where(incl, x[L:], 0.0) for x in gk]
    t_inv = [eye_l + jnp.where(levels[0], x, 0.0) for x in n]
    for lev in levels[1:]:
        cm = [jnp.where(lev, x, 0.0) for x in n]
        t_inv = each(jnp.add, t_inv, mm(_dot_nn, mm(_dot_nn, t_inv, cm), t_inv))
    abar_h = mm(_dot_nn, t_inv, at_h)
    uv_h = mm(_dot_nn, mm(_dot_nn, t_inv, m_), v_h)
    rbar_h = mm(_dot_nn, pb, abar_h)
    yv_h = each(jnp.add, mm(_dot_nn, pb, uv_h), mm(_dot_nn, pk, v_h))

    def head_total(xs):
        return [xs[2 * i] + xs[2 * i + 1] for i in range(len(chains))]

    abar, uv, yv = head_total(abar_h), head_total(uv_h), head_total(yv_h)
    rbar = each(jnp.add, rt, head_total(rbar_h))
    phi = each(lambda x, cl: jnp.where(same_head, x, 0.0) + jnp.where(eye_g, jnp.exp(cl), 0.0),
               mm(_dot_tn, abar, bend), c_last)
    psi = each(lambda x, y: jnp.where(same_head, x + y, 0.0), mm(_dot_tn, uv, bend), mm(_dot_tn, v, kend))
    y = each(jnp.add, mm(_dot_nt, rbar, st, WKV_PASSES_STATE), yv)
    st_new = each(jnp.add, mm(_dot_nn, st, phi, WKV_PASSES_STATE), psi)
    for (s, g), y_, st_ in zip(chains, y, st_new):
        y_ref[g, s] = y_
        st_ref[s, g] = st_


def _wkv_chunked(ops, s0, nseq, tseq):
    ns = 2
    L = WKV_CHUNK
    ops4 = [o.reshape(N_LANE_GROUPS, nseq, tseq, LANES) for o in ops]
    blk = pl.BlockSpec((N_LANE_GROUPS, ns, L, LANES), lambda i, j: (0, i, j, 0))
    sblk = pl.BlockSpec((ns, N_LANE_GROUPS, LANES, LANES), lambda i, j: (i, 0, 0, 0))
    y, st = pl.pallas_call(
        functools.partial(_wkv_chunk_kernel, ns),
        grid=(nseq // ns, tseq // L),
        in_specs=[blk] * 6 + [sblk],
        out_specs=[blk, sblk],
        out_shape=[jax.ShapeDtypeStruct((N_LANE_GROUPS, nseq, tseq, LANES), F32),
                   jax.ShapeDtypeStruct((nseq, N_LANE_GROUPS, LANES, LANES), F32)],
        compiler_params=_cparams(("parallel", "arbitrary")),
    )(*ops4, s0)
    return y.reshape(N_LANE_GROUPS, nseq * tseq, LANES), st


def _rope(x, c_ref, s1_ref, s2_ref):
    parts = []
    for h in range(x.shape[1] // LANES):
        xh = x[:, h * LANES:(h + 1) * LANES]
        half = QK_ROPE // 2
        parts.append(xh * c_ref[...] + pltpu.roll(xh, half, axis=1) * s1_ref[...]
                     + pltpu.roll(xh, LANES - half, axis=1) * s2_ref[...])
    return parts[0] if len(parts) == 1 else jnp.concatenate(parts, axis=1)


SOFTMAX_SCALE = (QK_NOPE + QK_ROPE) ** -0.5
LOG2E = math.log2(math.e)


def _mla_common(qa_ref, kva_ref, kpe_ref, qn_ref, kvn_ref, wq_ref, cq_ref, ck_ref, s1_ref, s2_ref,
                scale=SOFTMAX_SCALE):
    qn = _rms(qa_ref[...], qn_ref[...]).astype(BF16)
    q = jnp.dot(qn, wq_ref[...], preferred_element_type=F32)
    q = _rope(q, cq_ref, s1_ref, s2_ref) * scale
    ckv = _rms(kva_ref[...], kvn_ref[...])
    kr = _rope(kpe_ref[...], ck_ref, s1_ref, s2_ref)
    return q, ckv, kr


def _mla_prep_prompt_kernel(qa_ref, kva_ref, kpe_ref, qn_ref, kvn_ref, wq_ref, wk_ref, wv_ref,
                            cq_ref, ck_ref, s1_ref, s2_ref, q_o, k_o, v_o, ckv_o, kr_o):
    q, ckv, kr = _mla_common(qa_ref, kva_ref, kpe_ref, qn_ref, kvn_ref, wq_ref, cq_ref, ck_ref,
                             s1_ref, s2_ref, scale=SOFTMAX_SCALE * LOG2E)
    ckv_o[...] = ckv
    kr_o[...] = kr
    cb = ckv.astype(BF16)
    kn = jnp.dot(cb, wk_ref[...], preferred_element_type=F32)
    vv = jnp.dot(cb, wv_ref[...], preferred_element_type=F32)
    for h in range(H_C):
        sl = slice(h * LANES, (h + 1) * LANES)
        q_o[h] = q[:, sl].astype(BF16)
        k_o[h] = (kn[:, sl] + kr).T.astype(BF16)
        v_o[h] = vv[:, sl].astype(BF16)


def _small_specs(tm, npos_blocks):
    return [pl.BlockSpec((tm, Q_LORA), lambda i: (i, COL_QA // Q_LORA)),
            pl.BlockSpec((tm, KV_LORA), lambda i: (i, COL_KVA // KV_LORA)),
            pl.BlockSpec((tm, LANES), lambda i: (i, COL_KPE // LANES)),
            pl.BlockSpec((1, Q_LORA), lambda i: (0, 0)),
            pl.BlockSpec((1, KV_LORA), lambda i: (0, 0)),
            pl.BlockSpec((Q_LORA, H_C * LANES), lambda i: (0, 0))], \
           [pl.BlockSpec((tm, LANES), lambda i: (i % npos_blocks, 0))] * 4


def _mla_prep_prompt(proj, lw, tabs, seq):
    t = proj.shape[0]
    tm = min(seq, 512)
    head, tab = _small_specs(tm, seq // tm)
    wfull = pl.BlockSpec((KV_LORA, H_C * LANES), lambda i: (0, 0))
    row = lambda c: pl.BlockSpec((tm, c), lambda i: (i, 0))
    return pl.pallas_call(
        _mla_prep_prompt_kernel,
        grid=(t // tm,),
        in_specs=head + [wfull, wfull] + tab,
        out_specs=[pl.BlockSpec((H_C, tm, LANES), lambda i: (0, i, 0)),
                   pl.BlockSpec((H_C, LANES, tm), lambda i: (0, 0, i)),
                   pl.BlockSpec((H_C, tm, LANES), lambda i: (0, i, 0)), row(KV_LORA), row(LANES)],
        out_shape=[jax.ShapeDtypeStruct((H_C, t, LANES), BF16), jax.ShapeDtypeStruct((H_C, LANES, t), BF16),
                   jax.ShapeDtypeStruct((H_C, t, LANES), BF16)]
                  + [jax.ShapeDtypeStruct((t, KV_LORA), F32), jax.ShapeDtypeStruct((t, LANES), F32)],
        compiler_params=_cparams(("parallel",)),
    )(proj, proj, proj, lw["q_a_norm"], lw["kv_a_norm"], lw["wq"], lw["wk"], lw["wv"], *tabs)


def _mla_prep_sample_kernel(qa_ref, kva_ref, kpe_ref, qn_ref, kvn_ref, wq_ref, wuk_ref, sel_ref,
                            cq_ref, ck_ref, s1_ref, s2_ref, ql_o, qp_o, ckv_o, kr_o):
    q, ckv, kr = _mla_common(qa_ref, kva_ref, kpe_ref, qn_ref, kvn_ref, wq_ref, cq_ref, ck_ref,
                             s1_ref, s2_ref)
    ckv_o[...] = ckv
    kr_o[...] = kr
    qb = q.astype(BF16)
    for h in range(H_C):
        qh = qb[:, h * LANES:(h + 1) * LANES]
        ql_o[h] = jnp.dot(qh, wuk_ref[h], preferred_element_type=F32)
        qp_o[h] = jnp.dot(qh, sel_ref[...], preferred_element_type=F32)


def _mla_prep_sample(proj, lw, tabs):
    t = proj.shape[0]
    tm = min(t, 256)
    head, tab = _small_specs(tm, 1)
    row = lambda c: pl.BlockSpec((tm, c), lambda i: (i, 0))
    return pl.pallas_call(
        _mla_prep_sample_kernel,
        grid=(t // tm,),
        in_specs=head + [pl.BlockSpec((H_C, LANES, KV_LORA), lambda i: (0, 0, 0)),
                         pl.BlockSpec((LANES, QK_ROPE), lambda i: (0, 0))] + tab,
        out_specs=[pl.BlockSpec((H_C, tm, KV_LORA), lambda i: (0, i, 0)),
                   pl.BlockSpec((H_C, tm, QK_ROPE), lambda i: (0, i, 0)),
                   row(KV_LORA), row(LANES)],
        out_shape=[jax.ShapeDtypeStruct((H_C, t, KV_LORA), F32),
                   jax.ShapeDtypeStruct((H_C, t, QK_ROPE), F32),
                   jax.ShapeDtypeStruct((t, KV_LORA), F32), jax.ShapeDtypeStruct((t, LANES), F32)],
        compiler_params=_cparams(("parallel",)),
    )(proj, proj, proj, lw["q_a_norm"], lw["kv_a_norm"], lw["wq"], lw["wuk_abs"], lw["sel_rope"], *tabs)


NEG = -1e30


def _lane_fold(x, op):
    parts = [x[:, j * LANES:(j + 1) * LANES] for j in range(x.shape[1] // LANES)]
    while len(parts) > 1:
        parts = [op(parts[i], parts[i + 1]) for i in range(0, len(parts) - 1, 2)] + (
            [parts[-1]] if len(parts) % 2 else [])
    return parts[0]


def _flash_kernel(tq, sub, qi_ref, ki_ref, q_ref, kt_ref, v_ref, o_ref, m_ref, l_ref, acc_ref):
    step = pl.program_id(2)
    qi = qi_ref[step]
    ki = ki_ref[step]
    nsub = tq // sub

    @pl.when(ki == 0)
    def _():
        m_ref[...] = jnp.full(m_ref.shape, NEG, F32)
        l_ref[...] = jnp.zeros(l_ref.shape, F32)
        acc_ref[...] = jnp.zeros(acc_ref.shape, F32)

    def scores(r, diagonal):
        nk = (r + 1) * sub if diagonal else tq
        s = jnp.dot(q_ref[0, pl.ds(r * sub, sub), :], kt_ref[0, :, 0:nk], preferred_element_type=F32)
        if diagonal:
            qpos = r * sub + lax.broadcasted_iota(jnp.int32, s.shape, 0)
            kpos = lax.broadcasted_iota(jnp.int32, s.shape, 1)
            s = jnp.where(kpos <= qpos, s, NEG)
        return s

    def block(diagonal):
        s_next = scores(0, diagonal)
        for r in range(nsub):
            s = s_next
            if r + 1 < nsub:
                s_next = scores(r + 1, diagonal)
            rows = pl.ds(r * sub, sub)
            nk = s.shape[1]
            m_old = m_ref[rows, :]
            m_new = jnp.maximum(m_old, jnp.max(_lane_fold(s, jnp.maximum), axis=-1, keepdims=True))
            alpha = jnp.exp2(m_old - m_new)
            p = jnp.exp2(s - m_new)
            l_ref[rows, :] = alpha * l_ref[rows, :] + jnp.sum(_lane_fold(p, jnp.add), axis=-1,
                                                              keepdims=True)
            acc_ref[rows, :] = alpha * acc_ref[rows, :] + jnp.dot(
                p.astype(BF16), v_ref[0, 0:nk, :], preferred_element_type=F32)
            m_ref[rows, :] = m_new

    @pl.when(ki < qi)
    def _():
        block(False)

    @pl.when(ki == qi)
    def _():
        block(True)
        o_ref[0] = acc_ref[...] / l_ref[...]


def _flash_prompt(q, kt, v, nbatch, seq):
    tq = min(seq, 2048)
    nq = seq // tq
    pairs =[(a, c) for a in range(nq) for c in range(a + 1)]
    qi_tab = jnp.asarray([p[0] for p in pairs], jnp.int32)
    ki_tab = jnp.asarray([p[1] for p in pairs], jnp.int32)
    qspec = pl.BlockSpec((1, tq, LANES), lambda b, h, t, qt, kt: (h, b * nq + qt[t], 0))
    kspec = pl.BlockSpec((1, tq, LANES), lambda b, h, t, qt, kt: (h, b * nq + kt[t], 0))
    ktspec = pl.BlockSpec((1, LANES, tq), lambda b, h, t, qt, kt: (h, 0, b * nq + kt[t]))
    grid_spec = pltpu.PrefetchScalarGridSpec(
        num_scalar_prefetch=2,
        grid=(nbatch, H_C, len(pairs)),
        in_specs=[qspec, ktspec, kspec],
        out_specs=qspec,
        scratch_shapes=[pltpu.VMEM((tq, 1), F32), pltpu.VMEM((tq, 1), F32), pltpu.VMEM((tq, LANES), F32)],
    )
    return pl.pallas_call(
        functools.partial(_flash_kernel, tq, min(tq, 256)),
        grid_spec=grid_spec,
        out_shape=jax.ShapeDtypeStruct((H_C, nbatch * seq, LANES), F32),
        compiler_params=_cparams(("parallel", "parallel", "arbitrary")),
    )(qi_tab, ki_tab, q, kt, v)


PAGES_PER_STEP = 64


def _joint_softmax_update(tiles, values, m_ref, l_ref, acc_ref):
    def tree(xs, op):
        while len(xs) > 1:
            xs = [op(xs[i], xs[i + 1]) for i in range(0, len(xs) - 1, 2)] + (
                [xs[-1]] if len(xs) % 2 else [])
        return xs[0]

    m_old = m_ref[...]
    m_new = jnp.maximum(m_old, jnp.max(tree(list(tiles), jnp.maximum), axis=-1, keepdims=True))
    alpha = jnp.exp(m_old - m_new)
    ps = [jnp.exp(s - m_new) for s in tiles]
    pvs = [jnp.dot(p.astype(BF16), c, preferred_element_type=F32) for p, c in zip(ps, values)]
    l_ref[...] = alpha * l_ref[...] + jnp.sum(tree(ps, jnp.add), axis=-1, keepdims=True)
    acc_ref[...] = alpha * acc_ref[...] + tree(pvs, jnp.add)
    m_ref[...] = m_new


def _paged_kernel(snew, npg, pt_ref, ql_ref, qp_ref, *rest):
    ckv_refs = rest[:npg]
    kr_refs = rest[npg:2 * npg]
    cn_ref, kn_ref, wv_ref, o_ref, m_ref, l_ref, acc_ref = rest[2 * npg:]
    pg = pl.program_id(1)
    rows = H_C * snew
    ql = ql_ref[...].reshape(rows, KV_LORA).astype(BF16)
    qp = qp_ref[...].reshape(rows, QK_ROPE).astype(BF16)
    page = ckv_refs[0].shape[2]

    @pl.when(pg == 0)
    def _():
        m_ref[...] = jnp.full(m_ref.shape, NEG, F32)
        l_ref[...] = jnp.zeros(l_ref.shape, F32)
        acc_ref[...] = jnp.zeros(acc_ref.shape, F32)

    def scores(cs, krts):
        lat = [_dot_nt(ql, c) for c in cs]
        rot = [_dot_nn(qp, r) for r in krts]
        return [a + b for a, b in zip(lat, rot)]

    cs = [ckv_refs[j][0, 0].astype(BF16) for j in range(npg)]
    krts = [kr_refs[j][0, 0].astype(BF16) for j in range(npg)]
    _joint_softmax_update(scores(cs, krts), cs, m_ref, l_ref, acc_ref)

    @pl.when(pg == pl.num_programs(1) - 1)
    def _():
        pad = lambda x: jnp.concatenate(
            [x, jnp.zeros((page - snew, x.shape[1]), F32)], axis=0).astype(BF16)
        c = pad(cn_ref[...])
        kn_t = pad(kn_ref[...]).T[ROPE_LANE0:ROPE_LANE0 + QK_ROPE, :]
        s = scores([c], [kn_t])[0]
        tok = lax.broadcasted_iota(jnp.int32, s.shape, 0) % snew
        key = lax.broadcasted_iota(jnp.int32, s.shape, 1)
        s = jnp.where(key <= tok, s, NEG)
        _joint_softmax_update([s], [c], m_ref, l_ref, acc_ref)
        o_lat = acc_ref[...] / l_ref[...]
        for h in range(H_C):
            o_ref[:, h * V_HEAD:(h + 1) * V_HEAD] = jnp.dot(
                o_lat[h * snew:(h + 1) * snew].astype(BF16), wv_ref[h], preferred_element_type=F32)


def _paged_attention(ql, qp, cache_ckv, cache_kr, layer, ckv_new, kr_new, wv, page_table, snew):
    nb, n_pages = page_table.shape
    page = cache_ckv.shape[2]
    npg = min(PAGES_PER_STEP, n_pages)
    ckv_specs = [pl.BlockSpec((1, 1, page, KV_LORA),
                              lambda b, g, pt, j=j: (layer, pt[b, g * npg + j], 0, 0)) for j in range(npg)]
    kr_specs = [pl.BlockSpec((1, 1, QK_ROPE, page),
                             lambda b, g, pt, j=j: (layer, pt[b, g * npg + j], 0, 0)) for j in range(npg)]
    rows = H_C * snew
    grid_spec = pltpu.PrefetchScalarGridSpec(
        num_scalar_prefetch=1,
        grid=(nb, n_pages // npg),
        in_specs=[pl.BlockSpec((H_C, snew, KV_LORA), lambda b, g, pt: (0, b, 0)),
                  pl.BlockSpec((H_C, snew, QK_ROPE), lambda b, g, pt: (0, b, 0))]
                 + ckv_specs + kr_specs
                 + [pl.BlockSpec((snew, KV_LORA), lambda b, g, pt: (b, 0)),
                    pl.BlockSpec((snew, LANES), lambda b, g, pt: (b, 0)),
                    pl.BlockSpec((H_C, KV_LORA, V_HEAD), lambda b, g, pt: (0, 0, 0))],
        out_specs=pl.BlockSpec((snew, D_BR), lambda b, g, pt: (b, 0)),
        scratch_shapes=[pltpu.VMEM((rows, 1), F32), pltpu.VMEM((rows, 1), F32),
                        pltpu.VMEM((rows, KV_LORA), F32)],
    )
    return pl.pallas_call(
        functools.partial(_paged_kernel, snew, npg),
        grid_spec=grid_spec,
        out_shape=jax.ShapeDtypeStruct((nb * snew, D_BR), F32),
        compiler_params=_cparams(("parallel", "arbitrary")),
    )(page_table, ql, qp, *([cache_ckv] * npg), *([cache_kr] * npg), ckv_new, kr_new, wv)


def _merge_kernel(nb, tq, y_grouped, o_grouped, y_ref, vkr_ref, bonus_ref, x_ref, o_ref, ga_ref, cb_ref, cc_ref, cx_ref, gb_ref,
                  gc_ref, ma_ref, mb_ref, mc_ref, hc_ref, hx_ref, lnw_ref, lnb_ref, cw_ref, np_ref,
                  wbr_ref, wout_ref, xo_ref, zt_ref):
    hm = _head_ones()
    tm = nb * tq
    if y_grouped:
        y = jnp.concatenate([y_ref[g] for g in range(N_LANE_GROUPS)], axis=1)
    else:
        y = y_ref[...] + vkr_ref[...]
    mu = _head_sum(y, hm) * (1.0 / HEAD_A)
    yc = y - mu
    var = _head_sum(yc * yc, hm) * (1.0 / HEAD_A)
    out_a = (yc * lax.rsqrt(var + GN_EPS) * lnw_ref[...] + lnb_ref[...] + bonus_ref[...]) * _silu(ga_ref[...])
    z = cc_ref[...] * cx_ref[...]
    zh = hc_ref[...] * hx_ref[...]
    z1 = _shift_rows(z, zh[:, 1:2, :], 1, nb, tq)
    z2 = _shift_rows(z, zh, 2, nb, tq)
    cw = cw_ref[...]
    conv = cw[0:1] * z2 + cw[1:2] * z1 + cw[2:3] * z
    out_b = cb_ref[...] * conv * _silu(gb_ref[...])
    if nb == 1:
        zt_ref[0] = z[tm - SUBLANES:, :]
    else:
        zt_ref[...] = z.reshape(nb, tq, D_BR)
    if o_grouped:
        o = jnp.concatenate([o_ref[h] for h in range(H_C)], axis=1)
    else:
        o = o_ref[...]
    out_c = o * _silu(gc_ref[...])
    mixed = (_sigmoid(ma_ref[...]) * jnp.dot(out_a.astype(BF16), wbr_ref[0], preferred_element_type=F32)
             + _sigmoid(mb_ref[...]) * jnp.dot(out_b.astype(BF16), wbr_ref[1], preferred_element_type=F32)
             + _sigmoid(mc_ref[...]) * jnp.dot(out_c.astype(BF16), wbr_ref[2], preferred_element_type=F32))
    res = jnp.dot(mixed.astype(BF16), wout_ref[...], preferred_element_type=F32)
    xo_ref[...] = x_ref[...] + _rms(res, np_ref[...])


def _merge(y, vkr, bonus, x, o, proj, halo_c, halo_x, lw, nb, tq):
    t = x.shape[0]
    y_grouped, o_grouped = y.ndim == 3, o.ndim == 3
    tm = nb * tq
    nseg = t // tq
    row = pl.BlockSpec((tm, D_BR), lambda i: (i, 0))
    prow = lambda cb: pl.BlockSpec((tm, D_BR), lambda i, cb=cb: (i, cb))
    halo = pl.BlockSpec((nb, 2, D_BR), lambda i: (i, 0, 0))
    vec = pl.BlockSpec((1, D_BR), lambda i: (0, 0))
    grouped = pl.BlockSpec((N_LANE_GROUPS, tm, LANES), lambda i: (0, i, 0))
    zt_rows = min(tq, SUBLANES)
    return pl.pallas_call(
        functools.partial(_merge_kernel, nb, tq, y_grouped, o_grouped),
        grid=(t // tm,),
        in_specs=[grouped if y_grouped else row, row, row, row, grouped if o_grouped else row] + [prow(c) for c in range(3, 12)] + [halo, halo, vec, vec,
                  pl.BlockSpec((3, D_BR), lambda i: (0, 0)), vec,
                  pl.BlockSpec((3, D_BR, D_MODEL), lambda i: (0, 0, 0)),
                  pl.BlockSpec((D_MODEL, D_MODEL), lambda i: (0, 0))],
        out_specs=[row, pl.BlockSpec((nb, zt_rows, D_BR), lambda i: (i, 0, 0))],
        out_shape=[jax.ShapeDtypeStruct((t, D_MODEL), F32),
                   jax.ShapeDtypeStruct((nseg, zt_rows, D_BR), F32)],
        compiler_params=_cparams(("parallel",)),
    )(y, vkr, bonus, x, o, *([proj] * 9), halo_c, halo_x, lw["ln_x_w"], lw["ln_x_b"], lw["conv_w"],
      lw["norm_post"], lw["w_branch"], lw["w_out"])


def _layer_weights(l, p):
    w_in = p["w_in"][l]
    z = lambda n: jnp.zeros((D_MODEL, n), F32)
    o_ga = 3 * D_BR + DECAY_LORA + ICLR_LORA
    o_qa = o_ga + 5 * D_BR
    o_kpe = o_qa + Q_LORA + KV_LORA
    o_gc = o_kpe + QK_ROPE
    w_re = jnp.concatenate([
        w_in[:, :3 * D_BR], w_in[:, o_ga:o_qa], w_in[:, o_gc:o_gc + 4 * D_BR],
        w_in[:, o_qa:o_kpe], w_in[:, 3 * D_BR:o_ga],
        z(ROPE_LANE0), w_in[:, o_kpe:o_gc], z(LANES - ROPE_LANE0 - QK_ROPE),
        z(N_PROJ - COL_KPE - LANES)], axis=1).astype(BF16)
    mu = p["mu_shift"][l]
    r2 = lambda a: a.reshape(1, -1)
    w_uq = p["w_uq"][l]
    wq = jnp.concatenate([w_uq, jnp.zeros((Q_LORA, H_C, LANES - QK_NOPE - QK_ROPE), F32)], axis=2)
    w_uk = p["w_uk"][l]
    wk = jnp.concatenate([w_uk, jnp.zeros((KV_LORA, H_C, LANES - QK_NOPE), F32)], axis=2)
    wuk_abs = jnp.concatenate([jnp.transpose(w_uk, (1, 2, 0)),
                               jnp.zeros((H_C, LANES - QK_NOPE, KV_LORA), F32)], axis=1)
    sel = np.zeros((LANES, QK_ROPE), np.float32)
    sel[ROPE_LANE0 + np.arange(QK_ROPE), np.arange(QK_ROPE)] = 1.0
    zl = jnp.zeros((DECAY_LORA, D_BR), F32)
    return {
        "w_in": w_re, "norm_pre": r2(p["norm_pre"][l]), "norm_post": r2(p["norm_post"][l]),
        "mu_r": r2(mu[:D_BR]), "mu_k": r2(mu[D_BR:2 * D_BR]), "mu_v": r2(mu[2 * D_BR:3 * D_BR]),
        "mu_lora": r2(mu[3 * D_BR:]),
        "w0": r2(p["w0"][l]), "a0": r2(p["a0"][l]),
        "w2p": jnp.concatenate([p["w2"][l], zl], axis=0).astype(BF16),
        "a2p": jnp.concatenate([zl, p["a2"][l]], axis=0).astype(BF16),
        "k_k": r2(p["k_k"][l]), "k_a": r2(p["k_a"][l]), "r_k": r2(p["r_k"][l]),
        "ln_x_w": r2(p["ln_x_w"][l]), "ln_x_b": r2(p["ln_x_b"][l]), "conv_w": p["conv_w"][l],
        "q_a_norm": r2(p["q_a_norm"][l]), "kv_a_norm": r2(p["kv_a_norm"][l]),
        "wq": wq.reshape(Q_LORA, H_C * LANES).astype(BF16),
        "wk": wk.reshape(KV_LORA, H_C * LANES).astype(BF16),
        "wv": p["w_uv"][l].reshape(KV_LORA, H_C * V_HEAD).astype(BF16),
        "wuk_abs": wuk_abs.astype(BF16),
        "wv_heads": jnp.transpose(p["w_uv"][l], (1, 0, 2)).astype(BF16),
        "sel_rope": jnp.asarray(sel, BF16),
        "w_branch": p["w_branch"][l].astype(BF16), "w_out": p["w_out"][l].astype(BF16),
    }


def _rope_tables(pos):
    half = QK_ROPE // 2
    freqs = ROPE_THETA ** (-jnp.arange(half, dtype=F32) / half)
    ang = pos.astype(F32)[:, None] * freqs[None, :]
    cos, sin = jnp.cos(ang), jnp.sin(ang)
    n = pos.shape[0]
    lo = jnp.zeros((n, ROPE_LANE0), F32)
    hi = jnp.zeros((n, LANES - ROPE_LANE0 - QK_ROPE), F32)
    zh = jnp.zeros((n, half), F32)
    ck = jnp.concatenate([lo, cos, cos, hi], axis=1)
    cq = jnp.concatenate([lo + 1.0, cos, cos, hi], axis=1)
    s1 = jnp.concatenate([lo, zh, sin, hi], axis=1)
    s2 = jnp.concatenate([lo, -sin, zh, hi], axis=1)
    return cq, ck, s1, s2


def _prev_row_halos(proj, first_rows, nseq, tseq, tq):
    nseg_per = tseq // tq
    out = {}
    for name, (c0, c1) in {"r": (COL_R, COL_R + D_BR), "k": (COL_K, COL_K + D_BR),
                           "v": (COL_V, COL_V + D_BR), "lora": (COL_LORA, COL_LORA + LANES)}.items():
        first = first_rows[name].reshape(nseq, 1, c1 - c0)
        if nseg_per > 1:
            p3 = proj.reshape(nseq, tseq, N_PROJ)
            inner = p3[:, tq - 1:tseq - 1:tq, c0:c1]
            first = jnp.concatenate([first, inner], axis=1)
        out[name] = first.reshape(nseq * nseg_per, 1, c1 - c0)
    return out


def _conv_halos(proj, conv_prev, nseq, tseq, tq):
    nseg_per = tseq // tq
    hc = conv_prev
    hx = jnp.ones_like(conv_prev)
    if nseg_per > 1:
        p3 = proj.reshape(nseq, nseg_per, tq, N_PROJ)[:, :-1, tq - 2:, :]
        hc = jnp.concatenate([hc[:, None], p3[..., COL_CC:COL_CC + D_BR]], axis=1)
        hx = jnp.concatenate([hx[:, None], p3[..., COL_CX:COL_CX + D_BR]], axis=1)
    return hc.reshape(nseq * nseg_per, 2, D_BR), hx.reshape(nseq * nseg_per, 2, D_BR)


def _layer(x, lw, tabs, nseq, tseq, shift_prev, conv_prev, wkv_prev, attn):
    t = nseq * tseq
    proj = _inproj(x, lw["norm_pre"], lw["w_in"])
    if tseq >= ROW_TILE:
        nb, tq = 1, ROW_TILE
    else:
        nb, tq = min(nseq, ROW_TILE // tseq), tseq
    first = {"r": shift_prev[:, :D_BR], "k": shift_prev[:, D_BR:2 * D_BR],
             "v": shift_prev[:, 2 * D_BR:3 * D_BR], "lora": shift_prev[:, 3 * D_BR:]}
    chunked = tseq % WKV_CHUNK == 0
    ops = _rwkv_prep(proj, _prev_row_halos(proj, first, nseq, tseq, tq), lw, nb, tq, chunked)
    if chunked:
        bonus = ops[6]
        vkr = bonus
        w6 = wkv_prev.reshape(nseq, N_LANE_GROUPS, 2, HEAD_A, HEAD_A)
        s0 = jnp.zeros((nseq, N_LANE_GROUPS, 2, HEAD_A, 2, HEAD_A), F32)
        s0 = s0.at[:, :, 0, :, 0, :].set(w6[:, :, 0]).at[:, :, 1, :, 1, :].set(w6[:, :, 1])
        y, st = _wkv_chunked(ops[:6], s0.reshape(nseq, N_LANE_GROUPS, LANES, LANES), nseq, tseq)
        st6 = st.reshape(nseq, N_LANE_GROUPS, 2, HEAD_A, 2, HEAD_A)
        wkv_new = jnp.stack([st6[:, :, 0, :, 0, :], st6[:, :, 1, :, 1, :]], axis=2).reshape(
            nseq, H_A, HEAD_A, HEAD_A)
    else:
        vkr, bonus = ops[7], ops[8]
        s0 = jnp.transpose(wkv_prev, (0, 2, 1, 3)).reshape(nseq, HEAD_A, D_BR)
        y, st = _wkv_scan(ops[:7], s0, nseq, tseq)
        wkv_new = jnp.transpose(st.reshape(nseq, HEAD_A, H_A, HEAD_A), (0, 2, 1, 3))
    o, ckv, kr128 = attn(proj, lw, tabs)
    hc, hx = _conv_halos(proj, conv_prev, nseq, tseq, tq)
    x_new, zt = _merge(y, vkr, bonus, x, o, proj, hc, hx, lw, nb, tq)
    p3 = proj.reshape(nseq, tseq, N_PROJ)[:, -1, :]
    new_shift = jnp.concatenate([p3[:, :3 * D_BR], p3[:, COL_LORA:COL_LORA + LANES]], axis=1)
    new_conv = zt.reshape(nseq, tseq // tq, -1, D_BR)[:, -1, -2:, :]
    kr = kr128[:, ROPE_LANE0:ROPE_LANE0 + QK_ROPE]
    return x_new, (ckv.reshape(nseq, tseq, KV_LORA), kr.reshape(nseq, tseq, QK_ROPE), wkv_new,
                   new_shift, new_conv)


def kernel(x_prompt, x_sample, cache_ckv, cache_krope, state_wkv, state_shift, state_conv, page_table,
           norm_pre, norm_post, w_in, mu_shift, w0, w2, a0, a2, k_k, k_a, r_k, ln_x_w, ln_x_b,
           conv_w, q_a_norm, w_uq, kv_a_norm, w_uk, w_uv, w_branch, w_out):
    params = dict(norm_pre=norm_pre, norm_post=norm_post, w_in=w_in, mu_shift=mu_shift, w0=w0, w2=w2,
                  a0=a0, a2=a2, k_k=k_k, k_a=k_a, r_k=r_k.reshape(r_k.shape[0], -1), ln_x_w=ln_x_w,
                  ln_x_b=ln_x_b, conv_w=conv_w, q_a_norm=q_a_norm, w_uq=w_uq, kv_a_norm=kv_a_norm,
                  w_uk=w_uk, w_uv=w_uv, w_branch=w_branch, w_out=w_out)
    bp, sp, _ = x_prompt.shape
    bd, sd, _ = x_sample.shape
    depth = w_in.shape[0]
    past_len = page_table.shape[1] * cache_ckv.shape[2]
    tabs_p = _rope_tables(jnp.arange(sp))
    tm_s = min(bd * sd, 256)
    tabs_s = _rope_tables(past_len + (jnp.arange(tm_s) % sd))
    cache_krope = jnp.swapaxes(cache_krope, 2, 3)

    xp = x_prompt.reshape(bp * sp, D_MODEL)
    xs = x_sample.reshape(bd * sd, D_MODEL)
    acc_p, acc_s = [], []
    for l in range(depth):
        lw = _layer_weights(l, params)

        def attn_prompt(proj, lw, tabs):
            q, k, v, ckv, kr128 = _mla_prep_prompt(proj, lw, tabs, sp)
            return _flash_prompt(q, k, v, bp, sp), ckv, kr128

        def attn_sample(proj, lw, tabs, l=l):
            ql, qp, ckv, kr128 = _mla_prep_sample(proj, lw, tabs)
            o = _paged_attention(ql, qp, cache_ckv, cache_krope, l, ckv, kr128, lw["wv_heads"],
                                 page_table, sd)
            return o, ckv, kr128

        xp, st_p = _layer(xp, lw, tabs_p, bp, sp, jnp.zeros((bp, state_shift.shape[2]), F32),
                          jnp.zeros((bp, 2, D_BR), F32), jnp.zeros((bp, H_A, HEAD_A, HEAD_A), F32),
                          attn_prompt)
        xs, st_s = _layer(xs, lw, tabs_s, bd, sd, state_shift[l], state_conv[l], state_wkv[l],
                          attn_sample)
        acc_p.append(st_p)
        acc_s.append(st_s)
    outs_p = [jnp.stack([a[i] for a in acc_p], axis=0) for i in range(5)]
    outs_s = [jnp.stack([a[i] for a in acc_s], axis=0) for i in range(5)]
    return (xp.reshape(bp, sp, D_MODEL), xs.reshape(bd, sd, D_MODEL), *outs_p, *outs_s)
```

```python
import functools
import math

import numpy as np
import jax
import jax.numpy as jnp
from jax import lax
from jax.experimental import pallas as pl
from jax.experimental.pallas import tpu as pltpu

F32 = jnp.float32
BF16 = jnp.bfloat16

D_MODEL = 1024
D_BR = D_MODEL
HEAD_A = 64
H_A = D_BR // HEAD_A
DECAY_LORA = 64
ICLR_LORA = 64
GN_EPS = 64e-5
H_C = 8
QK_NOPE = 64
QK_ROPE = 32
V_HEAD = D_BR // H_C
Q_LORA = D_MODEL // 4
KV_LORA = D_MODEL // 4
ROPE_THETA = 10000.0
NORM_EPS = 1e-6
LANES = 128
SUBLANES = 8
N_LANE_GROUPS = D_BR // LANES
ROPE_LANE0 = 64
VMEM_LIMIT = 56 * 1024 * 1024
ROW_TILE = 256

COL_R, COL_K, COL_V, COL_GA, COL_CB, COL_CC, COL_CX, COL_GB, COL_GC, COL_MA, COL_MB, COL_MC = (
    i * D_BR for i in range(12))
COL_QA = 12 * D_BR
COL_KVA = COL_QA + Q_LORA
COL_LORA = COL_KVA + KV_LORA
COL_KPE = COL_LORA + LANES
N_PROJ = 13 * D_BR


def _cparams(sem):
    return pltpu.CompilerParams(dimension_semantics=sem, vmem_limit_bytes=VMEM_LIMIT)


def _head_ones():
    r = lax.broadcasted_iota(jnp.int32, (LANES, LANES), 0) // HEAD_A
    c = lax.broadcasted_iota(jnp.int32, (LANES, LANES), 1) // HEAD_A
    return (r == c).astype(BF16)


def _split_dot(x, w):
    hi = x.astype(BF16)
    lo = (x - hi.astype(F32)).astype(BF16)
    return (jnp.dot(hi, w, preferred_element_type=F32) + jnp.dot(lo, w, preferred_element_type=F32))


def _split_dot_rhs(w, x):
    hi = x.astype(BF16)
    lo = (x - hi.astype(F32)).astype(BF16)
    return (jnp.dot(w, hi, preferred_element_type=F32) + jnp.dot(w, lo, preferred_element_type=F32))


def _head_sum(x, hm):
    parts = [_split_dot(x[:, g * LANES:(g + 1) * LANES], hm) for g in range(x.shape[1] // LANES)]
    return jnp.concatenate(parts, axis=1)


def _shift_rows(p, halo, k, nb, tq):
    c = p.shape[1]
    prev = pltpu.roll(p, k, axis=0)
    pos = lax.broadcasted_iota(jnp.int32, p.shape, 0) % tq
    out = prev
    for j in range(k):
        hb = jnp.broadcast_to(halo[:, j:j + 1, :], (nb, tq, c)).reshape(nb * tq, c)
        out = jnp.where(pos == j, hb, out)
    return out


def _sigmoid(x):
    return 1.0 / (1.0 + jnp.exp(-x))


def _silu(x):
    return x * _sigmoid(x)


def _rms(x, g):
    return x * lax.rsqrt(jnp.mean(x * x, axis=-1, keepdims=True) + NORM_EPS) * g


def _inproj_kernel(x_ref, g_ref, w_ref, o_ref, h_ref):
    @pl.when(pl.program_id(1) == 0)
    def _():
        h_ref[...] = _rms(x_ref[...], g_ref[...]).astype(BF16)

    o_ref[...] = jnp.dot(h_ref[...], w_ref[...], preferred_element_type=F32)


def _inproj(x, g, w):
    t = x.shape[0]
    tm = min(t, 1024)
    tn = 1024
    return pl.pallas_call(
        _inproj_kernel,
        grid=(t // tm, N_PROJ // tn),
        in_specs=[pl.BlockSpec((tm, D_MODEL), lambda i, j: (i, 0)),
                  pl.BlockSpec((1, D_MODEL), lambda i, j: (0, 0)),
                  pl.BlockSpec((D_MODEL, tn), lambda i, j: (0, j))],
        out_specs=pl.BlockSpec((tm, tn), lambda i, j: (i, j)),
        out_shape=jax.ShapeDtypeStruct((t, N_PROJ), F32),
        scratch_shapes=[pltpu.VMEM((tm, D_MODEL), BF16)],
        compiler_params=_cparams(("parallel", "arbitrary")),
    )(x, g, w)


def _rwkv_prep_kernel(nb, tq, chunked, r_ref, k_ref, v_ref, lo_ref, hr_ref, hk_ref, hv_ref, hl_ref,
                      mur_ref, muk_ref, muv_ref, mul_ref, w0_ref, w2_ref, a0_ref, a2_ref,
                      kk_ref, ka_ref, rk_ref, *outs):
    hm = _head_ones()

    def mix(ref, halo_ref, mu_ref):
        p = ref[...]
        prev = _shift_rows(p, halo_ref[...], 1, nb, tq)
        return p + (prev - p) * mu_ref[...]

    r = mix(r_ref, hr_ref, mur_ref)
    k = mix(k_ref, hk_ref, muk_ref)
    v = mix(v_ref, hv_ref, muv_ref)
    lo = mix(lo_ref, hl_ref, mul_ref)
    wf = w0_ref[...] + jnp.dot(jnp.tanh(lo).astype(BF16), w2_ref[...], preferred_element_type=F32)
    log_decay = -math.exp(-0.5) * _sigmoid(wf)
    a = _sigmoid(a0_ref[...] + jnp.dot(lo.astype(BF16), a2_ref[...], preferred_element_type=F32))
    kk = k * kk_ref[...]
    kk = kk / jnp.maximum(jnp.sqrt(_head_sum(kk * kk, hm)), 1e-12)
    kmod = k * (1.0 + (a - 1.0) * ka_ref[...])
    b = kk * a
    bonus = _head_sum(r * kmod * rk_ref[...], hm) * v
    if chunked:
        *group_outs, bonus_o = outs
        bonus_o[...] = bonus
        for o_ref, val in zip(group_outs, (r, log_decay, kmod, v, -kk, b)):
            for g in range(N_LANE_GROUPS):
                o_ref[g] = val[:, g * LANES:(g + 1) * LANES]
        return
    wr_o, a_o, w_o, b_o, k_o, v_o, br_o, vkr_o, bonus_o = outs
    decay = jnp.exp(log_decay)
    wr_o[...] = decay * r
    a_o[...] = -kk
    w_o[...] = decay
    b_o[...] = b
    k_o[...] = kmod
    v_o[...] = v
    br_o[...] = _head_sum(b * r, hm)
    vkr_o[...] = _head_sum(kmod * r, hm) * v
    bonus_o[...] = bonus


def _rwkv_prep(proj, halos, lw, nb, tq, chunked):
    t = proj.shape[0]
    tm = nb * tq
    row_out = pl.BlockSpec((tm, D_BR), lambda i: (i, 0))
    if chunked:
        out_specs = [pl.BlockSpec((N_LANE_GROUPS, tm, LANES), lambda i: (0, i, 0))] * 6 + [row_out]
        outs = [jax.ShapeDtypeStruct((N_LANE_GROUPS, t, LANES), F32)] * 6 + [
            jax.ShapeDtypeStruct((t, D_BR), F32)]
    else:
        out_specs = [row_out] * 9
        outs = [jax.ShapeDtypeStruct((t, D_BR), F32)] * 9
    row = lambda cb: pl.BlockSpec((tm, D_BR), lambda i, cb=cb: (i, cb))
    halo = lambda c: pl.BlockSpec((nb, 1, c), lambda i: (i, 0, 0))
    vec = lambda c: pl.BlockSpec((1, c), lambda i: (0, 0))
    mat = pl.BlockSpec((LANES, D_BR), lambda i: (0, 0))
    return pl.pallas_call(
        functools.partial(_rwkv_prep_kernel, nb, tq, chunked),
        grid=(t // tm,),
        in_specs=[row(0), row(1), row(2),
                  pl.BlockSpec((tm, LANES), lambda i: (i, COL_LORA // LANES)),
                  halo(D_BR), halo(D_BR), halo(D_BR), halo(LANES),
                  vec(D_BR), vec(D_BR), vec(D_BR), vec(LANES),
                  vec(D_BR), mat, vec(D_BR), mat, vec(D_BR), vec(D_BR), vec(D_BR)],
        out_specs=out_specs,
        out_shape=outs,
        compiler_params=_cparams(("parallel",)),
    )(proj, proj, proj, proj, halos["r"], halos["k"], halos["v"], halos["lora"],
      lw["mu_r"], lw["mu_k"], lw["mu_v"], lw["mu_lora"], lw["w0"], lw["w2p"], lw["a0"], lw["a2p"],
      lw["k_k"], lw["k_a"], lw["r_k"])


def _wkv_kernel(ns, tb, wr_ref, a_ref, w_ref, b_ref, k_ref, v_ref, br_ref, s0_ref,
                y_ref, st_ref, p_ref):
    @pl.when(pl.program_id(1) == 0)
    def _():
        st_ref[...] = s0_ref[...]

    hm = _head_ones()
    ri = lax.broadcasted_iota(jnp.int32, (HEAD_A, LANES), 0)
    ci = lax.broadcasted_iota(jnp.int32, (HEAD_A, LANES), 1)
    diag = ((ci % HEAD_A) == ri).astype(F32)

    er = lax.broadcasted_iota(jnp.int32, (SUBLANES, HEAD_A), 0)
    ec = lax.broadcasted_iota(jnp.int32, (SUBLANES, HEAD_A), 1)
    tok_sum = ((ec // SUBLANES) == er).astype(BF16)

    def step(u, carry, base):
        t = base + u
        for s in range(ns):
            row = lambda ref: ref[s, pl.ds(t, 1), :]
            wr, a, w, b, k, v, br = (row(x) for x in (wr_ref, a_ref, w_ref, b_ref, k_ref, v_ref, br_ref))
            for g in range(N_LANE_GROUPS):
                sl = slice(g * LANES, (g + 1) * LANES)
                st = st_ref[s, :, sl]
                m1 = st * a[:, sl]
                m1_hi = m1.astype(BF16)
                m1_lo = (m1 - m1_hi.astype(F32)).astype(BF16)
                m2 = (st * wr[:, sl]).astype(BF16)
                m3 = (diag * v[:, sl]).astype(BF16)
                res = jnp.dot(jnp.concatenate([m1_hi, m1_lo, m2, m3], axis=0), hm,
                              preferred_element_type=F32)
                sa = res[0:HEAD_A] + res[HEAD_A:2 * HEAD_A]
                y2 = res[2 * HEAD_A:3 * HEAD_A]
                vc = res[3 * HEAD_A:4 * HEAD_A]
                st_ref[s, :, sl] = st * w[:, sl] + sa * b[:, sl] + vc * k[:, sl]
                yd = (y2 + sa * br[:, sl]) * diag
                p_ref[s, u, :, sl] = jnp.sum(yd.reshape(HEAD_A // SUBLANES, SUBLANES, LANES), axis=0)
        return carry

    def block8(t8, carry):
        base = pl.multiple_of(t8 * SUBLANES, SUBLANES)
        lax.fori_loop(0, SUBLANES, functools.partial(step, base=base), 0)
        for s in range(ns):
            y_ref[s, pl.ds(base, SUBLANES), :] = _split_dot_rhs(
                tok_sum, p_ref[s].reshape(SUBLANES * SUBLANES, D_BR))
        return carry

    lax.fori_loop(0, tb // SUBLANES, block8, 0)


def _wkv_scan(ops, s0, nseq, tseq):
    ns = 8 if nseq % 8 == 0 else 2
    tb = min(tseq, 128)
    ops3 = [o.reshape(nseq, tseq, D_BR) for o in ops]
    blk = pl.BlockSpec((ns, tb, D_BR), lambda i, j: (i, j, 0))
    sblk = pl.BlockSpec((ns, HEAD_A, D_BR), lambda i, j: (i, 0, 0))
    y, st = pl.pallas_call(
        functools.partial(_wkv_kernel, ns, tb),
        grid=(nseq // ns, tseq // tb),
        in_specs=[blk] * 7 + [sblk],
        out_specs=[blk, sblk],
        out_shape=[jax.ShapeDtypeStruct((nseq, tseq, D_BR), F32),
                   jax.ShapeDtypeStruct((nseq, HEAD_A, D_BR), F32)],
        scratch_shapes=[pltpu.VMEM((ns, SUBLANES, SUBLANES, D_BR), F32)],
        compiler_params=_cparams(("parallel", "arbitrary")),
    )(*ops3, s0)
    return y.reshape(nseq * tseq, D_BR), st


WKV_CHUNK = 64
WKV_PASSES_LOCAL = 1
WKV_PASSES_STATE = 3


def _dot_nn(a, b):
    return jnp.dot(a, b, preferred_element_type=F32)


def _dot_nt(a, b):
    return lax.dot_general(a, b, (((1,), (1,)), ((), ())), preferred_element_type=F32)


def _dot_tn(a, b):
    return lax.dot_general(a, b, (((0,), (0,)), ((), ())), preferred_element_type=F32)


def _hilo(x):
    hi = x.astype(BF16)
    return hi, (x - hi.astype(F32)).astype(BF16)


def _mm(dot, a, b, passes):
    ah, al = _hilo(a)
    bh, bl = _hilo(b)
    out = dot(ah, bh)
    if passes >= 3:
        out = out + dot(al, bh) + dot(ah, bl)
    return out


def _wkv_chunk_kernel(ns, r_ref, lw_ref, k_ref, v_ref, a_ref, b_ref, s0_ref, y_ref, st_ref):
    L = r_ref.shape[2]

    @pl.when(pl.program_id(1) == 0)
    def _():
        st_ref[...] = s0_ref[...]

    ti = lax.broadcasted_iota(jnp.int32, (L, L), 0)
    si = lax.broadcasted_iota(jnp.int32, (L, L), 1)
    incl = si <= ti
    strict = si < ti
    eye_l = (si == ti).astype(F32)
    tril_ones = incl.astype(BF16)
    levels = []
    m = 1
    while m < L:
        levels.append(((ti // (2 * m)) == (si // (2 * m))) & ((ti % (2 * m)) >= m) & ((si % (2 * m)) < m))
        m *= 2
    lane = lax.broadcasted_iota(jnp.int32, (1, LANES), 1)
    head_masks = [(lane < HEAD_A).astype(F32), (lane >= HEAD_A).astype(F32)]
    gr = lax.broadcasted_iota(jnp.int32, (LANES, LANES), 0)
    gc = lax.broadcasted_iota(jnp.int32, (LANES, LANES), 1)
    same_head = (gr // HEAD_A) == (gc // HEAD_A)
    eye_g = gr == gc

    def each(fn, *lists):
        return [fn(*xs) for xs in zip(*lists)]

    def mm(dot, xs, ys, passes=WKV_PASSES_LOCAL):
        return each(lambda x, y: _mm(dot, x, y, passes), xs, ys)

    chains = [(s, g) for s in range(ns) for g in range(N_LANE_GROUPS)]
    r, lw, k, v, a, b = ([ref[g, s] for s, g in chains]
                         for ref in (r_ref, lw_ref, k_ref, v_ref, a_ref, b_ref))
    st = [st_ref[s, g] for s, g in chains]

    def cumsum(x):
        hi, lo = _hilo(x)
        return _dot_nn(tril_ones, hi) + _dot_nn(tril_ones, lo)

    c = each(cumsum, lw)
    c_last = [x[L - 1:L, :] for x in c]
    e_neg = [jnp.exp(-x) for x in c]
    e_end = each(lambda cl, x: jnp.exp(cl - x), c_last, c)
    at = each(lambda a_, x, w_: a_ * jnp.exp(x - w_), a, c, lw)
    rt = each(lambda r_, x: r_ * jnp.exp(x), r, c)
    bt = each(jnp.multiply, b, e_neg)
    kt = each(jnp.multiply, k, e_neg)
    bend = each(jnp.multiply, b, e_end)
    kend = each(jnp.multiply, k, e_end)

    def per_head(xs):
        return [x * hm_ for x in xs for hm_ in head_masks]

    def per_group(xs):
        return [x for x in xs for _ in head_masks]

    at_h, rt_h, v_h = per_head(at), per_head(rt), per_head(v)
    q2 = each(lambda x, y: jnp.concatenate([x, y], axis=0), at_h, rt_h)
    gb = mm(_dot_nt, q2, per_group(bt))
    gk = mm(_dot_nt, q2, per_group(kt))
    n = [jnp.where(strict, x[:L], 0.0) for x in gb]
    pb = [jnp.where(incl, x[L:], 0.0) for x in gb]
    m_ = [jnp.where(strict, x[:L], 0.0) for x in gk]
    pk = [jnp.where(incl, x[L:], 0.0) for x in gk]
    t_inv = [eye_l + jnp.where(levels[0], x, 0.0) for x in n]
    for lev in levels[1:]:
        cm = [jnp.where(lev, x, 0.0) for x in n]
        t_inv = each(jnp.add, t_inv, mm(_dot_nn, mm(_dot_nn, t_inv, cm), t_inv))
    abar_h = mm(_dot_nn, t_inv, at_h)
    uv_h = mm(_dot_nn, mm(_dot_nn, t_inv, m_), v_h)
    rbar_h = mm(_dot_nn, pb, abar_h)
    yv_h = each(jnp.add, mm(_dot_nn, pb, uv_h), mm(_dot_nn, pk, v_h))

    def head_total(xs):
        return [xs[2 * i] + xs[2 * i + 1] for i in range(len(chains))]

    abar, uv, yv = head_total(abar_h), head_total(uv_h), head_total(yv_h)
    rbar = each(jnp.add, rt, head_total(rbar_h))
    phi = each(lambda x, cl: jnp.where(same_head, x, 0.0) + jnp.where(eye_g, jnp.exp(cl), 0.0),
               mm(_dot_tn, abar, bend), c_last)
    psi = each(lambda x, y: jnp.where(same_head, x + y, 0.0), mm(_dot_tn, uv, bend), mm(_dot_tn, v, kend))
    y = each(jnp.add, mm(_dot_nt, rbar, st, WKV_PASSES_STATE), yv)
    st_new = each(jnp.add, mm(_dot_nn, st, phi, WKV_PASSES_STATE), psi)
    for (s, g), y_, st_ in zip(chains, y, st_new):
        y_ref[g, s] = y_
        st_ref[s, g] = st_


def _wkv_chunked(ops, s0, nseq, tseq):
    ns = 2
    L = WKV_CHUNK
    ops4 = [o.reshape(N_LANE_GROUPS, nseq, tseq, LANES) for o in ops]
    blk = pl.BlockSpec((N_LANE_GROUPS, ns, L, LANES), lambda i, j: (0, i, j, 0))
    sblk = pl.BlockSpec((ns, N_LANE_GROUPS, LANES, LANES), lambda i, j: (i, 0, 0, 0))
    y, st = pl.pallas_call(
        functools.partial(_wkv_chunk_kernel, ns),
        grid=(nseq // ns, tseq // L),
        in_specs=[blk] * 6 + [sblk],
        out_specs=[blk, sblk],
        out_shape=[jax.ShapeDtypeStruct((N_LANE_GROUPS, nseq, tseq, LANES), F32),
                   jax.ShapeDtypeStruct((nseq, N_LANE_GROUPS, LANES, LANES), F32)],
        compiler_params=_cparams(("parallel", "arbitrary")),
    )(*ops4, s0)
    return y.reshape(N_LANE_GROUPS, nseq * tseq, LANES), st


def _rope(x, c_ref, s1_ref, s2_ref):
    parts = []
    for h in range(x.shape[1] // LANES):
        xh = x[:, h * LANES:(h + 1) * LANES]
        half = QK_ROPE // 2
        parts.append(xh * c_ref[...] + pltpu.roll(xh, half, axis=1) * s1_ref[...]
                     + pltpu.roll(xh, LANES - half, axis=1) * s2_ref[...])
    return parts[0] if len(parts) == 1 else jnp.concatenate(parts, axis=1)


SOFTMAX_SCALE = (QK_NOPE + QK_ROPE) ** -0.5
LOG2E = math.log2(math.e)


def _mla_common(qa_ref, kva_ref, kpe_ref, qn_ref, kvn_ref, wq_ref, cq_ref, ck_ref, s1_ref, s2_ref,
                scale=SOFTMAX_SCALE):
    qn = _rms(qa_ref[...], qn_ref[...]).astype(BF16)
    q = jnp.dot(qn, wq_ref[...], preferred_element_type=F32)
    q = _rope(q, cq_ref, s1_ref, s2_ref) * scale
    ckv = _rms(kva_ref[...], kvn_ref[...])
    kr = _rope(kpe_ref[...], ck_ref, s1_ref, s2_ref)
    return q, ckv, kr


def _mla_prep_prompt_kernel(qa_ref, kva_ref, kpe_ref, qn_ref, kvn_ref, wq_ref, wk_ref, wv_ref,
                            cq_ref, ck_ref, s1_ref, s2_ref, q_o, k_o, v_o, ckv_o, kr_o):
    q, ckv, kr = _mla_common(qa_ref, kva_ref, kpe_ref, qn_ref, kvn_ref, wq_ref, cq_ref, ck_ref,
                             s1_ref, s2_ref, scale=SOFTMAX_SCALE * LOG2E)
    ckv_o[...] = ckv
    kr_o[...] = kr
    cb = ckv.astype(BF16)
    kn = jnp.dot(cb, wk_ref[...], preferred_element_type=F32)
    vv = jnp.dot(cb, wv_ref[...], preferred_element_type=F32)
    for h in range(H_C):
        sl = slice(h * LANES, (h + 1) * LANES)
        q_o[h] = q[:, sl].astype(BF16)
        k_o[h] = (kn[:, sl] + kr).T.astype(BF16)
        v_o[h] = vv[:, sl].astype(BF16)


def _small_specs(tm, npos_blocks):
    return [pl.BlockSpec((tm, Q_LORA), lambda i: (i, COL_QA // Q_LORA)),
            pl.BlockSpec((tm, KV_LORA), lambda i: (i, COL_KVA // KV_LORA)),
            pl.BlockSpec((tm, LANES), lambda i: (i, COL_KPE // LANES)),
            pl.BlockSpec((1, Q_LORA), lambda i: (0, 0)),
            pl.BlockSpec((1, KV_LORA), lambda i: (0, 0)),
            pl.BlockSpec((Q_LORA, H_C * LANES), lambda i: (0, 0))], \
           [pl.BlockSpec((tm, LANES), lambda i: (i % npos_blocks, 0))] * 4


def _mla_prep_prompt(proj, lw, tabs, seq):
    t = proj.shape[0]
    tm = min(seq, 512)
    head, tab = _small_specs(tm, seq // tm)
    wfull = pl.BlockSpec((KV_LORA, H_C * LANES), lambda i: (0, 0))
    row = lambda c: pl.BlockSpec((tm, c), lambda i: (i, 0))
    return pl.pallas_call(
        _mla_prep_prompt_kernel,
        grid=(t // tm,),
        in_specs=head + [wfull, wfull] + tab,
        out_specs=[pl.BlockSpec((H_C, tm, LANES), lambda i: (0, i, 0)),
                   pl.BlockSpec((H_C, LANES, tm), lambda i: (0, 0, i)),
                   pl.BlockSpec((H_C, tm, LANES), lambda i: (0, i, 0)), row(KV_LORA), row(LANES)],
        out_shape=[jax.ShapeDtypeStruct((H_C, t, LANES), BF16), jax.ShapeDtypeStruct((H_C, LANES, t), BF16),
                   jax.ShapeDtypeStruct((H_C, t, LANES), BF16)]
                  + [jax.ShapeDtypeStruct((t, KV_LORA), F32), jax.ShapeDtypeStruct((t, LANES), F32)],
        compiler_params=_cparams(("parallel",)),
    )(proj, proj, proj, lw["q_a_norm"], lw["kv_a_norm"], lw["wq"], lw["wk"], lw["wv"], *tabs)


def _mla_prep_sample_kernel(qa_ref, kva_ref, kpe_ref, qn_ref, kvn_ref, wq_ref, wuk_ref, sel_ref,
                            cq_ref, ck_ref, s1_ref, s2_ref, ql_o, qp_o, ckv_o, kr_o):
    q, ckv, kr = _mla_common(qa_ref, kva_ref, kpe_ref, qn_ref, kvn_ref, wq_ref, cq_ref, ck_ref,
                             s1_ref, s2_ref)
    ckv_o[...] = ckv
    kr_o[...] = kr
    qb = q.astype(BF16)
    for h in range(H_C):
        qh = qb[:, h * LANES:(h + 1) * LANES]
        ql_o[h] = jnp.dot(qh, wuk_ref[h], preferred_element_type=F32)
        qp_o[h] = jnp.dot(qh, sel_ref[...], preferred_element_type=F32)


def _mla_prep_sample(proj, lw, tabs):
    t = proj.shape[0]
    tm = min(t, 256)
    head, tab = _small_specs(tm, 1)
    row = lambda c: pl.BlockSpec((tm, c), lambda i: (i, 0))
    return pl.pallas_call(
        _mla_prep_sample_kernel,
        grid=(t // tm,),
        in_specs=head + [pl.BlockSpec((H_C, LANES, KV_LORA), lambda i: (0, 0, 0)),
                         pl.BlockSpec((LANES, QK_ROPE), lambda i: (0, 0))] + tab,
        out_specs=[pl.BlockSpec((H_C, tm, KV_LORA), lambda i: (0, i, 0)),
                   pl.BlockSpec((H_C, tm, QK_ROPE), lambda i: (0, i, 0)),
                   row(KV_LORA), row(LANES)],
        out_shape=[jax.ShapeDtypeStruct((H_C, t, KV_LORA), F32),
                   jax.ShapeDtypeStruct((H_C, t, QK_ROPE), F32),
                   jax.ShapeDtypeStruct((t, KV_LORA), F32), jax.ShapeDtypeStruct((t, LANES), F32)],
        compiler_params=_cparams(("parallel",)),
    )(proj, proj, proj, lw["q_a_norm"], lw["kv_a_norm"], lw["wq"], lw["wuk_abs"], lw["sel_rope"], *tabs)


NEG = -1e30


def _lane_fold(x, op):
    parts = [x[:, j * LANES:(j + 1) * LANES] for j in range(x.shape[1] // LANES)]
    while len(parts) > 1:
        parts = [op(parts[i], parts[i + 1]) for i in range(0, len(parts) - 1, 2)] + (
            [parts[-1]] if len(parts) % 2 else [])
    return parts[0]


def _flash_kernel(tq, sub, qi_ref, ki_ref, q_ref, kt_ref, v_ref, o_ref, m_ref, l_ref, acc_ref):
    step = pl.program_id(2)
    qi = qi_ref[step]
    ki = ki_ref[step]
    nsub = tq // sub

    @pl.when(ki == 0)
    def _():
        m_ref[...] = jnp.full(m_ref.shape, NEG, F32)
        l_ref[...] = jnp.zeros(l_ref.shape, F32)
        acc_ref[...] = jnp.zeros(acc_ref.shape, F32)

    def scores(r, diagonal):
        nk = (r + 1) * sub if diagonal else tq
        s = jnp.dot(q_ref[0, pl.ds(r * sub, sub), :], kt_ref[0, :, 0:nk], preferred_element_type=F32)
        if diagonal:
            qpos = r * sub + lax.broadcasted_iota(jnp.int32, s.shape, 0)
            kpos = lax.broadcasted_iota(jnp.int32, s.shape, 1)
            s = jnp.where(kpos <= qpos, s, NEG)
        return s

    def block(diagonal):
        s_next = scores(0, diagonal)
        for r in range(nsub):
            s = s_next
            if r + 1 < nsub:
                s_next = scores(r + 1, diagonal)
            rows = pl.ds(r * sub, sub)
            nk = s.shape[1]
            m_old = m_ref[rows, :]
            m_new = jnp.maximum(m_old, jnp.max(_lane_fold(s, jnp.maximum), axis=-1, keepdims=True))
            alpha = jnp.exp2(m_old - m_new)
            p = jnp.exp2(s - m_new)
            l_ref[rows, :] = alpha * l_ref[rows, :] + jnp.sum(_lane_fold(p, jnp.add), axis=-1,
                                                              keepdims=True)
            acc_ref[rows, :] = alpha * acc_ref[rows, :] + jnp.dot(
                p.astype(BF16), v_ref[0, 0:nk, :], preferred_element_type=F32)
            m_ref[rows, :] = m_new

    @pl.when(ki < qi)
    def _():
        block(False)

    @pl.when(ki == qi)
    def _():
        block(True)
        o_ref[0] = acc_ref[...] / l_ref[...]


def _flash_prompt(q, kt, v, nbatch, seq):
    tq = min(seq, 2048)
    nq = seq // tq
    pairs =[(a, c) for a in range(nq) for c in range(a + 1)]
    qi_tab = jnp.asarray([p[0] for p in pairs], jnp.int32)
    ki_tab = jnp.asarray([p[1] for p in pairs], jnp.int32)
    qspec = pl.BlockSpec((1, tq, LANES), lambda b, h, t, qt, kt: (h, b * nq + qt[t], 0))
    kspec = pl.BlockSpec((1, tq, LANES), lambda b, h, t, qt, kt: (h, b * nq + kt[t], 0))
    ktspec = pl.BlockSpec((1, LANES, tq), lambda b, h, t, qt, kt: (h, 0, b * nq + kt[t]))
    grid_spec = pltpu.PrefetchScalarGridSpec(
        num_scalar_prefetch=2,
        grid=(nbatch, H_C, len(pairs)),
        in_specs=[qspec, ktspec, kspec],
        out_specs=qspec,
        scratch_shapes=[pltpu.VMEM((tq, 1), F32), pltpu.VMEM((tq, 1), F32), pltpu.VMEM((tq, LANES), F32)],
    )
    return pl.pallas_call(
        functools.partial(_flash_kernel, tq, min(tq, 256)),
        grid_spec=grid_spec,
        out_shape=jax.ShapeDtypeStruct((H_C, nbatch * seq, LANES), F32),
        compiler_params=_cparams(("parallel", "parallel", "arbitrary")),
    )(qi_tab, ki_tab, q, kt, v)


PAGES_PER_STEP = 64


def _joint_softmax_update(tiles, values, m_ref, l_ref, acc_ref):
    def tree(xs, op):
        while len(xs) > 1:
            xs = [op(xs[i], xs[i + 1]) for i in range(0, len(xs) - 1, 2)] + (
                [xs[-1]] if len(xs) % 2 else [])
        return xs[0]

    m_old = m_ref[...]
    m_new = jnp.maximum(m_old, jnp.max(tree(list(tiles), jnp.maximum), axis=-1, keepdims=True))
    alpha = jnp.exp(m_old - m_new)
    ps = [jnp.exp(s - m_new) for s in tiles]
    pvs = [jnp.dot(p.astype(BF16), c, preferred_element_type=F32) for p, c in zip(ps, values)]
    l_ref[...] = alpha * l_ref[...] + jnp.sum(tree(ps, jnp.add), axis=-1, keepdims=True)
    acc_ref[...] = alpha * acc_ref[...] + tree(pvs, jnp.add)
    m_ref[...] = m_new


def _paged_kernel(snew, npg, pt_ref, ql_ref, qp_ref, *rest):
    ckv_refs = rest[:npg]
    kr_refs = rest[npg:2 * npg]
    cn_ref, kn_ref, wv_ref, o_ref, m_ref, l_ref, acc_ref = rest[2 * npg:]
    pg = pl.program_id(1)
    rows = H_C * snew
    ql = ql_ref[...].reshape(rows, KV_LORA).astype(BF16)
    qp = qp_ref[...].reshape(rows, QK_ROPE).astype(BF16)
    page = ckv_refs[0].shape[2]

    @pl.when(pg == 0)
    def _():
        m_ref[...] = jnp.full(m_ref.shape, NEG, F32)
        l_ref[...] = jnp.zeros(l_ref.shape, F32)
        acc_ref[...] = jnp.zeros(acc_ref.shape, F32)

    def scores(cs, krts):
        lat = [_dot_nt(ql, c) for c in cs]
        rot = [_dot_nn(qp, r) for r in krts]
        return [a + b for a, b in zip(lat, rot)]

    cs = [ckv_refs[j][0, 0].astype(BF16) for j in range(npg)]
    krts = [kr_refs[j][0, 0].astype(BF16) for j in range(npg)]
    _joint_softmax_update(scores(cs, krts), cs, m_ref, l_ref, acc_ref)

    @pl.when(pg == pl.num_programs(1) - 1)
    def _():
        pad = lambda x: jnp.concatenate(
            [x, jnp.zeros((page - snew, x.shape[1]), F32)], axis=0).astype(BF16)
        c = pad(cn_ref[...])
        kn_t = pad(kn_ref[...]).T[ROPE_LANE0:ROPE_LANE0 + QK_ROPE, :]
        s = scores([c], [kn_t])[0]
        tok = lax.broadcasted_iota(jnp.int32, s.shape, 0) % snew
        key = lax.broadcasted_iota(jnp.int32, s.shape, 1)
        s = jnp.where(key <= tok, s, NEG)
        _joint_softmax_update([s], [c], m_ref, l_ref, acc_ref)
        o_lat = acc_ref[...] / l_ref[...]
        for h in range(H_C):
            o_ref[:, h * V_HEAD:(h + 1) * V_HEAD] = jnp.dot(
                o_lat[h * snew:(h + 1) * snew].astype(BF16), wv_ref[h], preferred_element_type=F32)


def _paged_attention(ql, qp, cache_ckv, cache_kr, layer, ckv_new, kr_new, wv, page_table, snew):
    nb, n_pages = page_table.shape
    page = cache_ckv.shape[2]
    npg = min(PAGES_PER_STEP, n_pages)
    ckv_specs = [pl.BlockSpec((1, 1, page, KV_LORA),
                              lambda b, g, pt, j=j: (layer, pt[b, g * npg + j], 0, 0)) for j in range(npg)]
    kr_specs = [pl.BlockSpec((1, 1, QK_ROPE, page),
                             lambda b, g, pt, j=j: (layer, pt[b, g * npg + j], 0, 0)) for j in range(npg)]
    rows = H_C * snew
    grid_spec = pltpu.PrefetchScalarGridSpec(
        num_scalar_prefetch=1,
        grid=(nb, n_pages // npg),
        in_specs=[pl.BlockSpec((H_C, snew, KV_LORA), lambda b, g, pt: (0, b, 0)),
                  pl.BlockSpec((H_C, snew, QK_ROPE), lambda b, g, pt: (0, b, 0))]
                 + ckv_specs + kr_specs
                 + [pl.BlockSpec((snew, KV_LORA), lambda b, g, pt: (b, 0)),
                    pl.BlockSpec((snew, LANES), lambda b, g, pt: (b, 0)),
                    pl.BlockSpec((H_C, KV_LORA, V_HEAD), lambda b, g, pt: (0, 0, 0))],
        out_specs=pl.BlockSpec((snew, D_BR), lambda b, g, pt: (b, 0)),
        scratch_shapes=[pltpu.VMEM((rows, 1), F32), pltpu.VMEM((rows, 1), F32),
                        pltpu.VMEM((rows, KV_LORA), F32)],
    )
    return pl.pallas_call(
        functools.partial(_paged_kernel, snew, npg),
        grid_spec=grid_spec,
        out_shape=jax.ShapeDtypeStruct((nb * snew, D_BR), F32),
        compiler_params=_cparams(("parallel", "arbitrary")),
    )(page_table, ql, qp, *([cache_ckv] * npg), *([cache_kr] * npg), ckv_new, kr_new, wv)


def _merge_kernel(nb, tq, y_grouped, o_grouped, y_ref, vkr_ref, bonus_ref, x_ref, o_ref, ga_ref, cb_ref, cc_ref, cx_ref, gb_ref,
                  gc_ref, ma_ref, mb_ref, mc_ref, hc_ref, hx_ref, lnw_ref, lnb_ref, cw_ref, np_ref,
                  wbr_ref, wout_ref, xo_ref, zt_ref):
    hm = _head_ones()
    tm = nb * tq
    if y_grouped:
        y = jnp.concatenate([y_ref[g] for g in range(N_LANE_GROUPS)], axis=1)
    else:
        y = y_ref[...] + vkr_ref[...]
    mu = _head_sum(y, hm) * (1.0 / HEAD_A)
    yc = y - mu
    var = _head_sum(yc * yc, hm) * (1.0 / HEAD_A)
    out_a = (yc * lax.rsqrt(var + GN_EPS) * lnw_ref[...] + lnb_ref[...] + bonus_ref[...]) * _silu(ga_ref[...])
    z = cc_ref[...] * cx_ref[...]
    zh = hc_ref[...] * hx_ref[...]
    z1 = _shift_rows(z, zh[:, 1:2, :], 1, nb, tq)
    z2 = _shift_rows(z, zh, 2, nb, tq)
    cw = cw_ref[...]
    conv = cw[0:1] * z2 + cw[1:2] * z1 + cw[2:3] * z
    out_b = cb_ref[...] * conv * _silu(gb_ref[...])
    if nb == 1:
        zt_ref[0] = z[tm - SUBLANES:, :]
    else:
        zt_ref[...] = z.reshape(nb, tq, D_BR)
    if o_grouped:
        o = jnp.concatenate([o_ref[h] for h in range(H_C)], axis=1)
    else:
        o = o_ref[...]
    out_c = o * _silu(gc_ref[...])
    mixed = (_sigmoid(ma_ref[...]) * jnp.dot(out_a.astype(BF16), wbr_ref[0], preferred_element_type=F32)
             + _sigmoid(mb_ref[...]) * jnp.dot(out_b.astype(BF16), wbr_ref[1], preferred_element_type=F32)
             + _sigmoid(mc_ref[...]) * jnp.dot(out_c.astype(BF16), wbr_ref[2], preferred_element_type=F32))
    res = jnp.dot(mixed.astype(BF16), wout_ref[...], preferred_element_type=F32)
    xo_ref[...] = x_ref[...] + _rms(res, np_ref[...])


def _merge(y, vkr, bonus, x, o, proj, halo_c, halo_x, lw, nb, tq):
    t = x.shape[0]
    y_grouped, o_grouped = y.ndim == 3, o.ndim == 3
    tm = nb * tq
    nseg = t // tq
    row = pl.BlockSpec((tm, D_BR), lambda i: (i, 0))
    prow = lambda cb: pl.BlockSpec((tm, D_BR), lambda i, cb=cb: (i, cb))
    halo = pl.BlockSpec((nb, 2, D_BR), lambda i: (i, 0, 0))
    vec = pl.BlockSpec((1, D_BR), lambda i: (0, 0))
    grouped = pl.BlockSpec((N_LANE_GROUPS, tm, LANES), lambda i: (0, i, 0))
    zt_rows = min(tq, SUBLANES)
    return pl.pallas_call(
        functools.partial(_merge_kernel, nb, tq, y_grouped, o_grouped),
        grid=(t // tm,),
        in_specs=[grouped if y_grouped else row, row, row, row, grouped if o_grouped else row] + [prow(c) for c in range(3, 12)] + [halo, halo, vec, vec,
                  pl.BlockSpec((3, D_BR), lambda i: (0, 0)), vec,
                  pl.BlockSpec((3, D_BR, D_MODEL), lambda i: (0, 0, 0)),
                  pl.BlockSpec((D_MODEL, D_MODEL), lambda i: (0, 0))],
        out_specs=[row, pl.BlockSpec((nb, zt_rows, D_BR), lambda i: (i, 0, 0))],
        out_shape=[jax.ShapeDtypeStruct((t, D_MODEL), F32),
                   jax.ShapeDtypeStruct((nseg, zt_rows, D_BR), F32)],
        compiler_params=_cparams(("parallel",)),
    )(y, vkr, bonus, x, o, *([proj] * 9), halo_c, halo_x, lw["ln_x_w"], lw["ln_x_b"], lw["conv_w"],
      lw["norm_post"], lw["w_branch"], lw["w_out"])


def _layer_weights(l, p):
    w_in = p["w_in"][l]
    z = lambda n: jnp.zeros((D_MODEL, n), F32)
    o_ga = 3 * D_BR + DECAY_LORA + ICLR_LORA
    o_qa = o_ga + 5 * D_BR
    o_kpe = o_qa + Q_LORA + KV_LORA
    o_gc = o_kpe + QK_ROPE
    w_re = jnp.concatenate([
        w_in[:, :3 * D_BR], w_in[:, o_ga:o_qa], w_in[:, o_gc:o_gc + 4 * D_BR],
        w_in[:, o_qa:o_kpe], w_in[:, 3 * D_BR:o_ga],
        z(ROPE_LANE0), w_in[:, o_kpe:o_gc], z(LANES - ROPE_LANE0 - QK_ROPE),
        z(N_PROJ - COL_KPE - LANES)], axis=1).astype(BF16)
    mu = p["mu_shift"][l]
    r2 = lambda a: a.reshape(1, -1)
    w_uq = p["w_uq"][l]
    wq = jnp.concatenate([w_uq, jnp.zeros((Q_LORA, H_C, LANES - QK_NOPE - QK_ROPE), F32)], axis=2)
    w_uk = p["w_uk"][l]
    wk = jnp.concatenate([w_uk, jnp.zeros((KV_LORA, H_C, LANES - QK_NOPE), F32)], axis=2)
    wuk_abs = jnp.concatenate([jnp.transpose(w_uk, (1, 2, 0)),
                               jnp.zeros((H_C, LANES - QK_NOPE, KV_LORA), F32)], axis=1)
    sel = np.zeros((LANES, QK_ROPE), np.float32)
    sel[ROPE_LANE0 + np.arange(QK_ROPE), np.arange(QK_ROPE)] = 1.0
    zl = jnp.zeros((DECAY_LORA, D_BR), F32)
    return {
        "w_in": w_re, "norm_pre": r2(p["norm_pre"][l]), "norm_post": r2(p["norm_post"][l]),
        "mu_r": r2(mu[:D_BR]), "mu_k": r2(mu[D_BR:2 * D_BR]), "mu_v": r2(mu[2 * D_BR:3 * D_BR]),
        "mu_lora": r2(mu[3 * D_BR:]),
        "w0": r2(p["w0"][l]), "a0": r2(p["a0"][l]),
        "w2p": jnp.concatenate([p["w2"][l], zl], axis=0).astype(BF16),
        "a2p": jnp.concatenate([zl, p["a2"][l]], axis=0).astype(BF16),
        "k_k": r2(p["k_k"][l]), "k_a": r2(p["k_a"][l]), "r_k": r2(p["r_k"][l]),
        "ln_x_w": r2(p["ln_x_w"][l]), "ln_x_b": r2(p["ln_x_b"][l]), "conv_w": p["conv_w"][l],
        "q_a_norm": r2(p["q_a_norm"][l]), "kv_a_norm": r2(p["kv_a_norm"][l]),
        "wq": wq.reshape(Q_LORA, H_C * LANES).astype(BF16),
        "wk": wk.reshape(KV_LORA, H_C * LANES).astype(BF16),
        "wv": p["w_uv"][l].reshape(KV_LORA, H_C * V_HEAD).astype(BF16),
        "wuk_abs": wuk_abs.astype(BF16),
        "wv_heads": jnp.transpose(p["w_uv"][l], (1, 0, 2)).astype(BF16),
        "sel_rope": jnp.asarray(sel, BF16),
        "w_branch": p["w_branch"][l].astype(BF16), "w_out": p["w_out"][l].astype(BF16),
    }


def _rope_tables(pos):
    half = QK_ROPE // 2
    freqs = ROPE_THETA ** (-jnp.arange(half, dtype=F32) / half)
    ang = pos.astype(F32)[:, None] * freqs[None, :]
    cos, sin = jnp.cos(ang), jnp.sin(ang)
    n = pos.shape[0]
    lo = jnp.zeros((n, ROPE_LANE0), F32)
    hi = jnp.zeros((n, LANES - ROPE_LANE0 - QK_ROPE), F32)
    zh = jnp.zeros((n, half), F32)
    ck = jnp.concatenate([lo, cos, cos, hi], axis=1)
    cq = jnp.concatenate([lo + 1.0, cos, cos, hi], axis=1)
    s1 = jnp.concatenate([lo, zh, sin, hi], axis=1)
    s2 = jnp.concatenate([lo, -sin, zh, hi], axis=1)
    return cq, ck, s1, s2


def _segment_edges(proj, nseq, tseq, tq):
    if tseq // tq == 1:
        return None
    edges = proj.reshape(nseq, tseq // tq, tq, N_PROJ)[:, :-1, tq - 2:, :]
    return lax.optimization_barrier(edges)


def _prev_row_halos(edges, first_rows, nseq, tseq, tq):
    nseg_per = tseq // tq
    out = {}
    for name, (c0, c1) in {"r": (COL_R, COL_R + D_BR), "k": (COL_K, COL_K + D_BR),
                           "v": (COL_V, COL_V + D_BR), "lora": (COL_LORA, COL_LORA + LANES)}.items():
        first = first_rows[name].reshape(nseq, 1, c1 - c0)
        if nseg_per > 1:
            first = jnp.concatenate([first, edges[:, :, 1, c0:c1]], axis=1)
        out[name] = first.reshape(nseq * nseg_per, 1, c1 - c0)
    return out


def _conv_halos(edges, conv_prev, nseq, tseq, tq):
    nseg_per = tseq // tq
    hc = conv_prev
    hx = jnp.ones_like(conv_prev)
    if nseg_per > 1:
        hc = jnp.concatenate([hc[:, None], edges[..., COL_CC:COL_CC + D_BR]], axis=1)
        hx = jnp.concatenate([hx[:, None], edges[..., COL_CX:COL_CX + D_BR]], axis=1)
    return hc.reshape(nseq * nseg_per, 2, D_BR), hx.reshape(nseq * nseg_per, 2, D_BR)


def _layer(x, lw, tabs, nseq, tseq, shift_prev, conv_prev, wkv_prev, attn):
    t = nseq * tseq
    proj = _inproj(x, lw["norm_pre"], lw["w_in"])
    if tseq >= ROW_TILE:
        nb, tq = 1, ROW_TILE
    else:
        nb, tq = min(nseq, ROW_TILE // tseq), tseq
    first = {"r": shift_prev[:, :D_BR], "k": shift_prev[:, D_BR:2 * D_BR],
             "v": shift_prev[:, 2 * D_BR:3 * D_BR], "lora": shift_prev[:, 3 * D_BR:]}
    chunked = tseq % WKV_CHUNK == 0
    edges = _segment_edges(proj, nseq, tseq, tq)
    ops = _rwkv_prep(proj, _prev_row_halos(edges, first, nseq, tseq, tq), lw, nb, tq, chunked)
    if chunked:
        bonus = ops[6]
        vkr = bonus
        w6 = wkv_prev.reshape(nseq, N_LANE_GROUPS, 2, HEAD_A, HEAD_A)
        s0 = jnp.zeros((nseq, N_LANE_GROUPS, 2, HEAD_A, 2, HEAD_A), F32)
        s0 = s0.at[:, :, 0, :, 0, :].set(w6[:, :, 0]).at[:, :, 1, :, 1, :].set(w6[:, :, 1])
        y, st = _wkv_chunked(ops[:6], s0.reshape(nseq, N_LANE_GROUPS, LANES, LANES), nseq, tseq)
        st6 = st.reshape(nseq, N_LANE_GROUPS, 2, HEAD_A, 2, HEAD_A)
        wkv_new = jnp.stack([st6[:, :, 0, :, 0, :], st6[:, :, 1, :, 1, :]], axis=2).reshape(
            nseq, H_A, HEAD_A, HEAD_A)
    else:
        vkr, bonus = ops[7], ops[8]
        s0 = jnp.transpose(wkv_prev, (0, 2, 1, 3)).reshape(nseq, HEAD_A, D_BR)
        y, st = _wkv_scan(ops[:7], s0, nseq, tseq)
        wkv_new = jnp.transpose(st.reshape(nseq, HEAD_A, H_A, HEAD_A), (0, 2, 1, 3))
    o, ckv, kr128 = attn(proj, lw, tabs)
    hc, hx = _conv_halos(edges, conv_prev, nseq, tseq, tq)
    x_new, zt = _merge(y, vkr, bonus, x, o, proj, hc, hx, lw, nb, tq)
    p3 = proj.reshape(nseq, tseq, N_PROJ)[:, -1, :]
    new_shift = jnp.concatenate([p3[:, :3 * D_BR], p3[:, COL_LORA:COL_LORA + LANES]], axis=1)
    new_conv = zt.reshape(nseq, tseq // tq, -1, D_BR)[:, -1, -2:, :]
    kr = kr128[:, ROPE_LANE0:ROPE_LANE0 + QK_ROPE]
    return x_new, (ckv.reshape(nseq, tseq, KV_LORA), kr.reshape(nseq, tseq, QK_ROPE), wkv_new,
                   new_shift, new_conv)


def kernel(x_prompt, x_sample, cache_ckv, cache_krope, state_wkv, state_shift, state_conv, page_table,
           norm_pre, norm_post, w_in, mu_shift, w0, w2, a0, a2, k_k, k_a, r_k, ln_x_w, ln_x_b,
           conv_w, q_a_norm, w_uq, kv_a_norm, w_uk, w_uv, w_branch, w_out):
    params = dict(norm_pre=norm_pre, norm_post=norm_post, w_in=w_in, mu_shift=mu_shift, w0=w0, w2=w2,
                  a0=a0, a2=a2, k_k=k_k, k_a=k_a, r_k=r_k.reshape(r_k.shape[0], -1), ln_x_w=ln_x_w,
                  ln_x_b=ln_x_b, conv_w=conv_w, q_a_norm=q_a_norm, w_uq=w_uq, kv_a_norm=kv_a_norm,
                  w_uk=w_uk, w_uv=w_uv, w_branch=w_branch, w_out=w_out)
    bp, sp, _ = x_prompt.shape
    bd, sd, _ = x_sample.shape
    depth = w_in.shape[0]
    past_len = page_table.shape[1] * cache_ckv.shape[2]
    tabs_p = _rope_tables(jnp.arange(sp))
    tm_s = min(bd * sd, 256)
    tabs_s = _rope_tables(past_len + (jnp.arange(tm_s) % sd))
    cache_krope = jnp.swapaxes(cache_krope, 2, 3)

    xp = x_prompt.reshape(bp * sp, D_MODEL)
    xs = x_sample.reshape(bd * sd, D_MODEL)
    acc_p, acc_s = [], []
    for l in range(depth):
        lw = _layer_weights(l, params)

        def attn_prompt(proj, lw, tabs):
            q, k, v, ckv, kr128 = _mla_prep_prompt(proj, lw, tabs, sp)
            return _flash_prompt(q, k, v, bp, sp), ckv, kr128

        def attn_sample(proj, lw, tabs, l=l):
            ql, qp, ckv, kr128 = _mla_prep_sample(proj, lw, tabs)
            o = _paged_attention(ql, qp, cache_ckv, cache_krope, l, ckv, kr128, lw["wv_heads"],
                                 page_table, sd)
            return o, ckv, kr128

        xp, st_p = _layer(xp, lw, tabs_p, bp, sp, jnp.zeros((bp, state_shift.shape[2]), F32),
                          jnp.zeros((bp, 2, D_BR), F32), jnp.zeros((bp, H_A, HEAD_A, HEAD_A), F32),
                          attn_prompt)
        xs, st_s = _layer(xs, lw, tabs_s, bd, sd, state_shift[l], state_conv[l], state_wkv[l],
                          attn_sample)
        acc_p.append(st_p)
        acc_s.append(st_s)
    outs_p = [jnp.stack([a[i] for a in acc_p], axis=0) for i in range(5)]
    outs_s = [jnp.stack([a[i] for a in acc_s], axis=0) for i in range(5)]
    return (xp.reshape(bp, sp, D_MODEL), xs.reshape(bd, sd, D_MODEL), *outs_p, *outs_s)
```

```python
import functools
import math

import numpy as np
import jax
import jax.numpy as jnp
from jax import lax
from jax.experimental import pallas as pl
from jax.experimental.pallas import tpu as pltpu

F32 = jnp.float32
BF16 = jnp.bfloat16

D_MODEL = 1024
D_BR = D_MODEL
HEAD_A = 64
H_A = D_BR // HEAD_A
DECAY_LORA = 64
ICLR_LORA = 64
GN_EPS = 64e-5
H_C = 8
QK_NOPE = 64
QK_ROPE = 32
V_HEAD = D_BR // H_C
Q_LORA = D_MODEL // 4
KV_LORA = D_MODEL // 4
ROPE_THETA = 10000.0
NORM_EPS = 1e-6
LANES = 128
SUBLANES = 8
N_LANE_GROUPS = D_BR // LANES
ROPE_LANE0 = 64
VMEM_LIMIT = 56 * 1024 * 1024
ROW_TILE = 256

COL_R, COL_K, COL_V, COL_GA, COL_CB, COL_CC, COL_CX, COL_GB, COL_GC, COL_MA, COL_MB, COL_MC = (
    i * D_BR for i in range(12))
COL_QA = 12 * D_BR
COL_KVA = COL_QA + Q_LORA
COL_LORA = COL_KVA + KV_LORA
COL_KPE = COL_LORA + LANES
N_PROJ = 13 * D_BR


def _cparams(sem):
    return pltpu.CompilerParams(dimension_semantics=sem, vmem_limit_bytes=VMEM_LIMIT)


def _head_ones():
    r = lax.broadcasted_iota(jnp.int32, (LANES, LANES), 0) // HEAD_A
    c = lax.broadcasted_iota(jnp.int32, (LANES, LANES), 1) // HEAD_A
    return (r == c).astype(BF16)


def _split_dot(x, w):
    hi = x.astype(BF16)
    lo = (x - hi.astype(F32)).astype(BF16)
    return (jnp.dot(hi, w, preferred_element_type=F32) + jnp.dot(lo, w, preferred_element_type=F32))


def _split_dot_rhs(w, x):
    hi = x.astype(BF16)
    lo = (x - hi.astype(F32)).astype(BF16)
    return (jnp.dot(w, hi, preferred_element_type=F32) + jnp.dot(w, lo, preferred_element_type=F32))


def _head_sum(x, hm):
    parts = [_split_dot(x[:, g * LANES:(g + 1) * LANES], hm) for g in range(x.shape[1] // LANES)]
    return jnp.concatenate(parts, axis=1)


def _shift_rows(p, halo, k, nb, tq):
    c = p.shape[1]
    prev = pltpu.roll(p, k, axis=0)
    pos = lax.broadcasted_iota(jnp.int32, p.shape, 0) % tq
    out = prev
    for j in range(k):
        hb = jnp.broadcast_to(halo[:, j:j + 1, :], (nb, tq, c)).reshape(nb * tq, c)
        out = jnp.where(pos == j, hb, out)
    return out


def _sigmoid(x):
    return 1.0 / (1.0 + jnp.exp(-x))


def _silu(x):
    return x * _sigmoid(x)


def _rms(x, g):
    return x * lax.rsqrt(jnp.mean(x * x, axis=-1, keepdims=True) + NORM_EPS) * g


def _inproj_kernel(x_ref, g_ref, w_ref, o_ref, h_ref):
    @pl.when(pl.program_id(1) == 0)
    def _():
        h_ref[...] = _rms(x_ref[...], g_ref[...]).astype(BF16)

    o_ref[...] = jnp.dot(h_ref[...], w_ref[...], preferred_element_type=F32)


def _inproj(x, g, w):
    t = x.shape[0]
    tm = min(t, 1024)
    tn = 1024
    return pl.pallas_call(
        _inproj_kernel,
        grid=(t // tm, N_PROJ // tn),
        in_specs=[pl.BlockSpec((tm, D_MODEL), lambda i, j: (i, 0)),
                  pl.BlockSpec((1, D_MODEL), lambda i, j: (0, 0)),
                  pl.BlockSpec((D_MODEL, tn), lambda i, j: (0, j))],
        out_specs=pl.BlockSpec((tm, tn), lambda i, j: (i, j)),
        out_shape=jax.ShapeDtypeStruct((t, N_PROJ), F32),
        scratch_shapes=[pltpu.VMEM((tm, D_MODEL), BF16)],
        compiler_params=_cparams(("parallel", "arbitrary")),
    )(x, g, w)


def _rwkv_prep_kernel(nb, tq, chunked, r_ref, k_ref, v_ref, lo_ref, hr_ref, hk_ref, hv_ref, hl_ref,
                      mur_ref, muk_ref, muv_ref, mul_ref, w0_ref, w2_ref, a0_ref, a2_ref,
                      kk_ref, ka_ref, rk_ref, *outs):
    hm = _head_ones()

    def mix(ref, halo_ref, mu_ref):
        p = ref[...]
        prev = _shift_rows(p, halo_ref[...], 1, nb, tq)
        return p + (prev - p) * mu_ref[...]

    r = mix(r_ref, hr_ref, mur_ref)
    k = mix(k_ref, hk_ref, muk_ref)
    v = mix(v_ref, hv_ref, muv_ref)
    lo = mix(lo_ref, hl_ref, mul_ref)
    wf = w0_ref[...] + jnp.dot(jnp.tanh(lo).astype(BF16), w2_ref[...], preferred_element_type=F32)
    log_decay = -math.exp(-0.5) * _sigmoid(wf)
    a = _sigmoid(a0_ref[...] + jnp.dot(lo.astype(BF16), a2_ref[...], preferred_element_type=F32))
    kk = k * kk_ref[...]
    kk = kk / jnp.maximum(jnp.sqrt(_head_sum(kk * kk, hm)), 1e-12)
    kmod = k * (1.0 + (a - 1.0) * ka_ref[...])
    b = kk * a
    bonus = _head_sum(r * kmod * rk_ref[...], hm) * v
    if chunked:
        *group_outs, bonus_o = outs
        bonus_o[...] = bonus
        for o_ref, val in zip(group_outs, (r, log_decay, kmod, v, -kk, b)):
            for g in range(N_LANE_GROUPS):
                o_ref[g] = val[:, g * LANES:(g + 1) * LANES]
        return
    wr_o, a_o, w_o, b_o, k_o, v_o, br_o, vkr_o, bonus_o = outs
    decay = jnp.exp(log_decay)
    wr_o[...] = decay * r
    a_o[...] = -kk
    w_o[...] = decay
    b_o[...] = b
    k_o[...] = kmod
    v_o[...] = v
    br_o[...] = _head_sum(b * r, hm)
    vkr_o[...] = _head_sum(kmod * r, hm) * v
    bonus_o[...] = bonus


def _rwkv_prep(proj, halos, lw, nb, tq, chunked):
    t = proj.shape[0]
    tm = nb * tq
    row_out = pl.BlockSpec((tm, D_BR), lambda i: (i, 0))
    if chunked:
        out_specs = [pl.BlockSpec((N_LANE_GROUPS, tm, LANES), lambda i: (0, i, 0))] * 6 + [row_out]
        outs = [jax.ShapeDtypeStruct((N_LANE_GROUPS, t, LANES), F32)] * 6 + [
            jax.ShapeDtypeStruct((t, D_BR), F32)]
    else:
        out_specs = [row_out] * 9
        outs = [jax.ShapeDtypeStruct((t, D_BR), F32)] * 9
    row = lambda cb: pl.BlockSpec((tm, D_BR), lambda i, cb=cb: (i, cb))
    halo = lambda c: pl.BlockSpec((nb, 1, c), lambda i: (i, 0, 0))
    vec = lambda c: pl.BlockSpec((1, c), lambda i: (0, 0))
    mat = pl.BlockSpec((LANES, D_BR), lambda i: (0, 0))
    return pl.pallas_call(
        functools.partial(_rwkv_prep_kernel, nb, tq, chunked),
        grid=(t // tm,),
        in_specs=[row(0), row(1), row(2),
                  pl.BlockSpec((tm, LANES), lambda i: (i, COL_LORA // LANES)),
                  halo(D_BR), halo(D_BR), halo(D_BR), halo(LANES),
                  vec(D_BR), vec(D_BR), vec(D_BR), vec(LANES),
                  vec(D_BR), mat, vec(D_BR), mat, vec(D_BR), vec(D_BR), vec(D_BR)],
        out_specs=out_specs,
        out_shape=outs,
        compiler_params=_cparams(("parallel",)),
    )(proj, proj, proj, proj, halos["r"], halos["k"], halos["v"], halos["lora"],
      lw["mu_r"], lw["mu_k"], lw["mu_v"], lw["mu_lora"], lw["w0"], lw["w2p"], lw["a0"], lw["a2p"],
      lw["k_k"], lw["k_a"], lw["r_k"])


def _wkv_kernel(ns, tb, wr_ref, a_ref, w_ref, b_ref, k_ref, v_ref, br_ref, s0_ref,
                y_ref, st_ref, p_ref):
    @pl.when(pl.program_id(1) == 0)
    def _():
        st_ref[...] = s0_ref[...]

    hm = _head_ones()
    ri = lax.broadcasted_iota(jnp.int32, (HEAD_A, LANES), 0)
    ci = lax.broadcasted_iota(jnp.int32, (HEAD_A, LANES), 1)
    diag = ((ci % HEAD_A) == ri).astype(F32)

    er = lax.broadcasted_iota(jnp.int32, (SUBLANES, HEAD_A), 0)
    ec = lax.broadcasted_iota(jnp.int32, (SUBLANES, HEAD_A), 1)
    tok_sum = ((ec // SUBLANES) == er).astype(BF16)

    def step(u, carry, base):
        t = base + u
        for s in range(ns):
            row = lambda ref: ref[s, pl.ds(t, 1), :]
            wr, a, w, b, k, v, br = (row(x) for x in (wr_ref, a_ref, w_ref, b_ref, k_ref, v_ref, br_ref))
            for g in range(N_LANE_GROUPS):
                sl = slice(g * LANES, (g + 1) * LANES)
                st = st_ref[s, :, sl]
                m1 = st * a[:, sl]
                m1_hi = m1.astype(BF16)
                m1_lo = (m1 - m1_hi.astype(F32)).astype(BF16)
                m2 = (st * wr[:, sl]).astype(BF16)
                m3 = (diag * v[:, sl]).astype(BF16)
                res = jnp.dot(jnp.concatenate([m1_hi, m1_lo, m2, m3], axis=0), hm,
                              preferred_element_type=F32)
                sa = res[0:HEAD_A] + res[HEAD_A:2 * HEAD_A]
                y2 = res[2 * HEAD_A:3 * HEAD_A]
                vc = res[3 * HEAD_A:4 * HEAD_A]
                st_ref[s, :, sl] = st * w[:, sl] + sa * b[:, sl] + vc * k[:, sl]
                yd = (y2 + sa * br[:, sl]) * diag
                p_ref[s, u, :, sl] = jnp.sum(yd.reshape(HEAD_A // SUBLANES, SUBLANES, LANES), axis=0)
        return carry

    def block8(t8, carry):
        base = pl.multiple_of(t8 * SUBLANES, SUBLANES)
        lax.fori_loop(0, SUBLANES, functools.partial(step, base=base), 0)
        for s in range(ns):
            y_ref[s, pl.ds(base, SUBLANES), :] = _split_dot_rhs(
                tok_sum, p_ref[s].reshape(SUBLANES * SUBLANES, D_BR))
        return carry

    lax.fori_loop(0, tb // SUBLANES, block8, 0)


def _wkv_scan(ops, s0, nseq, tseq):
    ns = 8 if nseq % 8 == 0 else 2
    tb = min(tseq, 128)
    ops3 = [o.reshape(nseq, tseq, D_BR) for o in ops]
    blk = pl.BlockSpec((ns, tb, D_BR), lambda i, j: (i, j, 0))
    sblk = pl.BlockSpec((ns, HEAD_A, D_BR), lambda i, j: (i, 0, 0))
    y, st = pl.pallas_call(
        functools.partial(_wkv_kernel, ns, tb),
        grid=(nseq // ns, tseq // tb),
        in_specs=[blk] * 7 + [sblk],
        out_specs=[blk, sblk],
        out_shape=[jax.ShapeDtypeStruct((nseq, tseq, D_BR), F32),
                   jax.ShapeDtypeStruct((nseq, HEAD_A, D_BR), F32)],
        scratch_shapes=[pltpu.VMEM((ns, SUBLANES, SUBLANES, D_BR), F32)],
        compiler_params=_cparams(("parallel", "arbitrary")),
    )(*ops3, s0)
    return y.reshape(nseq * tseq, D_BR), st


WKV_CHUNK = 64
WKV_PASSES_LOCAL = 1
WKV_PASSES_STATE = 3


def _dot_nn(a, b):
    return jnp.dot(a, b, preferred_element_type=F32)


def _dot_nt(a, b):
    return lax.dot_general(a, b, (((1,), (1,)), ((), ())), preferred_element_type=F32)


def _dot_tn(a, b):
    return lax.dot_general(a, b, (((0,), (0,)), ((), ())), preferred_element_type=F32)


def _hilo(x):
    hi = x.astype(BF16)
    return hi, (x - hi.astype(F32)).astype(BF16)


def _mm(dot, a, b, passes):
    ah, al = _hilo(a)
    bh, bl = _hilo(b)
    out = dot(ah, bh)
    if passes >= 3:
        out = out + dot(al, bh) + dot(ah, bl)
    return out


def _wkv_chunk_kernel(ns, r_ref, lw_ref, k_ref, v_ref, a_ref, b_ref, s0_ref, y_ref, st_ref):
    L = r_ref.shape[2]

    @pl.when(pl.program_id(1) == 0)
    def _():
        st_ref[...] = s0_ref[...]

    ti = lax.broadcasted_iota(jnp.int32, (L, L), 0)
    si = lax.broadcasted_iota(jnp.int32, (L, L), 1)
    incl = si <= ti
    strict = si < ti
    eye_l = (si == ti).astype(F32)
    tril_ones = incl.astype(BF16)
    levels = []
    m = 1
    while m < L:
        levels.append(((ti // (2 * m)) == (si // (2 * m))) & ((ti % (2 * m)) >= m) & ((si % (2 * m)) < m))
        m *= 2
    lane = lax.broadcasted_iota(jnp.int32, (1, LANES), 1)
    head_masks = [(lane < HEAD_A).astype(F32), (lane >= HEAD_A).astype(F32)]
    gr = lax.broadcasted_iota(jnp.int32, (LANES, LANES), 0)
    gc = lax.broadcasted_iota(jnp.int32, (LANES, LANES), 1)
    same_head = (gr // HEAD_A) == (gc // HEAD_A)
    eye_g = gr == gc

    def each(fn, *lists):
        return [fn(*xs) for xs in zip(*lists)]

    def mm(dot, xs, ys, passes=WKV_PASSES_LOCAL):
        return each(lambda x, y: _mm(dot, x, y, passes), xs, ys)

    chains = [(s, g) for s in range(ns) for g in range(N_LANE_GROUPS)]
    r, lw, k, v, a, b = ([ref[g, s] for s, g in chains]
                         for ref in (r_ref, lw_ref, k_ref, v_ref, a_ref, b_ref))
    st = [st_ref[s, g] for s, g in chains]

    def cumsum(x):
        hi, lo = _hilo(x)
        return _dot_nn(tril_ones, hi) + _dot_nn(tril_ones, lo)

    c = each(cumsum, lw)
    c_last = [x[L - 1:L, :] for x in c]
    e_neg = [jnp.exp(-x) for x in c]
    e_end = each(lambda cl, x: jnp.exp(cl - x), c_last, c)
    at = each(lambda a_, x, w_: a_ * jnp.exp(x - w_), a, c, lw)
    rt = each(lambda r_, x: r_ * jnp.exp(x), r, c)
    bt = each(jnp.multiply, b, e_neg)
    kt = each(jnp.multiply, k, e_neg)
    bend = each(jnp.multiply, b, e_end)
    kend = each(jnp.multiply, k, e_end)

    def per_head(xs):
        return [x * hm_ for x in xs for hm_ in head_masks]

    def per_group(xs):
        return [x for x in xs for _ in head_masks]

    at_h, rt_h, v_h = per_head(at), per_head(rt), per_head(v)
    q2 = each(lambda x, y: jnp.concatenate([x, y], axis=0), at_h, rt_h)
    gb = mm(_dot_nt, q2, per_group(bt))
    gk = mm(_dot_nt, q2, per_group(kt))
    n = [jnp.where(strict, x[:L], 0.0) for x in gb]
    pb = [jnp.where(incl, x[L:], 0.0) for x in gb]
    m_ = [jnp.where(strict, x[:L], 0.0) for x in gk]
    pk = [jnp.where(incl, x[L:], 0.0) for x in gk]
    t_inv = [eye_l + jnp.where(levels[0], x, 0.0) for x in n]
    for lev in levels[1:]:
        cm = [jnp.where(lev, x, 0.0) for x in n]
        t_inv = each(jnp.add, t_inv, mm(_dot_nn, mm(_dot_nn, t_inv, cm), t_inv))
    abar_h = mm(_dot_nn, t_inv, at_h)
    uv_h = mm(_dot_nn, mm(_dot_nn, t_inv, m_), v_h)
    rbar_h = mm(_dot_nn, pb, abar_h)
    yv_h = each(jnp.add, mm(_dot_nn, pb, uv_h), mm(_dot_nn, pk, v_h))

    def head_total(xs):
        return [xs[2 * i] + xs[2 * i + 1] for i in range(len(chains))]

    abar, uv, yv = head_total(abar_h), head_total(uv_h), head_total(yv_h)
    rbar = each(jnp.add, rt, head_total(rbar_h))
    phi = each(lambda x, cl: jnp.where(same_head, x, 0.0) + jnp.where(eye_g, jnp.exp(cl), 0.0),
               mm(_dot_tn, abar, bend), c_last)
    psi = each(lambda x, y: jnp.where(same_head, x + y, 0.0), mm(_dot_tn, uv, bend), mm(_dot_tn, v, kend))
    y = each(jnp.add, mm(_dot_nt, rbar, st, WKV_PASSES_STATE), yv)
    st_new = each(jnp.add, mm(_dot_nn, st, phi, WKV_PASSES_STATE), psi)
    for (s, g), y_, st_ in zip(chains, y, st_new):
        y_ref[g, s] = y_
        st_ref[s, g] = st_


def _wkv_chunked(ops, s0, nseq, tseq):
    ns = 2
    L = min(WKV_CHUNK, tseq)
    ops4 = [o.reshape(N_LANE_GROUPS, nseq, tseq, LANES) for o in ops]
    blk = pl.BlockSpec((N_LANE_GROUPS, ns, L, LANES), lambda i, j: (0, i, j, 0))
    sblk = pl.BlockSpec((ns, N_LANE_GROUPS, LANES, LANES), lambda i, j: (i, 0, 0, 0))
    y, st = pl.pallas_call(
        functools.partial(_wkv_chunk_kernel, ns),
        grid=(nseq // ns, tseq // L),
        in_specs=[blk] * 6 + [sblk],
        out_specs=[blk, sblk],
        out_shape=[jax.ShapeDtypeStruct((N_LANE_GROUPS, nseq, tseq, LANES), F32),
                   jax.ShapeDtypeStruct((nseq, N_LANE_GROUPS, LANES, LANES), F32)],
        compiler_params=_cparams(("parallel", "arbitrary")),
    )(*ops4, s0)
    return y.reshape(N_LANE_GROUPS, nseq * tseq, LANES), st


def _rope(x, c_ref, s1_ref, s2_ref):
    parts = []
    for h in range(x.shape[1] // LANES):
        xh = x[:, h * LANES:(h + 1) * LANES]
        half = QK_ROPE // 2
        parts.append(xh * c_ref[...] + pltpu.roll(xh, half, axis=1) * s1_ref[...]
                     + pltpu.roll(xh, LANES - half, axis=1) * s2_ref[...])
    return parts[0] if len(parts) == 1 else jnp.concatenate(parts, axis=1)


SOFTMAX_SCALE = (QK_NOPE + QK_ROPE) ** -0.5
LOG2E = math.log2(math.e)


def _mla_common(qa_ref, kva_ref, kpe_ref, qn_ref, kvn_ref, wq_ref, cq_ref, ck_ref, s1_ref, s2_ref,
                scale=SOFTMAX_SCALE):
    qn = _rms(qa_ref[...], qn_ref[...]).astype(BF16)
    q = jnp.dot(qn, wq_ref[...], preferred_element_type=F32)
    q = _rope(q, cq_ref, s1_ref, s2_ref) * scale
    ckv = _rms(kva_ref[...], kvn_ref[...])
    kr = _rope(kpe_ref[...], ck_ref, s1_ref, s2_ref)
    return q, ckv, kr


def _mla_prep_prompt_kernel(qa_ref, kva_ref, kpe_ref, qn_ref, kvn_ref, wq_ref, wk_ref, wv_ref,
                            cq_ref, ck_ref, s1_ref, s2_ref, q_o, k_o, v_o, ckv_o, kr_o):
    q, ckv, kr = _mla_common(qa_ref, kva_ref, kpe_ref, qn_ref, kvn_ref, wq_ref, cq_ref, ck_ref,
                             s1_ref, s2_ref, scale=SOFTMAX_SCALE * LOG2E)
    ckv_o[...] = ckv
    kr_o[...] = kr
    cb = ckv.astype(BF16)
    kn = jnp.dot(cb, wk_ref[...], preferred_element_type=F32)
    vv = jnp.dot(cb, wv_ref[...], preferred_element_type=F32)
    for h in range(H_C):
        sl = slice(h * LANES, (h + 1) * LANES)
        q_o[h] = q[:, sl].astype(BF16)
        k_o[h] = (kn[:, sl] + kr).T.astype(BF16)
        v_o[h] = vv[:, sl].astype(BF16)


def _small_specs(tm, npos_blocks):
    return [pl.BlockSpec((tm, Q_LORA), lambda i: (i, COL_QA // Q_LORA)),
            pl.BlockSpec((tm, KV_LORA), lambda i: (i, COL_KVA // KV_LORA)),
            pl.BlockSpec((tm, LANES), lambda i: (i, COL_KPE // LANES)),
            pl.BlockSpec((1, Q_LORA), lambda i: (0, 0)),
            pl.BlockSpec((1, KV_LORA), lambda i: (0, 0)),
            pl.BlockSpec((Q_LORA, H_C * LANES), lambda i: (0, 0))], \
           [pl.BlockSpec((tm, LANES), lambda i: (i % npos_blocks, 0))] * 4


def _mla_prep_prompt(proj, lw, tabs, seq):
    t = proj.shape[0]
    tm = min(seq, 512)
    head, tab = _small_specs(tm, seq // tm)
    wfull = pl.BlockSpec((KV_LORA, H_C * LANES), lambda i: (0, 0))
    row = lambda c: pl.BlockSpec((tm, c), lambda i: (i, 0))
    return pl.pallas_call(
        _mla_prep_prompt_kernel,
        grid=(t // tm,),
        in_specs=head + [wfull, wfull] + tab,
        out_specs=[pl.BlockSpec((H_C, tm, LANES), lambda i: (0, i, 0)),
                   pl.BlockSpec((H_C, LANES, tm), lambda i: (0, 0, i)),
                   pl.BlockSpec((H_C, tm, LANES), lambda i: (0, i, 0)), row(KV_LORA), row(LANES)],
        out_shape=[jax.ShapeDtypeStruct((H_C, t, LANES), BF16), jax.ShapeDtypeStruct((H_C, LANES, t), BF16),
                   jax.ShapeDtypeStruct((H_C, t, LANES), BF16)]
                  + [jax.ShapeDtypeStruct((t, KV_LORA), F32), jax.ShapeDtypeStruct((t, LANES), F32)],
        compiler_params=_cparams(("parallel",)),
    )(proj, proj, proj, lw["q_a_norm"], lw["kv_a_norm"], lw["wq"], lw["wk"], lw["wv"], *tabs)


def _mla_prep_sample_kernel(qa_ref, kva_ref, kpe_ref, qn_ref, kvn_ref, wq_ref, wuk_ref, sel_ref,
                            cq_ref, ck_ref, s1_ref, s2_ref, ql_o, qp_o, ckv_o, kr_o):
    q, ckv, kr = _mla_common(qa_ref, kva_ref, kpe_ref, qn_ref, kvn_ref, wq_ref, cq_ref, ck_ref,
                             s1_ref, s2_ref)
    ckv_o[...] = ckv
    kr_o[...] = kr
    qb = q.astype(BF16)
    for h in range(H_C):
        qh = qb[:, h * LANES:(h + 1) * LANES]
        ql_o[h] = jnp.dot(qh, wuk_ref[h], preferred_element_type=F32)
        qp_o[h] = jnp.dot(qh, sel_ref[...], preferred_element_type=F32)


def _mla_prep_sample(proj, lw, tabs):
    t = proj.shape[0]
    tm = min(t, 256)
    head, tab = _small_specs(tm, 1)
    row = lambda c: pl.BlockSpec((tm, c), lambda i: (i, 0))
    return pl.pallas_call(
        _mla_prep_sample_kernel,
        grid=(t // tm,),
        in_specs=head + [pl.BlockSpec((H_C, LANES, KV_LORA), lambda i: (0, 0, 0)),
                         pl.BlockSpec((LANES, QK_ROPE), lambda i: (0, 0))] + tab,
        out_specs=[pl.BlockSpec((H_C, tm, KV_LORA), lambda i: (0, i, 0)),
                   pl.BlockSpec((H_C, tm, QK_ROPE), lambda i: (0, i, 0)),
                   row(KV_LORA), row(LANES)],
        out_shape=[jax.ShapeDtypeStruct((H_C, t, KV_LORA), F32),
                   jax.ShapeDtypeStruct((H_C, t, QK_ROPE), F32),
                   jax.ShapeDtypeStruct((t, KV_LORA), F32), jax.ShapeDtypeStruct((t, LANES), F32)],
        compiler_params=_cparams(("parallel",)),
    )(proj, proj, proj, lw["q_a_norm"], lw["kv_a_norm"], lw["wq"], lw["wuk_abs"], lw["sel_rope"], *tabs)


NEG = -1e30


def _lane_fold(x, op):
    parts = [x[:, j * LANES:(j + 1) * LANES] for j in range(x.shape[1] // LANES)]
    while len(parts) > 1:
        parts = [op(parts[i], parts[i + 1]) for i in range(0, len(parts) - 1, 2)] + (
            [parts[-1]] if len(parts) % 2 else [])
    return parts[0]


def _flash_kernel(tq, sub, qi_ref, ki_ref, q_ref, kt_ref, v_ref, o_ref, m_ref, l_ref, acc_ref):
    step = pl.program_id(2)
    qi = qi_ref[step]
    ki = ki_ref[step]
    nsub = tq // sub

    @pl.when(ki == 0)
    def _():
        m_ref[...] = jnp.full(m_ref.shape, NEG, F32)
        l_ref[...] = jnp.zeros(l_ref.shape, F32)
        acc_ref[...] = jnp.zeros(acc_ref.shape, F32)

    def scores(r, diagonal):
        nk = (r + 1) * sub if diagonal else tq
        s = jnp.dot(q_ref[0, pl.ds(r * sub, sub), :], kt_ref[0, :, 0:nk], preferred_element_type=F32)
        if diagonal:
            qpos = r * sub + lax.broadcasted_iota(jnp.int32, s.shape, 0)
            kpos = lax.broadcasted_iota(jnp.int32, s.shape, 1)
            s = jnp.where(kpos <= qpos, s, NEG)
        return s

    def block(diagonal):
        s_next = scores(0, diagonal)
        for r in range(nsub):
            s = s_next
            if r + 1 < nsub:
                s_next = scores(r + 1, diagonal)
            rows = pl.ds(r * sub, sub)
            nk = s.shape[1]
            m_old = m_ref[rows, :]
            m_new = jnp.maximum(m_old, jnp.max(_lane_fold(s, jnp.maximum), axis=-1, keepdims=True))
            alpha = jnp.exp2(m_old - m_new)
            p = jnp.exp2(s - m_new)
            l_ref[rows, :] = alpha * l_ref[rows, :] + jnp.sum(_lane_fold(p, jnp.add), axis=-1,
                                                              keepdims=True)
            acc_ref[rows, :] = alpha * acc_ref[rows, :] + jnp.dot(
                p.astype(BF16), v_ref[0, 0:nk, :], preferred_element_type=F32)
            m_ref[rows, :] = m_new

    @pl.when(ki < qi)
    def _():
        block(False)

    @pl.when(ki == qi)
    def _():
        block(True)
        o_ref[0] = acc_ref[...] / l_ref[...]


def _flash_prompt(q, kt, v, nbatch, seq):
    tq = min(seq, 2048)
    nq = seq // tq
    pairs =[(a, c) for a in range(nq) for c in range(a + 1)]
    qi_tab = jnp.asarray([p[0] for p in pairs], jnp.int32)
    ki_tab = jnp.asarray([p[1] for p in pairs], jnp.int32)
    qspec = pl.BlockSpec((1, tq, LANES), lambda b, h, t, qt, kt: (h, b * nq + qt[t], 0))
    kspec = pl.BlockSpec((1, tq, LANES), lambda b, h, t, qt, kt: (h, b * nq + kt[t], 0))
    ktspec = pl.BlockSpec((1, LANES, tq), lambda b, h, t, qt, kt: (h, 0, b * nq + kt[t]))
    grid_spec = pltpu.PrefetchScalarGridSpec(
        num_scalar_prefetch=2,
        grid=(nbatch, H_C, len(pairs)),
        in_specs=[qspec, ktspec, kspec],
        out_specs=qspec,
        scratch_shapes=[pltpu.VMEM((tq, 1), F32), pltpu.VMEM((tq, 1), F32), pltpu.VMEM((tq, LANES), F32)],
    )
    return pl.pallas_call(
        functools.partial(_flash_kernel, tq, min(tq, 256)),
        grid_spec=grid_spec,
        out_shape=jax.ShapeDtypeStruct((H_C, nbatch * seq, LANES), F32),
        compiler_params=_cparams(("parallel", "parallel", "arbitrary")),
    )(qi_tab, ki_tab, q, kt, v)


PAGES_PER_STEP = 64


def _joint_softmax_update(tiles, values, m_ref, l_ref, acc_ref):
    def tree(xs, op):
        while len(xs) > 1:
            xs = [op(xs[i], xs[i + 1]) for i in range(0, len(xs) - 1, 2)] + (
                [xs[-1]] if len(xs) % 2 else [])
        return xs[0]

    m_old = m_ref[...]
    m_new = jnp.maximum(m_old, jnp.max(tree(list(tiles), jnp.maximum), axis=-1, keepdims=True))
    alpha = jnp.exp(m_old - m_new)
    ps = [jnp.exp(s - m_new) for s in tiles]
    pvs = [jnp.dot(p.astype(BF16), c, preferred_element_type=F32) for p, c in zip(ps, values)]
    l_ref[...] = alpha * l_ref[...] + jnp.sum(tree(ps, jnp.add), axis=-1, keepdims=True)
    acc_ref[...] = alpha * acc_ref[...] + tree(pvs, jnp.add)
    m_ref[...] = m_new


def _paged_kernel(snew, npg, pt_ref, ql_ref, qp_ref, *rest):
    ckv_refs = rest[:npg]
    kr_refs = rest[npg:2 * npg]
    cn_ref, kn_ref, wv_ref, o_ref, m_ref, l_ref, acc_ref = rest[2 * npg:]
    pg = pl.program_id(1)
    rows = H_C * snew
    ql = ql_ref[...].reshape(rows, KV_LORA).astype(BF16)
    qp = qp_ref[...].reshape(rows, QK_ROPE).astype(BF16)
    page = ckv_refs[0].shape[2]

    @pl.when(pg == 0)
    def _():
        m_ref[...] = jnp.full(m_ref.shape, NEG, F32)
        l_ref[...] = jnp.zeros(l_ref.shape, F32)
        acc_ref[...] = jnp.zeros(acc_ref.shape, F32)

    def scores(cs, krts):
        lat = [_dot_nt(ql, c) for c in cs]
        rot = [_dot_nn(qp, r) for r in krts]
        return [a + b for a, b in zip(lat, rot)]

    cs = [ckv_refs[j][0, 0].astype(BF16) for j in range(npg)]
    krts = [kr_refs[j][0, 0].astype(BF16) for j in range(npg)]
    _joint_softmax_update(scores(cs, krts), cs, m_ref, l_ref, acc_ref)

    @pl.when(pg == pl.num_programs(1) - 1)
    def _():
        pad = lambda x: jnp.concatenate(
            [x, jnp.zeros((page - snew, x.shape[1]), F32)], axis=0).astype(BF16)
        c = pad(cn_ref[...])
        kn_t = pad(kn_ref[...]).T[ROPE_LANE0:ROPE_LANE0 + QK_ROPE, :]
        s = scores([c], [kn_t])[0]
        tok = lax.broadcasted_iota(jnp.int32, s.shape, 0) % snew
        key = lax.broadcasted_iota(jnp.int32, s.shape, 1)
        s = jnp.where(key <= tok, s, NEG)
        _joint_softmax_update([s], [c], m_ref, l_ref, acc_ref)
        o_lat = acc_ref[...] / l_ref[...]
        for h in range(H_C):
            o_ref[:, h * V_HEAD:(h + 1) * V_HEAD] = jnp.dot(
                o_lat[h * snew:(h + 1) * snew].astype(BF16), wv_ref[h], preferred_element_type=F32)


def _paged_attention(ql, qp, cache_ckv, cache_kr, layer, ckv_new, kr_new, wv, page_table, snew):
    nb, n_pages = page_table.shape
    page = cache_ckv.shape[2]
    npg = min(PAGES_PER_STEP, n_pages)
    ckv_specs = [pl.BlockSpec((1, 1, page, KV_LORA),
                              lambda b, g, pt, j=j: (layer, pt[b, g * npg + j], 0, 0)) for j in range(npg)]
    kr_specs = [pl.BlockSpec((1, 1, QK_ROPE, page),
                             lambda b, g, pt, j=j: (layer, pt[b, g * npg + j], 0, 0)) for j in range(npg)]
    rows = H_C * snew
    grid_spec = pltpu.PrefetchScalarGridSpec(
        num_scalar_prefetch=1,
        grid=(nb, n_pages // npg),
        in_specs=[pl.BlockSpec((H_C, snew, KV_LORA), lambda b, g, pt: (0, b, 0)),
                  pl.BlockSpec((H_C, snew, QK_ROPE), lambda b, g, pt: (0, b, 0))]
                 + ckv_specs + kr_specs
                 + [pl.BlockSpec((snew, KV_LORA), lambda b, g, pt: (b, 0)),
                    pl.BlockSpec((snew, LANES), lambda b, g, pt: (b, 0)),
                    pl.BlockSpec((H_C, KV_LORA, V_HEAD), lambda b, g, pt: (0, 0, 0))],
        out_specs=pl.BlockSpec((snew, D_BR), lambda b, g, pt: (b, 0)),
        scratch_shapes=[pltpu.VMEM((rows, 1), F32), pltpu.VMEM((rows, 1), F32),
                        pltpu.VMEM((rows, KV_LORA), F32)],
    )
    return pl.pallas_call(
        functools.partial(_paged_kernel, snew, npg),
        grid_spec=grid_spec,
        out_shape=jax.ShapeDtypeStruct((nb * snew, D_BR), F32),
        compiler_params=_cparams(("parallel", "arbitrary")),
    )(page_table, ql, qp, *([cache_ckv] * npg), *([cache_kr] * npg), ckv_new, kr_new, wv)


def _merge_kernel(nb, tq, y_grouped, o_grouped, y_ref, vkr_ref, bonus_ref, x_ref, o_ref, ga_ref, cb_ref, cc_ref, cx_ref, gb_ref,
                  gc_ref, ma_ref, mb_ref, mc_ref, hc_ref, hx_ref, lnw_ref, lnb_ref, cw_ref, np_ref,
                  wbr_ref, wout_ref, xo_ref, zt_ref):
    hm = _head_ones()
    tm = nb * tq
    if y_grouped:
        y = jnp.concatenate([y_ref[g] for g in range(N_LANE_GROUPS)], axis=1)
    else:
        y = y_ref[...] + vkr_ref[...]
    mu = _head_sum(y, hm) * (1.0 / HEAD_A)
    yc = y - mu
    var = _head_sum(yc * yc, hm) * (1.0 / HEAD_A)
    out_a = (yc * lax.rsqrt(var + GN_EPS) * lnw_ref[...] + lnb_ref[...] + bonus_ref[...]) * _silu(ga_ref[...])
    z = cc_ref[...] * cx_ref[...]
    zh = hc_ref[...] * hx_ref[...]
    z1 = _shift_rows(z, zh[:, 1:2, :], 1, nb, tq)
    z2 = _shift_rows(z, zh, 2, nb, tq)
    cw = cw_ref[...]
    conv = cw[0:1] * z2 + cw[1:2] * z1 + cw[2:3] * z
    out_b = cb_ref[...] * conv * _silu(gb_ref[...])
    if nb == 1:
        zt_ref[0] = z[tm - SUBLANES:, :]
    else:
        zt_ref[...] = z.reshape(nb, tq, D_BR)
    if o_grouped:
        o = jnp.concatenate([o_ref[h] for h in range(H_C)], axis=1)
    else:
        o = o_ref[...]
    out_c = o * _silu(gc_ref[...])
    mixed = (_sigmoid(ma_ref[...]) * jnp.dot(out_a.astype(BF16), wbr_ref[0], preferred_element_type=F32)
             + _sigmoid(mb_ref[...]) * jnp.dot(out_b.astype(BF16), wbr_ref[1], preferred_element_type=F32)
             + _sigmoid(mc_ref[...]) * jnp.dot(out_c.astype(BF16), wbr_ref[2], preferred_element_type=F32))
    res = jnp.dot(mixed.astype(BF16), wout_ref[...], preferred_element_type=F32)
    xo_ref[...] = x_ref[...] + _rms(res, np_ref[...])


def _merge(y, vkr, bonus, x, o, proj, halo_c, halo_x, lw, nb, tq):
    t = x.shape[0]
    y_grouped, o_grouped = y.ndim == 3, o.ndim == 3
    tm = nb * tq
    nseg = t // tq
    row = pl.BlockSpec((tm, D_BR), lambda i: (i, 0))
    prow = lambda cb: pl.BlockSpec((tm, D_BR), lambda i, cb=cb: (i, cb))
    halo = pl.BlockSpec((nb, 2, D_BR), lambda i: (i, 0, 0))
    vec = pl.BlockSpec((1, D_BR), lambda i: (0, 0))
    grouped = pl.BlockSpec((N_LANE_GROUPS, tm, LANES), lambda i: (0, i, 0))
    zt_rows = min(tq, SUBLANES)
    return pl.pallas_call(
        functools.partial(_merge_kernel, nb, tq, y_grouped, o_grouped),
        grid=(t // tm,),
        in_specs=[grouped if y_grouped else row, row, row, row, grouped if o_grouped else row] + [prow(c) for c in range(3, 12)] + [halo, halo, vec, vec,
                  pl.BlockSpec((3, D_BR), lambda i: (0, 0)), vec,
                  pl.BlockSpec((3, D_BR, D_MODEL), lambda i: (0, 0, 0)),
                  pl.BlockSpec((D_MODEL, D_MODEL), lambda i: (0, 0))],
        out_specs=[row, pl.BlockSpec((nb, zt_rows, D_BR), lambda i: (i, 0, 0))],
        out_shape=[jax.ShapeDtypeStruct((t, D_MODEL), F32),
                   jax.ShapeDtypeStruct((nseg, zt_rows, D_BR), F32)],
        compiler_params=_cparams(("parallel",)),
    )(y, vkr, bonus, x, o, *([proj] * 9), halo_c, halo_x, lw["ln_x_w"], lw["ln_x_b"], lw["conv_w"],
      lw["norm_post"], lw["w_branch"], lw["w_out"])


def _layer_weights(l, p):
    w_in = p["w_in"][l]
    z = lambda n: jnp.zeros((D_MODEL, n), F32)
    o_ga = 3 * D_BR + DECAY_LORA + ICLR_LORA
    o_qa = o_ga + 5 * D_BR
    o_kpe = o_qa + Q_LORA + KV_LORA
    o_gc = o_kpe + QK_ROPE
    w_re = jnp.concatenate([
        w_in[:, :3 * D_BR], w_in[:, o_ga:o_qa], w_in[:, o_gc:o_gc + 4 * D_BR],
        w_in[:, o_qa:o_kpe], w_in[:, 3 * D_BR:o_ga],
        z(ROPE_LANE0), w_in[:, o_kpe:o_gc], z(LANES - ROPE_LANE0 - QK_ROPE),
        z(N_PROJ - COL_KPE - LANES)], axis=1).astype(BF16)
    mu = p["mu_shift"][l]
    r2 = lambda a: a.reshape(1, -1)
    w_uq = p["w_uq"][l]
    wq = jnp.concatenate([w_uq, jnp.zeros((Q_LORA, H_C, LANES - QK_NOPE - QK_ROPE), F32)], axis=2)
    w_uk = p["w_uk"][l]
    wk = jnp.concatenate([w_uk, jnp.zeros((KV_LORA, H_C, LANES - QK_NOPE), F32)], axis=2)
    wuk_abs = jnp.concatenate([jnp.transpose(w_uk, (1, 2, 0)),
                               jnp.zeros((H_C, LANES - QK_NOPE, KV_LORA), F32)], axis=1)
    sel = np.zeros((LANES, QK_ROPE), np.float32)
    sel[ROPE_LANE0 + np.arange(QK_ROPE), np.arange(QK_ROPE)] = 1.0
    zl = jnp.zeros((DECAY_LORA, D_BR), F32)
    return {
        "w_in": w_re, "norm_pre": r2(p["norm_pre"][l]), "norm_post": r2(p["norm_post"][l]),
        "mu_r": r2(mu[:D_BR]), "mu_k": r2(mu[D_BR:2 * D_BR]), "mu_v": r2(mu[2 * D_BR:3 * D_BR]),
        "mu_lora": r2(mu[3 * D_BR:]),
        "w0": r2(p["w0"][l]), "a0": r2(p["a0"][l]),
        "w2p": jnp.concatenate([p["w2"][l], zl], axis=0).astype(BF16),
        "a2p": jnp.concatenate([zl, p["a2"][l]], axis=0).astype(BF16),
        "k_k": r2(p["k_k"][l]), "k_a": r2(p["k_a"][l]), "r_k": r2(p["r_k"][l]),
        "ln_x_w": r2(p["ln_x_w"][l]), "ln_x_b": r2(p["ln_x_b"][l]), "conv_w": p["conv_w"][l],
        "q_a_norm": r2(p["q_a_norm"][l]), "kv_a_norm": r2(p["kv_a_norm"][l]),
        "wq": wq.reshape(Q_LORA, H_C * LANES).astype(BF16),
        "wk": wk.reshape(KV_LORA, H_C * LANES).astype(BF16),
        "wv": p["w_uv"][l].reshape(KV_LORA, H_C * V_HEAD).astype(BF16),
        "wuk_abs": wuk_abs.astype(BF16),
        "wv_heads": jnp.transpose(p["w_uv"][l], (1, 0, 2)).astype(BF16),
        "sel_rope": jnp.asarray(sel, BF16),
        "w_branch": p["w_branch"][l].astype(BF16), "w_out": p["w_out"][l].astype(BF16),
    }


def _rope_tables(pos):
    half = QK_ROPE // 2
    freqs = ROPE_THETA ** (-jnp.arange(half, dtype=F32) / half)
    ang = pos.astype(F32)[:, None] * freqs[None, :]
    cos, sin = jnp.cos(ang), jnp.sin(ang)
    n = pos.shape[0]
    lo = jnp.zeros((n, ROPE_LANE0), F32)
    hi = jnp.zeros((n, LANES - ROPE_LANE0 - QK_ROPE), F32)
    zh = jnp.zeros((n, half), F32)
    ck = jnp.concatenate([lo, cos, cos, hi], axis=1)
    cq = jnp.concatenate([lo + 1.0, cos, cos, hi], axis=1)
    s1 = jnp.concatenate([lo, zh, sin, hi], axis=1)
    s2 = jnp.concatenate([lo, -sin, zh, hi], axis=1)
    return cq, ck, s1, s2


def _segment_edges(proj, nseq, tseq, tq):
    if tseq // tq == 1:
        return None
    edges = proj.reshape(nseq, tseq // tq, tq, N_PROJ)[:, :-1, tq - 2:, :]
    return lax.optimization_barrier(edges)


def _prev_row_halos(edges, first_rows, nseq, tseq, tq):
    nseg_per = tseq // tq
    out = {}
    for name, (c0, c1) in {"r": (COL_R, COL_R + D_BR), "k": (COL_K, COL_K + D_BR),
                           "v": (COL_V, COL_V + D_BR), "lora": (COL_LORA, COL_LORA + LANES)}.items():
        first = first_rows[name].reshape(nseq, 1, c1 - c0)
        if nseg_per > 1:
            first = jnp.concatenate([first, edges[:, :, 1, c0:c1]], axis=1)
        out[name] = first.reshape(nseq * nseg_per, 1, c1 - c0)
    return out


def _conv_halos(edges, conv_prev, nseq, tseq, tq):
    nseg_per = tseq // tq
    hc = conv_prev
    hx = jnp.ones_like(conv_prev)
    if nseg_per > 1:
        hc = jnp.concatenate([hc[:, None], edges[..., COL_CC:COL_CC + D_BR]], axis=1)
        hx = jnp.concatenate([hx[:, None], edges[..., COL_CX:COL_CX + D_BR]], axis=1)
    return hc.reshape(nseq * nseg_per, 2, D_BR), hx.reshape(nseq * nseg_per, 2, D_BR)


def _layer(x, lw, tabs, nseq, tseq, shift_prev, conv_prev, wkv_prev, attn):
    t = nseq * tseq
    proj = _inproj(x, lw["norm_pre"], lw["w_in"])
    if tseq >= ROW_TILE:
        nb, tq = 1, ROW_TILE
    else:
        nb, tq = min(nseq, ROW_TILE // tseq), tseq
    first = {"r": shift_prev[:, :D_BR], "k": shift_prev[:, D_BR:2 * D_BR],
             "v": shift_prev[:, 2 * D_BR:3 * D_BR], "lora": shift_prev[:, 3 * D_BR:]}
    chunked = tseq % min(WKV_CHUNK, tseq) == 0
    edges = _segment_edges(proj, nseq, tseq, tq)
    ops = _rwkv_prep(proj, _prev_row_halos(edges, first, nseq, tseq, tq), lw, nb, tq, chunked)
    if chunked:
        bonus = ops[6]
        vkr = bonus
        w6 = wkv_prev.reshape(nseq, N_LANE_GROUPS, 2, HEAD_A, HEAD_A)
        s0 = jnp.zeros((nseq, N_LANE_GROUPS, 2, HEAD_A, 2, HEAD_A), F32)
        s0 = s0.at[:, :, 0, :, 0, :].set(w6[:, :, 0]).at[:, :, 1, :, 1, :].set(w6[:, :, 1])
        y, st = _wkv_chunked(ops[:6], s0.reshape(nseq, N_LANE_GROUPS, LANES, LANES), nseq, tseq)
        st6 = st.reshape(nseq, N_LANE_GROUPS, 2, HEAD_A, 2, HEAD_A)
        wkv_new = jnp.stack([st6[:, :, 0, :, 0, :], st6[:, :, 1, :, 1, :]], axis=2).reshape(
            nseq, H_A, HEAD_A, HEAD_A)
    else:
        vkr, bonus = ops[7], ops[8]
        s0 = jnp.transpose(wkv_prev, (0, 2, 1, 3)).reshape(nseq, HEAD_A, D_BR)
        y, st = _wkv_scan(ops[:7], s0, nseq, tseq)
        wkv_new = jnp.transpose(st.reshape(nseq, HEAD_A, H_A, HEAD_A), (0, 2, 1, 3))
    o, ckv, kr128 = attn(proj, lw, tabs)
    hc, hx = _conv_halos(edges, conv_prev, nseq, tseq, tq)
    x_new, zt = _merge(y, vkr, bonus, x, o, proj, hc, hx, lw, nb, tq)
    p3 = proj.reshape(nseq, tseq, N_PROJ)[:, -1, :]
    new_shift = jnp.concatenate([p3[:, :3 * D_BR], p3[:, COL_LORA:COL_LORA + LANES]], axis=1)
    new_conv = zt.reshape(nseq, tseq // tq, -1, D_BR)[:, -1, -2:, :]
    kr = kr128[:, ROPE_LANE0:ROPE_LANE0 + QK_ROPE]
    return x_new, (ckv.reshape(nseq, tseq, KV_LORA), kr.reshape(nseq, tseq, QK_ROPE), wkv_new,
                   new_shift, new_conv)


def kernel(x_prompt, x_sample, cache_ckv, cache_krope, state_wkv, state_shift, state_conv, page_table,
           norm_pre, norm_post, w_in, mu_shift, w0, w2, a0, a2, k_k, k_a, r_k, ln_x_w, ln_x_b,
           conv_w, q_a_norm, w_uq, kv_a_norm, w_uk, w_uv, w_branch, w_out):
    params = dict(norm_pre=norm_pre, norm_post=norm_post, w_in=w_in, mu_shift=mu_shift, w0=w0, w2=w2,
                  a0=a0, a2=a2, k_k=k_k, k_a=k_a, r_k=r_k.reshape(r_k.shape[0], -1), ln_x_w=ln_x_w,
                  ln_x_b=ln_x_b, conv_w=conv_w, q_a_norm=q_a_norm, w_uq=w_uq, kv_a_norm=kv_a_norm,
                  w_uk=w_uk, w_uv=w_uv, w_branch=w_branch, w_out=w_out)
    bp, sp, _ = x_prompt.shape
    bd, sd, _ = x_sample.shape
    depth = w_in.shape[0]
    past_len = page_table.shape[1] * cache_ckv.shape[2]
    tabs_p = _rope_tables(jnp.arange(sp))
    tm_s = min(bd * sd, 256)
    tabs_s = _rope_tables(past_len + (jnp.arange(tm_s) % sd))
    cache_krope = jnp.swapaxes(cache_krope, 2, 3)

    xp = x_prompt.reshape(bp * sp, D_MODEL)
    xs = x_sample.reshape(bd * sd, D_MODEL)
    acc_p, acc_s = [], []
    for l in range(depth):
        lw = _layer_weights(l, params)

        def attn_prompt(proj, lw, tabs):
            q, k, v, ckv, kr128 = _mla_prep_prompt(proj, lw, tabs, sp)
            return _flash_prompt(q, k, v, bp, sp), ckv, kr128

        def attn_sample(proj, lw, tabs, l=l):
            ql, qp, ckv, kr128 = _mla_prep_sample(proj, lw, tabs)
            o = _paged_attention(ql, qp, cache_ckv, cache_krope, l, ckv, kr128, lw["wv_heads"],
                                 page_table, sd)
            return o, ckv, kr128

        xp, st_p = _layer(xp, lw, tabs_p, bp, sp, jnp.zeros((bp, state_shift.shape[2]), F32),
                          jnp.zeros((bp, 2, D_BR), F32), jnp.zeros((bp, H_A, HEAD_A, HEAD_A), F32),
                          attn_prompt)
        xs, st_s = _layer(xs, lw, tabs_s, bd, sd, state_shift[l], state_conv[l], state_wkv[l],
                          attn_sample)
        acc_p.append(st_p)
        acc_s.append(st_s)
    outs_p = [jnp.stack([a[i] for a in acc_p], axis=0) for i in range(5)]
    outs_s = [jnp.stack([a[i] for a in acc_s], axis=0) for i in range(5)]
    return (xp.reshape(bp, sp, D_MODEL), xs.reshape(bd, sd, D_MODEL), *outs_p, *outs_s)
```
